```python
import math
import jax, jax.numpy as jnp
from jax import lax
import numpy as np

D_MODEL = 2048
BATCH = 2
SEQ = 4096
DEPTH = 1
DEC_BATCH = 32
DEC_SEQ = 8
PAST_LEN = 8192
PAGE_SIZE = 128

HEAD_DIM = 64
D_ATT = D_MODEL // 2
N_HEADS = D_ATT // (2 * HEAD_DIM)
D_POOL = D_MODEL - D_ATT
POOL_WINDOWS = (2, 4, 8, 16)
N_POOL_GROUPS = len(POOL_WINDOWS)
POOL_GROUP = D_POOL // N_POOL_GROUPS
POOL_STATE_LEN = max(POOL_WINDOWS) - 1
N_EXPERTS = 32
TOP_K = 4
D_FF = D_MODEL
SWIGLU_LIMIT = 7.0
SWIGLU_ALPHA = 1.702
ROPE_THETA = 10000.0
NORM_EPS = 1e-5
Q_BLOCK = 128
MOE_BLOCK = 128
N_ADA = 6

kernel_name = "hybrid_diffattn_pool_moe_adaln_step"

F32 = jnp.float32


def rmsnorm(x, g):
    xf = x.astype(F32)
    y = xf * lax.rsqrt(jnp.mean(xf * xf, axis=-1, keepdims=True) + NORM_EPS) * g.astype(F32)
    return y.astype(x.dtype)


def adaln(c, w_ada, b_ada):
    m = jax.nn.silu(c) @ w_ada + b_ada
    return jnp.split(m[:, None, :], N_ADA, axis=-1)


def modulate(x, g, shift, scale):
    return rmsnorm(x, g) * (1 + scale) + shift


def rope(x, pos):
    inv = 1.0 / (ROPE_THETA ** (jnp.arange(0, HEAD_DIM, 2, dtype=F32) / HEAD_DIM))
    ang = pos.astype(F32)[:, None] * inv[None, :]
    ang = jnp.concatenate([ang, ang], axis=-1)
    cos = jnp.cos(ang)[:, None, None, :]
    sin = jnp.sin(ang)[:, None, None, :]
    xf = x.astype(F32)
    half = HEAD_DIM // 2
    rot = jnp.concatenate([-xf[..., half:], xf[..., :half]], axis=-1)
    return (xf * cos + rot * sin).astype(x.dtype)


def project(h, w_in):
    B, T, _ = h.shape
    z = h @ w_in
    q, k, v, u = jnp.split(z, [D_ATT, 2 * D_ATT, 3 * D_ATT], axis=-1)
    return (q.reshape(B, T, N_HEADS, 2, HEAD_DIM), k.reshape(B, T, N_HEADS, 2, HEAD_DIM),
            v.reshape(B, T, N_HEADS, 2 * HEAD_DIM), u)


def diff_scores(q, k):
    return jnp.einsum('bthmd,blhmd->bhmtl', q, k, preferred_element_type=F32) * (HEAD_DIM ** -0.5)


def diff_weights(s, lam):
    p = jax.nn.softmax(s, axis=-1)
    return p[:, :, 0] - lam * p[:, :, 1]


def diff_attn_prompt(q, k, v, lam):
    B, S = q.shape[0], q.shape[1]
    nb = S // Q_BLOCK
    qb = q.reshape(B, nb, Q_BLOCK, N_HEADS, 2, HEAD_DIM).swapaxes(0, 1)
    kpos = jnp.arange(S)

    def one_block(args):
        qblk, i = args
        qpos = i * Q_BLOCK + jnp.arange(Q_BLOCK)
        mask = kpos[None, :] <= qpos[:, None]
        s = jnp.where(mask, diff_scores(qblk, k), -jnp.inf)
        w = diff_weights(s, lam).astype(v.dtype)
        return jnp.einsum('bhtl,blhe->bthe', w, v)

    o = lax.map(one_block, (qb, jnp.arange(nb)))
    return o.swapaxes(0, 1).reshape(B, S, N_HEADS, 2 * HEAD_DIM)


def diff_attn_sample(q, k_new, v_new, k_past, v_past, lam):
    T = q.shape[1]
    P = k_past.shape[1]
    causal = jnp.tril(jnp.ones((T, T), dtype=bool))
    s = jnp.concatenate([diff_scores(q, k_past),
                         jnp.where(causal, diff_scores(q, k_new), -jnp.inf)], axis=-1)
    w = diff_weights(s, lam).astype(v_new.dtype)
    return (jnp.einsum('bhtl,blhe->bthe', w[..., :P], v_past)
            + jnp.einsum('bhtl,blhe->bthe', w[..., P:], v_new))


def attn_out(o, g_subln, lam_init):
    B, T = o.shape[0], o.shape[1]
    return (rmsnorm(o, g_subln) * (1.0 - lam_init)).reshape(B, T, D_ATT)


def pool_mix(u, pos, w_pool, scale):
    B, L, _ = u.shape
    ug = u.astype(F32).reshape(B, L, N_POOL_GROUPS, POOL_GROUP)
    cs = jnp.cumsum(ug, axis=1)
    means = []
    for g, w in enumerate(POOL_WINDOWS):
        c = cs[:, :, g]
        lag = jnp.pad(c, ((0, 0), (w, 0), (0, 0)))[:, :L]
        cnt = jnp.minimum(pos + 1, w).astype(F32)[None, :, None]
        means.append((c - lag) / cnt)
    d = jnp.stack(means, axis=2) - ug
    y = jnp.einsum('blgc,gcd->blgd', d, w_pool.astype(F32)).reshape(B, L, D_POOL) * scale.astype(F32)
    return y.astype(u.dtype)


def moe(h, l, w_router, b_router, w_gate, b_gate, w_up, b_up, w_down, b_down):
    shp = h.shape
    hf = h.reshape(-1, shp[-1])
    N = hf.shape[0]
    NK = N * TOP_K
    logits = jnp.dot(hf, w_router[l], preferred_element_type=F32) + b_router[l].astype(F32)
    top_val, top_idx = lax.top_k(logits, TOP_K)
    gates = jax.nn.softmax(top_val, axis=-1)
    flat_e = top_idx.reshape(-1).astype(jnp.int32)
    order = jnp.argsort(flat_e, stable=True).astype(jnp.int32)
    sorted_e = flat_e[order]
    sorted_tok = order // TOP_K
    counts = jnp.bincount(flat_e, length=N_EXPERTS).astype(jnp.int32)
    padded = (counts + MOE_BLOCK - 1) // MOE_BLOCK * MOE_BLOCK
    start = jnp.cumsum(counts) - counts
    pend = jnp.cumsum(padded)
    pstart = pend - padded
    dest = (pstart[sorted_e] + jnp.arange(NK, dtype=jnp.int32) - start[sorted_e]).astype(jnp.int32)
    n_blocks = -(-(NK + N_EXPERTS * (MOE_BLOCK - 1)) // MOE_BLOCK)
    R = n_blocks * MOE_BLOCK
    row_tok = jnp.full((R,), N, jnp.int32).at[dest].set(sorted_tok)
    xs = jnp.take(hf, row_tok, axis=0, mode='fill', fill_value=0)
    block_e = jnp.minimum(jnp.searchsorted(pend, jnp.arange(n_blocks, dtype=jnp.int32) * MOE_BLOCK,
                                           side='right'), N_EXPERTS - 1)

    def expert_block(args):
        xb, e = args
        gate = xb @ w_gate[l, e] + b_gate[l, e]
        up = xb @ w_up[l, e] + b_up[l, e]
        gate = jnp.minimum(gate, SWIGLU_LIMIT)
        up = jnp.clip(up, -SWIGLU_LIMIT, SWIGLU_LIMIT)
        act = (up + 1) * (gate * jax.nn.sigmoid(SWIGLU_ALPHA * gate))
        return act @ w_down[l, e] + b_down[l, e]

    ys = lax.map(expert_block, (xs.reshape(n_blocks, MOE_BLOCK, -1), block_e)).reshape(R, -1)
    pair_row = jnp.zeros((NK,), jnp.int32).at[order].set(dest)
    y_pairs = ys[pair_row].reshape(N, TOP_K, -1)
    y = jnp.einsum('nk,nkd->nd', gates.astype(h.dtype), y_pairs)
    return y.reshape(shp)


def setup_inputs(seed: int = 0) -> dict:
    key = jax.random.key(seed)
    ks = jax.random.split(key, 32)
    n_pages = PAST_LEN // PAGE_SIZE
    n_used = DEC_BATCH * n_pages
    n_pool = (n_used * 5 + 3) // 4
    d_in = 3 * D_ATT + D_POOL

    def nrm(k, shape, s):
        return jax.random.normal(k, shape, F32) * s

    page_table = jax.random.permutation(ks[0], n_pool)[:n_used].reshape(DEC_BATCH, n_pages).astype(jnp.int32)
    return {
        "x_prompt": nrm(ks[1], (BATCH, SEQ, D_MODEL), 1.0),
        "x_sample": nrm(ks[2], (DEC_BATCH, DEC_SEQ, D_MODEL), 1.0),
        "cache_k": nrm(ks[3], (DEPTH, n_pool, PAGE_SIZE, N_HEADS, 2 * HEAD_DIM), 1.0),
        "cache_v": nrm(ks[4], (DEPTH, n_pool, PAGE_SIZE, N_HEADS, 2 * HEAD_DIM), 1.0),
        "state_pool": nrm(ks[5], (DEPTH, DEC_BATCH, POOL_STATE_LEN, D_POOL), 1.0),
        "page_table": page_table,
        "c_prompt": nrm(ks[6], (BATCH, D_MODEL), 1.0),
        "c_sample": nrm(ks[7], (DEC_BATCH, D_MODEL), 1.0),
        "w_ada": nrm(ks[8], (DEPTH, D_MODEL, N_ADA * D_MODEL), 0.5 * D_MODEL ** -0.5),
        "b_ada": nrm(ks[9], (DEPTH, N_ADA * D_MODEL), 0.02),
        "norm1_g": 1.0 + nrm(ks[10], (DEPTH, D_MODEL), 0.02),
        "norm2_g": 1.0 + nrm(ks[11], (DEPTH, D_MODEL), 0.02),
        "w_in": nrm(ks[12], (DEPTH, D_MODEL, d_in), D_MODEL ** -0.5),
        "lam_q1": nrm(ks[13], (DEPTH, HEAD_DIM), 0.1),
        "lam_k1": nrm(ks[14], (DEPTH, HEAD_DIM), 0.1),
        "lam_q2": nrm(ks[15], (DEPTH, HEAD_DIM), 0.1),
        "lam_k2": nrm(ks[16], (DEPTH, HEAD_DIM), 0.1),
        "subln_g": 1.0 + nrm(ks[17], (DEPTH, 2 * HEAD_DIM), 0.02),
        "w_pool": nrm(ks[18], (DEPTH, N_POOL_GROUPS, POOL_GROUP, POOL_GROUP), POOL_GROUP ** -0.5),
        "pool_scale": 1.0 + nrm(ks[19], (DEPTH, D_POOL), 0.1),
        "w_out": nrm(ks[20], (DEPTH, D_ATT + D_POOL, D_MODEL), (D_ATT + D_POOL) ** -0.5),
        "w_router": nrm(ks[21], (DEPTH, D_MODEL, N_EXPERTS), D_MODEL ** -0.5),
        "b_router": nrm(ks[22], (DEPTH, N_EXPERTS), 0.01),
        "w_gate": nrm(ks[23], (DEPTH, N_EXPERTS, D_MODEL, D_FF), D_MODEL ** -0.5),
        "b_gate": nrm(ks[24], (DEPTH, N_EXPERTS, D_FF), 0.01),
        "w_up": nrm(ks[25], (DEPTH, N_EXPERTS, D_MODEL, D_FF), D_MODEL ** -0.5),
        "b_up": nrm(ks[26], (DEPTH, N_EXPERTS, D_FF), 0.01),
        "w_down": nrm(ks[27], (DEPTH, N_EXPERTS, D_FF, D_MODEL), D_FF ** -0.5),
        "b_down": nrm(ks[28], (DEPTH, N_EXPERTS, D_MODEL), 0.01),
        "final_g": 1.0 + nrm(ks[29], (D_MODEL,), 0.02),
    }


def reference(x_prompt, x_sample, cache_k, cache_v, state_pool, page_table, c_prompt, c_sample,
              w_ada, b_ada, norm1_g, norm2_g, w_in, lam_q1, lam_k1, lam_q2, lam_k2, subln_g,
              w_pool, pool_scale, w_out, w_router, b_router, w_gate, b_gate, w_up, b_up,
              w_down, b_down, final_g):
    B, S, _ = x_prompt.shape
    DB, T, _ = x_sample.shape
    past = page_table.shape[1] * cache_k.shape[2]
    pos_p = jnp.arange(S)
    pos_s = past + jnp.arange(T)
    pos_ext = past - POOL_STATE_LEN + jnp.arange(POOL_STATE_LEN + T)
    xp, xs = x_prompt, x_sample
    kp_l, vp_l, pp_l, ks_l, vs_l, ps_l = [], [], [], [], [], []
    for l in range(DEPTH):
        lam_init = 0.8 - 0.6 * math.exp(-0.3 * l)
        lam = (jnp.exp(jnp.sum(lam_q1[l].astype(F32) * lam_k1[l].astype(F32)))
               - jnp.exp(jnp.sum(lam_q2[l].astype(F32) * lam_k2[l].astype(F32))) + lam_init)
        mp = adaln(c_prompt, w_ada[l], b_ada[l])
        ms = adaln(c_sample, w_ada[l], b_ada[l])

        q, k, v, u = project(modulate(xp, norm1_g[l], mp[0], mp[1]), w_in[l])
        q, k = rope(q, pos_p), rope(k, pos_p)
        a = attn_out(diff_attn_prompt(q, k, v, lam), subln_g[l], lam_init)
        pm = pool_mix(u, pos_p, w_pool[l], pool_scale[l])
        xp = xp + mp[2] * (jnp.concatenate([a, pm], axis=-1) @ w_out[l])
        kp_l.append(k.reshape(B, S // PAGE_SIZE, PAGE_SIZE, N_HEADS, 2 * HEAD_DIM))
        vp_l.append(v.reshape(B, S // PAGE_SIZE, PAGE_SIZE, N_HEADS, 2 * HEAD_DIM))
        pp_l.append(u[:, S - POOL_STATE_LEN:])

        q, k, v, u = project(modulate(xs, norm1_g[l], ms[0], ms[1]), w_in[l])
        q, k = rope(q, pos_s), rope(k, pos_s)
        k_past = cache_k[l, page_table].reshape(DB, past, N_HEADS, 2, HEAD_DIM).astype(k.dtype)
        v_past = cache_v[l, page_table].reshape(DB, past, N_HEADS, 2 * HEAD_DIM).astype(v.dtype)
        a = attn_out(diff_attn_sample(q, k, v, k_past, v_past, lam), subln_g[l], lam_init)
        u_ext = jnp.concatenate([state_pool[l].astype(u.dtype), u], axis=1)
        pm = pool_mix(u_ext, pos_ext, w_pool[l], pool_scale[l])[:, POOL_STATE_LEN:]
        xs = xs + ms[2] * (jnp.concatenate([a, pm], axis=-1) @ w_out[l])
        ks_l.append(k.reshape(DB, T, N_HEADS, 2 * HEAD_DIM))
        vs_l.append(v)
        ps_l.append(u_ext[:, -POOL_STATE_LEN:])

        xp = xp + mp[5] * moe(modulate(xp, norm2_g[l], mp[3], mp[4]), l, w_router, b_router,
                              w_gate, b_gate, w_up, b_up, w_down, b_down)
        xs = xs + ms[5] * moe(modulate(xs, norm2_g[l], ms[3], ms[4]), l, w_router, b_router,
                              w_gate, b_gate, w_up, b_up, w_down, b_down)

    y_prompt = rmsnorm(xp, final_g)
    y_sample = rmsnorm(xs, final_g)
    return (y_prompt, y_sample, jnp.stack(kp_l), jnp.stack(vp_l), jnp.stack(pp_l),
            jnp.stack(ks_l), jnp.stack(vs_l), jnp.stack(ps_l))
```

```python
import functools
import math

import jax
import jax.numpy as jnp
from jax import lax
from jax.experimental import pallas as pl
from jax.experimental.pallas import tpu as pltpu

F32 = jnp.float32
BF16 = jnp.bfloat16
I32 = jnp.int32

HEAD_DIM = 64
HEAD_W = 2 * HEAD_DIM
POOL_WINDOWS = (2, 4, 8, 16)
POOL_HALO = 16
TOP_K = 4
SWIGLU_LIMIT = 7.0
SWIGLU_ALPHA = 1.702
ROPE_THETA = 10000.0
NORM_EPS = 1e-5
N_ADA = 6
LANES = 128
SUBLANES = 8
V7X_VMEM_LIMIT = 58 * 1024 * 1024

TOK_TILE = 256
EXP_TILE = 256
EXP_GROUP = 6
FF_TILE = 512


def _cparams(sem, vmem=V7X_VMEM_LIMIT):
    return pltpu.CompilerParams(dimension_semantics=sem, vmem_limit_bytes=vmem)


def _cast_kernel(x_ref, o_ref):
    o_ref[...] = x_ref[...].astype(o_ref.dtype)


def _cast_bf16(w, rows):
    r, c = w.shape
    return pl.pallas_call(
        _cast_kernel,
        out_shape=jax.ShapeDtypeStruct((r, c), BF16),
        grid=(r // rows,),
        in_specs=[pl.BlockSpec((rows, c), lambda i: (i, 0))],
        out_specs=pl.BlockSpec((rows, c), lambda i: (i, 0)),
        compiler_params=_cparams(("arbitrary",)),
        name="cast_bf16",
    )(w)


def _ada_kernel(c_ref, w_ref, b_ref, o_ref):
    c = c_ref[...]
    s = (c * jax.nn.sigmoid(c)).astype(BF16)
    o_ref[...] = jnp.dot(s, w_ref[...].astype(BF16), preferred_element_type=F32) + b_ref[...]


def _adaln(c_all, w_ada, b_ada):
    rows, d = c_all.shape
    n = w_ada.shape[1]
    tn = min(1024, n)
    return pl.pallas_call(
        _ada_kernel,
        out_shape=jax.ShapeDtypeStruct((rows, n), F32),
        grid=(n // tn,),
        in_specs=[pl.BlockSpec((rows, d), lambda j: (0, 0)),
                  pl.BlockSpec((d, tn), lambda j: (0, j)),
                  pl.BlockSpec((1, tn), lambda j: (0, j))],
        out_specs=pl.BlockSpec((rows, tn), lambda j: (0, j)),
        compiler_params=_cparams(("arbitrary",)),
        name="adaln",
    )(c_all, w_ada, b_ada.reshape(1, n))


def _modulated_norm(x, g, shift, scale):
    ms = jnp.mean(x * x, axis=-1, keepdims=True)
    return (x * lax.rsqrt(ms + NORM_EPS) * g) * (1.0 + scale) + shift


def _proj_kernel(x_ref, shift_ref, scale_ref, g_ref, w_ref, cos_ref, sin_ref,
                 q_ref, k_ref, v_ref, u_ref, kb_ref, vb_ref, *, d_att):
    h = _modulated_norm(x_ref[...], g_ref[...], shift_ref[0], scale_ref[0]).astype(BF16)
    cos = cos_ref[...]
    sin = sin_ref[...]
    lane = lax.broadcasted_iota(I32, cos.shape, 1)
    first_half = (lane & (HEAD_DIM - 1)) < (HEAD_DIM // 2)

    def rope(z):
        rot = jnp.where(first_half, -pltpu.roll(z, LANES - HEAD_DIM // 2, 1), pltpu.roll(z, HEAD_DIM // 2, 1))
        return z * cos + rot * sin

    zq = jnp.dot(h, w_ref[:, 0:d_att], preferred_element_type=F32)
    zk = jnp.dot(h, w_ref[:, d_att:2 * d_att], preferred_element_type=F32)
    for hh in range(d_att // HEAD_W):
        sl = slice(hh * HEAD_W, (hh + 1) * HEAD_W)
        q_ref[:, sl] = rope(zq[:, sl]) * (HEAD_DIM ** -0.5)
        kr = rope(zk[:, sl])
        k_ref[:, sl] = kr
        kb_ref[:, sl] = kr.astype(BF16)
    zv = jnp.dot(h, w_ref[:, 2 * d_att:3 * d_att], preferred_element_type=F32)
    v_ref[...] = zv
    vb_ref[...] = zv.astype(BF16)
    u_ref[...] = jnp.dot(h, w_ref[:, 3 * d_att:], preferred_element_type=F32)


def _project(x, shift, scale, g, w_in_b, cos, sin, *, tiles_per_seq, pos_tiles):
    n, d = x.shape
    tm = min(TOK_TILE, n)
    d_in = w_in_b.shape[1]
    d_att = (d // 2)
    d_pool = d_in - 3 * d_att
    mod_rows = shift.shape[1]
    mod_spec = pl.BlockSpec((1, mod_rows, d), lambda i: (i // tiles_per_seq, 0, 0))
    tok = lambda c: pl.BlockSpec((tm, c), lambda i: (i, 0))
    return pl.pallas_call(
        functools.partial(_proj_kernel, d_att=d_att),
        out_shape=(jax.ShapeDtypeStruct((n, d_att), F32), jax.ShapeDtypeStruct((n, d_att), F32),
                   jax.ShapeDtypeStruct((n, d_att), F32), jax.ShapeDtypeStruct((n, d_pool), F32),
                   jax.ShapeDtypeStruct((n, d_att), BF16), jax.ShapeDtypeStruct((n, d_att), BF16)),
        grid=(n // tm,),
        in_specs=[tok(d), mod_spec, mod_spec,
                  pl.BlockSpec((1, d), lambda i: (0, 0)),
                  pl.BlockSpec((d, d_in), lambda i: (0, 0)),
                  pl.BlockSpec((tm, LANES), lambda i: (i % pos_tiles, 0)),
                  pl.BlockSpec((tm, LANES), lambda i: (i % pos_tiles, 0))],
        out_specs=(tok(d_att), tok(d_att), tok(d_att), tok(d_pool), tok(d_att), tok(d_att)),
        compiler_params=_cparams(("arbitrary",)),
        name="in_proj",
    )(x, shift, scale, g, w_in_b, cos, sin)


def _lambda_value(lam_ref, lam_init):
    lp = lam_ref[...]
    a = jnp.sum(lp[0:1] * lp[1:2], axis=1, keepdims=True)
    b = jnp.sum(lp[2:3] * lp[3:4], axis=1, keepdims=True)
    return jnp.exp(a) - jnp.exp(b) + lam_init


def _stack_maps(q):
    lane = lax.broadcasted_iota(I32, q.shape, 1)
    q1 = jnp.where(lane < HEAD_DIM, q, 0.0)
    q2 = jnp.where(lane >= HEAD_DIM, q, 0.0)
    return jnp.concatenate([q1, q2], axis=0).astype(BF16)


def _softmax_update(s, vb, m_prev, l_prev, acc_prev):
    reps = s.shape[1] // LANES
    m_cur = jnp.max(s, axis=1, keepdims=True)
    m_new = jnp.maximum(m_prev, m_cur)
    m_wide = m_new if reps == 1 else jnp.concatenate([m_new] * reps, axis=1)
    p = jnp.exp(s - m_wide)
    alpha = jnp.exp(m_prev - m_new)
    l_new = alpha * l_prev + jnp.sum(p, axis=1, keepdims=True)
    acc_new = acc_prev * alpha + jnp.dot(p.astype(BF16), vb, preferred_element_type=F32)
    return m_new, l_new, acc_new


def _diff_finish(l, acc, lam, g, t, lam_init):
    o = acc[:t] / l[:t] - lam * (acc[t:] / l[t:])
    ms = jnp.mean(o * o, axis=1, keepdims=True)
    return o * lax.rsqrt(ms + NORM_EPS) * g * (1.0 - lam_init)


_NT = (((1,), (1,)), ((), ()))


def _attn_prompt_kernel(lam_ref, g_ref, q_ref, k_ref, v_ref, o_ref, m_ref, l_ref, acc_ref, *, tq, lam_init):
    qi = pl.program_id(2)
    qq = _stack_maps(q_ref[...])
    m_ref[...] = jnp.full(m_ref.shape, -jnp.inf, F32)
    l_ref[...] = jnp.zeros(l_ref.shape, F32)
    acc_ref[...] = jnp.zeros(acc_ref.shape, F32)

    def block(j, masked):
        rows = pl.ds(pl.multiple_of(j * tq, tq), tq)
        s = lax.dot_general(qq, k_ref[rows, :], _NT, preferred_element_type=F32)
        if masked:
            r = lax.broadcasted_iota(I32, s.shape, 0)
            c = lax.broadcasted_iota(I32, s.shape, 1)
            r = jnp.where(r >= tq, r - tq, r)
            s = jnp.where(c <= r, s, -jnp.inf)
        m, l, acc = _softmax_update(s, v_ref[rows, :], m_ref[...], l_ref[...], acc_ref[...])
        m_ref[...] = m
        l_ref[...] = l
        acc_ref[...] = acc

    def body(j, carry):
        block(j, False)
        return carry

    lax.fori_loop(0, qi, body, 0)
    block(qi, True)
    lam = _lambda_value(lam_ref, lam_init)
    o_ref[...] = _diff_finish(l_ref[...], acc_ref[...], lam, g_ref[...], tq, lam_init).astype(o_ref.dtype)


def _attn_prompt(q, kb, vb, lamp, g, *, batch, seq, lam_init):
    n, d_att = q.shape
    nh = d_att // HEAD_W
    tq = min(256, seq)
    nq = seq // tq
    return pl.pallas_call(
        functools.partial(_attn_prompt_kernel, tq=tq, lam_init=lam_init),
        out_shape=jax.ShapeDtypeStruct((n, d_att), BF16),
        grid=(batch, nh, nq),
        in_specs=[pl.BlockSpec((8, LANES), lambda b, h, i: (0, 0)),
                  pl.BlockSpec((1, HEAD_W), lambda b, h, i: (0, 0)),
                  pl.BlockSpec((tq, HEAD_W), lambda b, h, i: (b * nq + i, h)),
                  pl.BlockSpec((seq, HEAD_W), lambda b, h, i: (b, h)),
                  pl.BlockSpec((seq, HEAD_W), lambda b, h, i: (b, h))],
        out_specs=pl.BlockSpec((tq, HEAD_W), lambda b, h, i: (b * nq + i, h)),
        scratch_shapes=[pltpu.VMEM((2 * tq, HEAD_W), F32)] * 3,
        compiler_params=_cparams(("arbitrary", "arbitrary", "arbitrary")),
        name="attn_prompt",
    )(lamp, g, q, kb, vb)


def _attn_sample_kernel(pt_ref, lam_ref, g_ref, q_ref, kn_ref, vn_ref, *rest, n_heads, pages, t_new, lam_init):
    k_pages = rest[:pages]
    v_pages = rest[pages:2 * pages]
    o_ref = rest[2 * pages]
    kc, vc, m_ref, l_ref, acc_ref = rest[2 * pages + 1:]
    c = pl.program_id(1)
    page = k_pages[0].shape[1]

    @pl.when(c == 0)
    def _():
        m_ref[...] = jnp.full(m_ref.shape, -jnp.inf, F32)
        l_ref[...] = jnp.zeros(l_ref.shape, F32)
        acc_ref[...] = jnp.zeros(acc_ref.shape, F32)

    for i in range(pages):
        for hh in range(n_heads):
            sl = slice(hh * HEAD_W, (hh + 1) * HEAD_W)
            kc[i * page:(i + 1) * page, sl] = k_pages[i][0, :, hh, :].astype(BF16)
            vc[i * page:(i + 1) * page, sl] = v_pages[i][0, :, hh, :].astype(BF16)
    q = q_ref[...]
    qqs = [_stack_maps(q[:, hh * HEAD_W:(hh + 1) * HEAD_W]) for hh in range(n_heads)]

    def update(hh, kh, vh, mask):
        s = lax.dot_general(qqs[hh], kh, _NT, preferred_element_type=F32)
        if mask is not None:
            s = jnp.where(mask, s, -jnp.inf)
        m, l, acc = _softmax_update(s, vh, m_ref[hh], l_ref[hh], acc_ref[hh])
        m_ref[hh] = m
        l_ref[hh] = l
        acc_ref[hh] = acc

    for hh in range(n_heads):
        sl = slice(hh * HEAD_W, (hh + 1) * HEAD_W)
        update(hh, kc[:, sl], vc[:, sl], None)

    @pl.when(c == pl.num_programs(1) - 1)
    def _():
        kc[0:LANES, :] = jnp.zeros((LANES, kc.shape[1]), BF16)
        vc[0:LANES, :] = jnp.zeros((LANES, vc.shape[1]), BF16)
        kc[0:t_new, :] = kn_ref[...].astype(BF16)
        vc[0:t_new, :] = vn_ref[...].astype(BF16)
        r = lax.broadcasted_iota(I32, (2 * t_new, LANES), 0)
        col = lax.broadcasted_iota(I32, (2 * t_new, LANES), 1)
        r = jnp.where(r >= t_new, r - t_new, r)
        mask = col <= r
        lam = _lambda_value(lam_ref, lam_init)
        for hh in range(n_heads):
            sl = slice(hh * HEAD_W, (hh + 1) * HEAD_W)
            update(hh, kc[0:LANES, sl], vc[0:LANES, sl], mask)
            o_ref[:, sl] = _diff_finish(l_ref[hh], acc_ref[hh], lam, g_ref[...], t_new, lam_init)


def _attn_sample(page_table, q, k_new, v_new, cache_k, cache_v, lamp, g, *, t_new, lam_init):
    n, d_att = q.shape
    nh = d_att // HEAD_W
    db, n_pages = page_table.shape
    page = cache_k.shape[1]
    pages = min(8, n_pages)

    def page_spec(i):
        return pl.BlockSpec((1, page, nh, HEAD_W), lambda b, c, pt: (pt[b, c * pages + i], 0, 0, 0))

    tok = pl.BlockSpec((t_new, d_att), lambda b, c, pt: (b, 0))
    grid_spec = pltpu.PrefetchScalarGridSpec(
        num_scalar_prefetch=1,
        grid=(db, n_pages // pages),
        in_specs=[pl.BlockSpec((8, LANES), lambda b, c, pt: (0, 0)),
                  pl.BlockSpec((1, HEAD_W), lambda b, c, pt: (0, 0)),
                  tok, tok, tok]
                 + [page_spec(i) for i in range(pages)] + [page_spec(i) for i in range(pages)],
        out_specs=tok,
        scratch_shapes=[pltpu.VMEM((pages * page, d_att), BF16), pltpu.VMEM((pages * page, d_att), BF16)]
                       + [pltpu.VMEM((nh, 2 * t_new, HEAD_W), F32)] * 3,
    )
    return pl.pallas_call(
        functools.partial(_attn_sample_kernel, n_heads=nh, pages=pages, t_new=t_new, lam_init=lam_init),
        out_shape=jax.ShapeDtypeStruct((n, d_att), F32),
        grid_spec=grid_spec,
        compiler_params=_cparams(("arbitrary", "arbitrary")),
        name="attn_sample",
    )(page_table, lamp, g, q, k_new, v_new, *([cache_k] * pages), *([cache_v] * pages))


def _pool_kernel(halo_ref, cur_ref, w_ref, scale_ref, o_ref, ext_ref, *, pos0, tile_pos, zero_first_halo):
    i = pl.program_id(1)
    t = cur_ref.shape[1]
    halo = halo_ref[0]
    if zero_first_halo:
        halo = jnp.where(i == 0, 0.0, halo)
    cur = cur_ref[0]
    ext_ref[0:POOL_HALO, :] = halo
    ext_ref[POOL_HALO:POOL_HALO + t, :] = cur
    pos = pos0 + i * tile_pos + lax.broadcasted_iota(I32, (t, 1), 0)
    gw = cur.shape[1] // len(POOL_WINDOWS)
    for gi, w in enumerate(POOL_WINDOWS):
        cols = slice(gi * gw, (gi + 1) * gw)
        total = cur[:, cols]
        for j in range(1, w):
            total = total + ext_ref[POOL_HALO - j:POOL_HALO - j + t, cols]
        cnt = jnp.minimum(pos + 1, w).astype(F32)
        dlt = total / cnt - cur[:, cols]
        y = jnp.dot(dlt.astype(BF16), w_ref[gi], preferred_element_type=F32)
        o_ref[0, :, cols] = (y * scale_ref[:, cols]).astype(o_ref.dtype)


def _pool_mix(u3, w_pool_b, scale, *, tile, cur_block0, pos0, zero_first_halo):
    b, rows, c = u3.shape
    n_tiles = (rows - cur_block0 * tile) // tile
    ng, gw = w_pool_b.shape[0], w_pool_b.shape[1]
    assert (cur_block0 * tile) % POOL_HALO == 0 and (tile % POOL_HALO == 0 or n_tiles == 1)

    def halo_map(bi, i):
        return (bi, jnp.maximum(((cur_block0 + i) * tile) // POOL_HALO - 1, 0), 0)

    return pl.pallas_call(
        functools.partial(_pool_kernel, pos0=pos0, tile_pos=tile, zero_first_halo=zero_first_halo),
        out_shape=jax.ShapeDtypeStruct((b, n_tiles * tile, c), BF16),
        grid=(b, n_tiles),
        in_specs=[pl.BlockSpec((1, POOL_HALO, c), halo_map),
                  pl.BlockSpec((1, tile, c), lambda bi, i: (bi, cur_block0 + i, 0)),
                  pl.BlockSpec((ng, gw, gw), lambda bi, i: (0, 0, 0)),
                  pl.BlockSpec((1, c), lambda bi, i: (0, 0))],
        out_specs=pl.BlockSpec((1, tile, c), lambda bi, i: (bi, i, 0)),
        scratch_shapes=[pltpu.VMEM((POOL_HALO + tile, c), F32)],
        compiler_params=_cparams(("arbitrary", "arbitrary")),
        name="pool_mix",
    )(u3, u3, w_pool_b, scale)


def _split_bf16(x):
    hi = x.astype(BF16)
    lo = (x - hi.astype(F32)).astype(BF16)
    return hi, lo


def _round_up_f32(x, m):
    return jnp.floor((x + (m - 1.0)) * (1.0 / m)) * m


def _mix_out_kernel(*refs, n_experts, aliased):
    (x_ref, a_ref, pm_ref, gate_ref, shift_ref, scale_ref, g_ref, wo_ref, wr_ref, br_ref) = refs[:10]
    x2_ref, h2_ref, pk_ref, gt_ref, tc_ref = refs[10 + aliased:]
    d_att = a_ref.shape[1]
    mix = (jnp.dot(a_ref[...].astype(BF16), wo_ref[0:d_att, :], preferred_element_type=F32)
           + jnp.dot(pm_ref[...], wo_ref[d_att:, :], preferred_element_type=F32))
    x2 = x_ref[...] + gate_ref[0] * mix
    x2_ref[...] = x2
    h2 = _modulated_norm(x2, g_ref[...], shift_ref[0], scale_ref[0])
    h2_ref[...] = h2.astype(BF16)
    hh, hl = _split_bf16(h2)
    wh, wl = _split_bf16(wr_ref[...])
    logits = (jnp.dot(hh, wh, preferred_element_type=F32) + jnp.dot(hl, wh, preferred_element_type=F32)
              + jnp.dot(hh, wl, preferred_element_type=F32)) + br_ref[...]
    t = logits.shape[0]
    lane = lax.broadcasted_iota(I32, logits.shape, 1)
    lanef = lane.astype(F32)
    work = jnp.where(lane < n_experts, logits, -jnp.inf)
    vals, ids = [], []
    for _ in range(TOP_K):
        mx = jnp.max(work, axis=1, keepdims=True)
        ix = jnp.min(jnp.where(work == mx, lanef, float(LANES)), axis=1, keepdims=True)
        vals.append(mx)
        ids.append(ix)
        work = jnp.where(lanef == ix, -jnp.inf, work)
    es = [jnp.exp(v - vals[0]) for v in vals]
    den = es[0]
    for e in es[1:]:
        den = den + e
    sel = jnp.zeros(logits.shape, F32)
    for k in range(TOP_K):
        sel = jnp.where(lanef == ids[k], 1.0, sel)
    r = lax.broadcasted_iota(I32, (t, t), 0)
    c = lax.broadcasted_iota(I32, (t, t), 1)
    earlier = jnp.where(c < r, 1.0, 0.0).astype(BF16)
    local_rank = jnp.dot(earlier, sel.astype(BF16), preferred_element_type=F32)
    cnt = jnp.sum(sel, axis=0, keepdims=True)
    cnt8 = jnp.broadcast_to(_round_up_f32(cnt, float(SUBLANES)), (SUBLANES, LANES))
    er = lax.broadcasted_iota(I32, (LANES, LANES), 0)
    ec = lax.broadcasted_iota(I32, (LANES, LANES), 1)
    before = jnp.where(er < ec, 1.0, 0.0).astype(BF16)
    chunk_off = jnp.dot(cnt8.astype(BF16), before, preferred_element_type=F32)[0:1, :]
    pos = local_rank + chunk_off
    pk_out = jnp.zeros(logits.shape, F32)
    gt_out = jnp.zeros(logits.shape, F32)
    for k in range(TOP_K):
        pk = jnp.sum(jnp.where(lanef == ids[k], pos, 0.0), axis=1, keepdims=True)
        pk_out = jnp.where(lane == k, pk, pk_out)
        gt_out = jnp.where(lane == k, es[k] / den, gt_out)
    pk_ref[...] = pk_out
    gt_ref[...] = gt_out
    tc_ref[0] = jnp.broadcast_to(cnt, (SUBLANES, LANES))


def _mix_out(x, a, pm, gate, shift, scale, g, w_out_b, w_router_p, b_router_p, *, n_total, tile0,
             tiles_per_seq, n_experts, prev=None):
    n, d = x.shape
    tm = min(TOK_TILE, n)
    d_att = a.shape[1]
    mod_rows = gate.shape[1]
    mod_spec = pl.BlockSpec((1, mod_rows, d), lambda i: (i // tiles_per_seq, 0, 0))
    tok = lambda c: pl.BlockSpec((tm, c), lambda i: (i, 0))
    out_tok = lambda c: pl.BlockSpec((tm, c), lambda i: (tile0 + i, 0))
    out_shape = (jax.ShapeDtypeStruct((n_total, d), F32), jax.ShapeDtypeStruct((n_total, d), BF16),
                 jax.ShapeDtypeStruct((n_total, LANES), F32), jax.ShapeDtypeStruct((n_total, LANES), F32),
                 jax.ShapeDtypeStruct((n_total // tm, SUBLANES, LANES), F32))
    in_specs = [tok(d), tok(d_att), tok(pm.shape[1]), mod_spec, mod_spec, mod_spec,
                pl.BlockSpec((1, d), lambda i: (0, 0)),
                pl.BlockSpec(w_out_b.shape, lambda i: (0, 0)),
                pl.BlockSpec(w_router_p.shape, lambda i: (0, 0)),
                pl.BlockSpec((1, LANES), lambda i: (0, 0))]
    args = [x, a, pm, gate, shift, scale, g, w_out_b, w_router_p, b_router_p]
    aliases = {}
    n_alias = 0
    if prev is not None:
        n_alias = len(prev)
        in_specs += [pl.BlockSpec(memory_space=pl.ANY)] * n_alias
        aliases = {len(args) + k: k for k in range(n_alias)}
        args += list(prev)
    return pl.pallas_call(
        functools.partial(_mix_out_kernel, n_experts=n_experts, aliased=n_alias),
        out_shape=out_shape,
        grid=(n // tm,),
        in_specs=in_specs,
        out_specs=(out_tok(d), out_tok(d), out_tok(LANES), out_tok(LANES),
                   pl.BlockSpec((1, SUBLANES, LANES), lambda i: (tile0 + i, 0, 0))),
        input_output_aliases=aliases,
        compiler_params=_cparams(("arbitrary",)),
        name="mix_out_router",
    )(*args)


def _chunk_sizes(tile):
    sizes = []
    s = tile
    while s >= SUBLANES:
        sizes.append(s)
        s //= 2
    return sizes


def _for_each_chunk_piece(cnt_ref, base_ref, i, n_experts, tile, fn):
    def per_expert(e, off):
        l8 = (cnt_ref[i * n_experts + e] + (SUBLANES - 1)) & (-SUBLANES)
        dst0 = base_ref[i * n_experts + e]
        done = jnp.int32(0)
        for sz in _chunk_sizes(tile):
            take = l8 & sz

            @pl.when(take != 0)
            def _(done=done, sz=sz):
                fn(pl.multiple_of(off + done, SUBLANES), pl.multiple_of(dst0 + done, SUBLANES), sz)

            done = done + take
        return off + l8

    return lax.fori_loop(0, n_experts, per_expert, jnp.int32(0))


def _one_hot_rows(pk, rows, weights=None):
    t = pk.shape[0]
    col = lax.broadcasted_iota(I32, (t, rows), 1).astype(F32)
    out = jnp.zeros((t, rows), F32)
    for k in range(TOP_K):
        w = 1.0 if weights is None else weights[:, k:k + 1]
        out = jnp.where(col == pk[:, k:k + 1], w, out)
    return out.astype(BF16)


def _dispatch_kernel(cnt_ref, base_ref, tail_ref, has_ref, h_ref, pk_ref, xs_hbm, sorted_ref, zero_ref, sem,
                     *, n_experts, tile):
    i = pl.program_id(0)

    @pl.when(i == 0)
    def _():
        zero_ref[...] = jnp.zeros(zero_ref.shape, F32)
        for e in range(n_experts):
            @pl.when(has_ref[e] == 1)
            def _(e=e):
                cp = pltpu.make_async_copy(
                    zero_ref, xs_hbm.at[pl.ds(pl.multiple_of(tail_ref[e], EXP_TILE), EXP_TILE)], sem)
                cp.start()
                cp.wait()

    onehot = _one_hot_rows(pk_ref[...], sorted_ref.shape[0])
    sorted_ref[...] = lax.dot_general(onehot, h_ref[...], (((0,), (0,)), ((), ())), preferred_element_type=F32)

    def copy(src, dst, sz):
        return pltpu.make_async_copy(sorted_ref.at[pl.ds(src, sz)], xs_hbm.at[pl.ds(dst, sz)], sem)

    _for_each_chunk_piece(cnt_ref, base_ref, i, n_experts, tile, lambda s, d, sz: copy(s, d, sz).start())
    _for_each_chunk_piece(cnt_ref, base_ref, i, n_experts, tile, lambda s, d, sz: copy(s, d, sz).wait())


def _sorted_rows(tile, n_experts):
    rows = tile * TOP_K + n_experts * (SUBLANES - 1)
    return -(-rows // LANES) * LANES


def _dispatch(h2, pk4, cnt_flat, base_flat, tail_rows, has_tile, *, r_max, n_experts):
    n, d = h2.shape
    tile = min(TOK_TILE, n)
    grid_spec = pltpu.PrefetchScalarGridSpec(
        num_scalar_prefetch=4,
        grid=(n // tile,),
        in_specs=[pl.BlockSpec((tile, d), lambda i, *_: (i, 0)),
                  pl.BlockSpec((tile, LANES), lambda i, *_: (i, 0))],
        out_specs=pl.BlockSpec(memory_space=pl.ANY),
        scratch_shapes=[pltpu.VMEM((_sorted_rows(tile, n_experts), d), F32),
                        pltpu.VMEM((EXP_TILE, d), F32), pltpu.SemaphoreType.DMA],
    )
    return pl.pallas_call(
        functools.partial(_dispatch_kernel, n_experts=n_experts, tile=tile),
        out_shape=jax.ShapeDtypeStruct((r_max, d), F32),
        grid_spec=grid_spec,
        compiler_params=_cparams(("arbitrary",)),
        name="dispatch_rows",
    )(cnt_flat, base_flat, tail_rows, has_tile, h2, pk4)


def _expert_kernel(st_e, st_j, st_n, g_t0, g_gs, g_valid,
                   xs_hbm, wg_ref, wu_ref, wd_ref, bg_ref, bu_ref, bd_ref, ys_hbm,
                   xbuf, actbuf, wgb, wub, wdb, stage_in, stage_out, sem_in, sem_out, *, n_j, n_n, tm, tf):
    s = pl.program_id(0)
    steps = n_j + n_n
    q = s // steps
    ph = s - q * steps
    valid = g_valid[q] == 1
    gs = g_gs[q]
    row0 = g_t0[q] * tm

    @pl.when(jnp.logical_and(valid, ph == 0))
    def _():
        def load(i, carry):
            cp = pltpu.make_async_copy(xs_hbm.at[pl.ds(pl.multiple_of(row0 + i * tm, tm), tm)], stage_in, sem_in)
            cp.start()
            cp.wait()
            xbuf[pl.ds(pl.multiple_of(i * tm, tm), tm), :] = stage_in[...].astype(BF16)
            return carry

        lax.fori_loop(0, gs, load, 0)

    @pl.when(jnp.logical_and(valid, ph < n_j))
    def _():
        wgb[...] = wg_ref[0].astype(BF16)
        wub[...] = wu_ref[0].astype(BF16)

        def tile(i, carry):
            rows = pl.ds(pl.multiple_of(i * tm, tm), tm)
            x = xbuf[rows, :]
            gate = jnp.dot(x, wgb[...], preferred_element_type=F32) + bg_ref[0]
            up = jnp.dot(x, wub[...], preferred_element_type=F32) + bu_ref[0]
            gate = jnp.minimum(gate, SWIGLU_LIMIT)
            up = jnp.clip(up, -SWIGLU_LIMIT, SWIGLU_LIMIT)
            act = (up + 1.0) * (gate * jax.nn.sigmoid(SWIGLU_ALPHA * gate))
            actbuf[ph, rows, :] = act.astype(BF16)
            return carry

        lax.fori_loop(0, gs, tile, 0)

    @pl.when(jnp.logical_and(valid, ph >= n_j))
    def _():
        wdb[...] = wd_ref[0].astype(BF16)
        col0 = pl.multiple_of((ph - n_j) * tf, tf)

        def out_copy(slot, i):
            return pltpu.make_async_copy(
                stage_out.at[slot],
                ys_hbm.at[pl.ds(pl.multiple_of(row0 + i * tm, tm), tm), pl.ds(col0, tf)],
                sem_out.at[slot])

        def tile(i, carry):
            rows = pl.ds(pl.multiple_of(i * tm, tm), tm)
            slot = i & 1

            @pl.when(i >= 2)
            def _():
                out_copy(slot, i).wait()

            y = bd_ref[0] + jnp.dot(actbuf[0, rows, :], wdb[0:tf, :], preferred_element_type=F32)
            for j in range(1, n_j):
                y = y + jnp.dot(actbuf[j, rows, :], wdb[j * tf:(j + 1) * tf, :], preferred_element_type=F32)
            stage_out[slot] = y
            out_copy(slot, i).start()
            return carry

        lax.fori_loop(0, gs, tile, 0)

        @pl.when(gs >= 1)
        def _():
            out_copy((gs - 1) & 1, 0).wait()

        @pl.when(gs >= 2)
        def _():
            out_copy(gs & 1, 0).wait()


def _expert_ffn(tables, xs, w_gate, b_gate, w_up, b_up, w_down, b_down, *, n_groups_max):
    st_e, st_j, st_n, g_t0, g_gs, g_valid = tables
    r_max, d = xs.shape
    n_exp, _, d_ff = w_gate.shape
    tf = min(FF_TILE, d_ff, d)
    n_j = d_ff // tf
    n_n = d // tf
    tm = EXP_TILE
    rows_g = EXP_GROUP * tm
    grid_spec = pltpu.PrefetchScalarGridSpec(
        num_scalar_prefetch=6,
        grid=(n_groups_max * (n_j + n_n),),
        in_specs=[pl.BlockSpec(memory_space=pl.ANY),
                  pl.BlockSpec((1, d, tf), lambda s, e, j, n, *_: (e[s], 0, j[s])),
                  pl.BlockSpec((1, d, tf), lambda s, e, j, n, *_: (e[s], 0, j[s])),
                  pl.BlockSpec((1, d_ff, tf), lambda s, e, j, n, *_: (e[s], 0, n[s])),
                  pl.BlockSpec((1, 1, tf), lambda s, e, j, n, *_: (e[s], 0, j[s])),
                  pl.BlockSpec((1, 1, tf), lambda s, e, j, n, *_: (e[s], 0, j[s])),
                  pl.BlockSpec((1, 1, tf), lambda s, e, j, n, *_: (e[s], 0, n[s]))],
        out_specs=pl.BlockSpec(memory_space=pl.ANY),
        scratch_shapes=[pltpu.VMEM((rows_g, d), BF16),
                        pltpu.VMEM((n_j, rows_g, tf), BF16),
                        pltpu.VMEM((d, tf), BF16), pltpu.VMEM((d, tf), BF16), pltpu.VMEM((d_ff, tf), BF16),
                        pltpu.VMEM((tm, d), F32), pltpu.VMEM((2, tm, tf), F32),
                        pltpu.SemaphoreType.DMA, pltpu.SemaphoreType.DMA((2,))],
    )
    return pl.pallas_call(
        functools.partial(_expert_kernel, n_j=n_j, n_n=n_n, tm=tm, tf=tf),
        out_shape=jax.ShapeDtypeStruct((r_max, d), F32),
        grid_spec=grid_spec,
        compiler_params=_cparams(("arbitrary",)),
        name="expert_ffn",
    )(st_e, st_j, st_n, g_t0, g_gs, g_valid, xs, w_gate, w_up, w_down,
      b_gate.reshape(n_exp, 1, d_ff), b_up.reshape(n_exp, 1, d_ff), b_down.reshape(n_exp, 1, d))


def _combine_kernel(cnt_ref, base_ref, ys_hbm, x2_ref, pk_ref, gt_ref, gate_p_ref, gate_s_ref, g_ref,
                    yp_ref, ysm_ref, rows_ref, sem, *, n_experts, tile, n_prompt_tiles):
    i = pl.program_id(0)

    @pl.when(i == 0)
    def _():
        rows_ref[...] = jnp.zeros(rows_ref.shape, F32)

    def copy(dst, src, sz):
        return pltpu.make_async_copy(ys_hbm.at[pl.ds(src, sz)], rows_ref.at[pl.ds(dst, sz)], sem)

    _for_each_chunk_piece(cnt_ref, base_ref, i, n_experts, tile, lambda s, d, sz: copy(s, d, sz).start())
    _for_each_chunk_piece(cnt_ref, base_ref, i, n_experts, tile, lambda s, d, sz: copy(s, d, sz).wait())
    weights = _one_hot_rows(pk_ref[...], rows_ref.shape[0], gt_ref[...])
    y = jnp.dot(weights, rows_ref[...].astype(BF16), preferred_element_type=F32)
    is_prompt = i < n_prompt_tiles
    gate = jnp.where(is_prompt, gate_p_ref[0], gate_s_ref[...])
    x3 = x2_ref[...] + gate * y
    ms = jnp.mean(x3 * x3, axis=-1, keepdims=True)
    out = x3 * lax.rsqrt(ms + NORM_EPS) * g_ref[...]

    @pl.when(is_prompt)
    def _():
        yp_ref[...] = out

    @pl.when(jnp.logical_not(is_prompt))
    def _():
        ysm_ref[...] = out


def _combine(ys, cnt_flat, base_flat, x2, pk4, gt, gate_p, gate_s, final_g, *, n_prompt, tiles_per_seq, n_experts):
    n, d = x2.shape
    tile = min(TOK_TILE, n_prompt)
    n_s = n - n_prompt
    assert n_s == tile and n_prompt % tile == 0
    npt = n_prompt // tile
    grid_spec = pltpu.PrefetchScalarGridSpec(
        num_scalar_prefetch=2,
        grid=(n // tile,),
        in_specs=[pl.BlockSpec(memory_space=pl.ANY),
                  pl.BlockSpec((tile, d), lambda i, *_: (i, 0)),
                  pl.BlockSpec((tile, LANES), lambda i, *_: (i, 0)),
                  pl.BlockSpec((tile, LANES), lambda i, *_: (i, 0)),
                  pl.BlockSpec((1, 1, d), lambda i, *_: (jnp.minimum(i, npt - 1) // tiles_per_seq, 0, 0)),
                  pl.BlockSpec((tile, d), lambda i, *_: (0, 0)),
                  pl.BlockSpec((1, d), lambda i, *_: (0, 0))],
        out_specs=(pl.BlockSpec((tile, d), lambda i, *_: (jnp.minimum(i, npt - 1), 0)),
                   pl.BlockSpec((tile, d), lambda i, *_: (0, 0))),
        scratch_shapes=[pltpu.VMEM((_sorted_rows(tile, n_experts), d), F32), pltpu.SemaphoreType.DMA],
    )
    return pl.pallas_call(
        functools.partial(_combine_kernel, n_experts=n_experts, tile=tile, n_prompt_tiles=npt),
        out_shape=(jax.ShapeDtypeStruct((n_prompt, d), F32), jax.ShapeDtypeStruct((n_s, d), F32)),
        grid_spec=grid_spec,
        compiler_params=_cparams(("arbitrary",)),
        name="combine_norm",
    )(cnt_flat, base_flat, ys, x2, pk4, gt, gate_p, gate_s, final_g)


def _expert_tables(tile_cnt, *, tile, n_j, n_n):
    n_tiles, n_experts = tile_cnt.shape
    tm, grp = EXP_TILE, EXP_GROUP
    cnt8 = (tile_cnt + (SUBLANES - 1)) // SUBLANES * SUBLANES
    rows_e = jnp.sum(cnt8, axis=0)
    max_rows = n_tiles * tile * TOP_K + n_experts * n_tiles * (SUBLANES - 1)
    t_max = -(-max_rows // tm) + n_experts
    ng_max = n_experts + t_max // grp
    ntile = (rows_e + tm - 1) // tm
    tile_start = jnp.cumsum(ntile) - ntile
    pstart = tile_start * tm
    base = pstart[None, :] + jnp.cumsum(cnt8, axis=0) - cnt8
    ng = (ntile + grp - 1) // grp
    cg = jnp.cumsum(ng)
    n_groups = cg[-1]
    q = jnp.arange(ng_max, dtype=I32)
    eq = jnp.minimum(jnp.searchsorted(cg, q, side="right").astype(I32), n_experts - 1)
    lg = q - (cg[eq] - ng[eq])
    t0 = tile_start[eq] + lg * grp
    gs = jnp.clip(ntile[eq] - lg * grp, 0, grp)
    valid = q < n_groups
    last = jnp.maximum(n_groups - 1, 0)
    eq = jnp.where(valid, eq, eq[last])
    t0 = jnp.where(valid, t0, t0[last])
    gs = jnp.where(valid, gs, 0)
    steps = n_j + n_n
    s = jnp.arange(ng_max * steps, dtype=I32)
    qs = s // steps
    ph = s - qs * steps
    vs = valid[qs]
    st_e = eq[qs]
    st_j = jnp.where(vs, jnp.minimum(ph, n_j - 1), n_j - 1)
    st_n = jnp.where(vs, jnp.maximum(ph - n_j, 0), n_n - 1)
    tail = jnp.maximum(pstart + (ntile - 1) * tm, 0)
    has = (ntile > 0).astype(I32)
    tabs = tuple(a.astype(I32) for a in (st_e, st_j, st_n, t0, gs, valid))
    return (tabs, tile_cnt.reshape(-1).astype(I32), base.reshape(-1).astype(I32), tail.astype(I32), has,
            t_max * tm, ng_max)


def _rope_tables(pos):
    inv = 1.0 / (ROPE_THETA ** (jnp.arange(0, HEAD_DIM, 2, dtype=F32) / HEAD_DIM))
    ang = pos.astype(F32)[:, None] * inv[None, :]
    reps = LANES // (HEAD_DIM // 2)
    return jnp.tile(jnp.cos(ang), (1, reps)), jnp.tile(jnp.sin(ang), (1, reps))


def kernel(x_prompt, x_sample, cache_k, cache_v, state_pool, page_table, c_prompt, c_sample, w_ada, b_ada, norm1_g, norm2_g, w_in, lam_q1, lam_k1, lam_q2, lam_k2, subln_g, w_pool, pool_scale, w_out, w_router, b_router, w_gate, b_gate, w_up, b_up, w_down, b_down, final_g):
    B, S, D = x_prompt.shape
    DB, T, _ = x_sample.shape
    depth = w_ada.shape[0]
    page = cache_k.shape[2]
    past = page_table.shape[1] * page
    d_att = D // 2
    n_heads = d_att // HEAD_W
    d_pool = w_in.shape[2] - 3 * d_att
    n_experts = w_router.shape[2]
    state_len = state_pool.shape[2]
    n_p, n_s = B * S, DB * T
    n_all = n_p + n_s
    tm = min(TOK_TILE, n_p)
    tiles_per_seq = S // tm
    assert depth == 1, "single-layer step"
    assert n_s == tm and S % tm == 0 and T % 8 == 0 and state_len < POOL_HALO <= tm

    cos_p, sin_p = _rope_tables(jnp.arange(S))
    cos_s, sin_s = _rope_tables(jnp.tile(past + jnp.arange(T), DB))

    l = 0
    lam_init = 0.8 - 0.6 * math.exp(-0.3 * l)
    lamp = jnp.zeros((8, LANES), F32)
    for r, vec in enumerate((lam_q1[l], lam_k1[l], lam_q2[l], lam_k2[l])):
        lamp = lamp.at[r, :HEAD_DIM].set(vec.astype(F32))
    subg = subln_g[l].reshape(1, HEAD_W)

    rows_c = -(-(B + DB) // 8) * 8
    c_all = jnp.zeros((rows_c, D), F32).at[:B].set(c_prompt).at[B:B + DB].set(c_sample)
    m_all = _adaln(c_all, w_ada[l], b_ada[l])
    mods_p = [m_all[:B, k * D:(k + 1) * D].reshape(B, 1, D) for k in range(N_ADA)]
    mods_s = [jnp.repeat(m_all[B:B + DB, k * D:(k + 1) * D], T, axis=0).reshape(1, n_s, D) for k in range(N_ADA)]

    w_in_b = _cast_bf16(w_in[l], 256)
    w_out_b = _cast_bf16(w_out[l], 256)
    ng, gw = w_pool.shape[1], w_pool.shape[2]
    w_pool_b = _cast_bf16(w_pool[l].reshape(ng * gw, gw), ng * gw).reshape(ng, gw, gw)
    g1 = norm1_g[l].reshape(1, D)
    g2 = norm2_g[l].reshape(1, D)
    pscale = pool_scale[l].reshape(1, d_pool)
    w_router_p = jnp.zeros((D, LANES), F32).at[:, :n_experts].set(w_router[l])
    b_router_p = jnp.zeros((1, LANES), F32).at[0, :n_experts].set(b_router[l].astype(F32))

    xp = x_prompt.reshape(n_p, D)
    q_p, k_p, v_p, u_p, kb_p, vb_p = _project(xp, mods_p[0], mods_p[1], g1, w_in_b, cos_p, sin_p,
                                              tiles_per_seq=tiles_per_seq, pos_tiles=tiles_per_seq)
    a_p = _attn_prompt(q_p, kb_p, vb_p, lamp, subg, batch=B, seq=S, lam_init=lam_init)
    pm_p = _pool_mix(u_p.reshape(B, S, d_pool), w_pool_b, pscale, tile=tm, cur_block0=0, pos0=0,
                     zero_first_halo=True).reshape(n_p, d_pool)

    xs_tok = x_sample.reshape(n_s, D)
    q_s, k_s, v_s, u_s, _, _ = _project(xs_tok, mods_s[0], mods_s[1], g1, w_in_b, cos_s, sin_s,
                                        tiles_per_seq=1, pos_tiles=1)
    a_s = _attn_sample(page_table, q_s, k_s, v_s, cache_k[l], cache_v[l], lamp, subg, t_new=T, lam_init=lam_init)
    u_ext = jnp.concatenate([jnp.zeros((DB, POOL_HALO - state_len, d_pool), F32),
                             state_pool[l].astype(F32), u_s.reshape(DB, T, d_pool)], axis=1)
    pm_s = _pool_mix(u_ext, w_pool_b, pscale, tile=T, cur_block0=POOL_HALO // T, pos0=past,
                     zero_first_halo=False).reshape(n_s, d_pool)

    outs = _mix_out(xp, a_p, pm_p, mods_p[2], mods_p[3], mods_p[4], g2, w_out_b, w_router_p, b_router_p,
                    n_total=n_all, tile0=0, tiles_per_seq=tiles_per_seq, n_experts=n_experts)
    x2, h2, pk4, gt4, tcnt = _mix_out(xs_tok, a_s, pm_s, mods_s[2], mods_s[3], mods_s[4], g2, w_out_b, w_router_p,
                                      b_router_p, n_total=n_all, tile0=n_p // tm, tiles_per_seq=1,
                                      n_experts=n_experts, prev=outs)

    d_ff = w_gate.shape[3]
    tf = min(FF_TILE, d_ff, D)
    tile_cnt = tcnt[:, 0, :n_experts].astype(I32)
    tabs, cnt_flat, base_flat, tail, has, r_max, ng_max = _expert_tables(
        tile_cnt, tile=tm, n_j=d_ff // tf, n_n=D // tf)
    xs_rows = _dispatch(h2, pk4, cnt_flat, base_flat, tail, has, r_max=r_max, n_experts=n_experts)
    ys_rows = _expert_ffn(tabs, xs_rows, w_gate[l], b_gate[l], w_up[l], b_up[l], w_down[l], b_down[l],
                          n_groups_max=ng_max)
    y_p, y_s = _combine(ys_rows, cnt_flat, base_flat, x2, pk4, gt4, mods_p[5], mods_s[5].reshape(n_s, D),
                        final_g.reshape(1, D), n_prompt=n_p, tiles_per_seq=tiles_per_seq, n_experts=n_experts)

    n_pages_p = S // page
    k_prompt = k_p.reshape(1, B, n_pages_p, page, n_heads, HEAD_W)
    v_prompt = v_p.reshape(1, B, n_pages_p, page, n_heads, HEAD_W)
    pool_prompt = u_p.reshape(B, S, d_pool)[:, S - state_len:][None]
    k_sample = k_s.reshape(1, DB, T, n_heads, HEAD_W)
    v_sample = v_s.reshape(1, DB, T, n_heads, HEAD_W)
    pool_sample = u_ext[:, -state_len:][None]
    return (y_p.reshape(B, S, D), y_s.reshape(DB, T, D), k_prompt, v_prompt, pool_prompt,
            k_sample, v_sample, pool_sample)
```

```python
import functools
import math

import jax
import jax.numpy as jnp
from jax import lax
from jax.experimental import pallas as pl
from jax.experimental.pallas import tpu as pltpu

F32 = jnp.float32
BF16 = jnp.bfloat16
I32 = jnp.int32

HEAD_DIM = 64
HEAD_W = 2 * HEAD_DIM
POOL_WINDOWS = (2, 4, 8, 16)
POOL_HALO = 16
TOP_K = 4
SWIGLU_LIMIT = 7.0
SWIGLU_ALPHA = 1.702
ROPE_THETA = 10000.0
NORM_EPS = 1e-5
N_ADA = 6
LANES = 128
SUBLANES = 8
V7X_VMEM_LIMIT = 58 * 1024 * 1024

TOK_TILE = 256
EXP_TILE = 256
EXP_GROUP = 6
FF_TILE = 512
Q_SCALE = (HEAD_DIM ** -0.5) * math.log2(math.e)


def _cparams(sem, vmem=V7X_VMEM_LIMIT):
    return pltpu.CompilerParams(dimension_semantics=sem, vmem_limit_bytes=vmem)


def _cast_kernel(x_ref, o_ref):
    o_ref[...] = x_ref[...].astype(o_ref.dtype)


def _cast_bf16(w, rows):
    r, c = w.shape
    return pl.pallas_call(
        _cast_kernel,
        out_shape=jax.ShapeDtypeStruct((r, c), BF16),
        grid=(r // rows,),
        in_specs=[pl.BlockSpec((rows, c), lambda i: (i, 0))],
        out_specs=pl.BlockSpec((rows, c), lambda i: (i, 0)),
        compiler_params=_cparams(("arbitrary",)),
        name="cast_bf16",
    )(w)


def _ada_kernel(c_ref, w_ref, b_ref, o_ref):
    c = c_ref[...]
    s = (c * jax.nn.sigmoid(c)).astype(BF16)
    o_ref[...] = jnp.dot(s, w_ref[...].astype(BF16), preferred_element_type=F32) + b_ref[...]


def _adaln(c_all, w_ada, b_ada):
    rows, d = c_all.shape
    n = w_ada.shape[1]
    tn = min(1024, n)
    return pl.pallas_call(
        _ada_kernel,
        out_shape=jax.ShapeDtypeStruct((rows, n), F32),
        grid=(n // tn,),
        in_specs=[pl.BlockSpec((rows, d), lambda j: (0, 0)),
                  pl.BlockSpec((d, tn), lambda j: (0, j)),
                  pl.BlockSpec((1, tn), lambda j: (0, j))],
        out_specs=pl.BlockSpec((rows, tn), lambda j: (0, j)),
        compiler_params=_cparams(("arbitrary",)),
        name="adaln",
    )(c_all, w_ada, b_ada.reshape(1, n))


def _modulated_norm(x, g, shift, scale):
    ms = jnp.mean(x * x, axis=-1, keepdims=True)
    return (x * lax.rsqrt(ms + NORM_EPS) * g) * (1.0 + scale) + shift


def _proj_kernel(x_ref, shift_ref, scale_ref, g_ref, w_ref, cos_ref, sin_ref,
                 q_ref, k_ref, v_ref, u_ref, kb_ref, vb_ref, *, d_att):
    h = _modulated_norm(x_ref[...], g_ref[...], shift_ref[0], scale_ref[0]).astype(BF16)
    cos = cos_ref[...]
    sin = sin_ref[...]
    lane = lax.broadcasted_iota(I32, cos.shape, 1)
    first_half = (lane & (HEAD_DIM - 1)) < (HEAD_DIM // 2)

    def rope(z):
        rot = jnp.where(first_half, -pltpu.roll(z, LANES - HEAD_DIM // 2, 1), pltpu.roll(z, HEAD_DIM // 2, 1))
        return z * cos + rot * sin

    zq = jnp.dot(h, w_ref[:, 0:d_att], preferred_element_type=F32)
    zk = jnp.dot(h, w_ref[:, d_att:2 * d_att], preferred_element_type=F32)
    for hh in range(d_att // HEAD_W):
        sl = slice(hh * HEAD_W, (hh + 1) * HEAD_W)
        q_ref[:, sl] = rope(zq[:, sl]) * Q_SCALE
        kr = rope(zk[:, sl])
        k_ref[:, sl] = kr
        kb_ref[:, sl] = kr.astype(BF16)
    zv = jnp.dot(h, w_ref[:, 2 * d_att:3 * d_att], preferred_element_type=F32)
    v_ref[...] = zv
    vb_ref[...] = zv.astype(BF16)
    u_ref[...] = jnp.dot(h, w_ref[:, 3 * d_att:], preferred_element_type=F32)


def _project(x, shift, scale, g, w_in_b, cos, sin, *, tiles_per_seq, pos_tiles):
    n, d = x.shape
    tm = min(TOK_TILE, n)
    d_in = w_in_b.shape[1]
    d_att = (d // 2)
    d_pool = d_in - 3 * d_att
    mod_rows = shift.shape[1]
    mod_spec = pl.BlockSpec((1, mod_rows, d), lambda i: (i // tiles_per_seq, 0, 0))
    tok = lambda c: pl.BlockSpec((tm, c), lambda i: (i, 0))
    return pl.pallas_call(
        functools.partial(_proj_kernel, d_att=d_att),
        out_shape=(jax.ShapeDtypeStruct((n, d_att), F32), jax.ShapeDtypeStruct((n, d_att), F32),
                   jax.ShapeDtypeStruct((n, d_att), F32), jax.ShapeDtypeStruct((n, d_pool), F32),
                   jax.ShapeDtypeStruct((n, d_att), BF16), jax.ShapeDtypeStruct((n, d_att), BF16)),
        grid=(n // tm,),
        in_specs=[tok(d), mod_spec, mod_spec,
                  pl.BlockSpec((1, d), lambda i: (0, 0)),
                  pl.BlockSpec((d, d_in), lambda i: (0, 0)),
                  pl.BlockSpec((tm, LANES), lambda i: (i % pos_tiles, 0)),
                  pl.BlockSpec((tm, LANES), lambda i: (i % pos_tiles, 0))],
        out_specs=(tok(d_att), tok(d_att), tok(d_att), tok(d_pool), tok(d_att), tok(d_att)),
        compiler_params=_cparams(("arbitrary",)),
        name="in_proj",
    )(x, shift, scale, g, w_in_b, cos, sin)


def _lambda_value(lam_ref, lam_init):
    lp = lam_ref[...]
    a = jnp.sum(lp[0:1] * lp[1:2], axis=1, keepdims=True)
    b = jnp.sum(lp[2:3] * lp[3:4], axis=1, keepdims=True)
    return jnp.exp(a) - jnp.exp(b) + lam_init


def _stack_maps(q):
    lane = lax.broadcasted_iota(I32, q.shape, 1)
    q1 = jnp.where(lane < HEAD_DIM, q, 0.0)
    q2 = jnp.where(lane >= HEAD_DIM, q, 0.0)
    return jnp.concatenate([q1, q2], axis=0).astype(BF16)


def _softmax_update(s, vb, m_prev, l_prev, acc_prev):
    reps = s.shape[1] // LANES
    m_cur = jnp.max(s, axis=1, keepdims=True)
    m_new = jnp.maximum(m_prev, m_cur)
    m_wide = m_new if reps == 1 else jnp.concatenate([m_new] * reps, axis=1)
    p = jnp.exp2(s - m_wide)
    alpha = jnp.exp2(m_prev - m_new)
    l_new = alpha * l_prev + jnp.sum(p, axis=1, keepdims=True)
    acc_new = acc_prev * alpha + jnp.dot(p.astype(BF16), vb, preferred_element_type=F32)
    return m_new, l_new, acc_new


def _diff_finish(l, acc, lam, g, t, lam_init):
    o = acc[:t] / l[:t] - lam * (acc[t:] / l[t:])
    ms = jnp.mean(o * o, axis=1, keepdims=True)
    return o * lax.rsqrt(ms + NORM_EPS) * g * (1.0 - lam_init)


_NT = (((1,), (1,)), ((), ()))


_TN = (((0,), (0,)), ((), ()))


def _attn_prompt_kernel(lam_ref, g_ref, q_ref, k_ref, v_ref, o_ref, m_ref, l_ref, acc_ref, *, tq, tk, lam_init):
    qi = pl.program_id(2)
    heads = q_ref.shape[1] // HEAD_W
    qqs = [_stack_maps(q_ref[:, hh * HEAD_W:(hh + 1) * HEAD_W]) for hh in range(heads)]
    m_ref[...] = jnp.full(m_ref.shape, -jnp.inf, F32)
    l_ref[...] = jnp.zeros(l_ref.shape, F32)
    acc_ref[...] = jnp.zeros(acc_ref.shape, F32)

    def block(start, size, masked):
        rows = pl.ds(start, size)
        for hh in range(heads):
            cols = slice(hh * HEAD_W, (hh + 1) * HEAD_W)
            st = lax.dot_general(k_ref[rows, cols], qqs[hh], _NT, preferred_element_type=F32)
            if masked:
                kpos = start + lax.broadcasted_iota(I32, st.shape, 0)
                c = lax.broadcasted_iota(I32, st.shape, 1)
                qpos = qi * tq + jnp.where(c >= tq, c - tq, c)
                st = jnp.where(kpos <= qpos, st, -jnp.inf)
            m_prev = m_ref[hh]
            m_new = jnp.maximum(m_prev, jnp.max(st, axis=0, keepdims=True))
            p = jnp.exp2(st - m_new)
            alpha = jnp.exp2(m_prev - m_new)
            l_ref[hh] = alpha * l_ref[hh] + jnp.sum(p, axis=0, keepdims=True)
            pv = lax.dot_general(v_ref[rows, cols], p.astype(BF16), _TN, preferred_element_type=F32)
            acc_ref[hh] = acc_ref[hh] * alpha + pv
            m_ref[hh] = m_new

    visible = qi * tq
    n_big = visible // tk

    def big(j, carry):
        block(pl.multiple_of(j * tk, tk), tk, False)
        return carry

    lax.fori_loop(0, n_big, big, 0)

    def small(j, carry):
        block(pl.multiple_of(n_big * tk + j * tq, tq), tq, False)
        return carry

    lax.fori_loop(0, (visible - n_big * tk) // tq, small, 0)
    block(pl.multiple_of(visible, tq), tq, True)

    lam = _lambda_value(lam_ref, lam_init)
    for hh in range(heads):
        l = l_ref[hh]
        acc = acc_ref[hh]
        o = acc[:, :tq] / l[:, :tq] - lam * (acc[:, tq:] / l[:, tq:])
        ms = jnp.mean(o * o, axis=0, keepdims=True)
        a = o * lax.rsqrt(ms + NORM_EPS) * g_ref[...] * (1.0 - lam_init)
        o_ref[:, hh * HEAD_W:(hh + 1) * HEAD_W] = a.T.astype(o_ref.dtype)


def _attn_prompt(q, kb, vb, lamp, g_col, *, batch, seq, lam_init):
    n, d_att = q.shape
    nh = d_att // HEAD_W
    hp = 2 if nh % 2 == 0 else 1
    tq = min(256, seq)
    tk = min(512, seq)
    nq = seq // tq
    return pl.pallas_call(
        functools.partial(_attn_prompt_kernel, tq=tq, tk=tk, lam_init=lam_init),
        out_shape=jax.ShapeDtypeStruct((n, d_att), BF16),
        grid=(batch, nh // hp, nq),
        in_specs=[pl.BlockSpec((8, LANES), lambda b, h, i: (0, 0)),
                  pl.BlockSpec((HEAD_W, 1), lambda b, h, i: (0, 0)),
                  pl.BlockSpec((tq, hp * HEAD_W), lambda b, h, i: (b * nq + i, h)),
                  pl.BlockSpec((seq, hp * HEAD_W), lambda b, h, i: (b, h)),
                  pl.BlockSpec((seq, hp * HEAD_W), lambda b, h, i: (b, h))],
        out_specs=pl.BlockSpec((tq, hp * HEAD_W), lambda b, h, i: (b * nq + i, h)),
        scratch_shapes=[pltpu.VMEM((hp, 1, 2 * tq), F32), pltpu.VMEM((hp, 1, 2 * tq), F32),
                        pltpu.VMEM((hp, HEAD_W, 2 * tq), F32)],
        compiler_params=_cparams(("arbitrary", "arbitrary", "arbitrary")),
        name="attn_prompt",
    )(lamp, g_col, q, kb, vb)


def _attn_sample_kernel(pt_ref, lam_ref, g_ref, q_ref, kn_ref, vn_ref, *rest, n_heads, pages, t_new, lam_init):
    k_pages = rest[:pages]
    v_pages = rest[pages:2 * pages]
    o_ref = rest[2 * pages]
    m_ref, l_ref, acc_ref = rest[2 * pages + 1:]
    c = pl.program_id(1)
    page = k_pages[0].shape[1] // n_heads

    @pl.when(c == 0)
    def _():
        m_ref[...] = jnp.full(m_ref.shape, -jnp.inf, F32)
        l_ref[...] = jnp.zeros(l_ref.shape, F32)
        acc_ref[...] = jnp.zeros(acc_ref.shape, F32)

    q = q_ref[...]
    qqs = [_stack_maps(q[:, hh * HEAD_W:(hh + 1) * HEAD_W]) for hh in range(n_heads)]

    def update(hh, kh, vh, mask):
        s = lax.dot_general(qqs[hh], kh, _NT, preferred_element_type=F32)
        if mask is not None:
            s = jnp.where(mask, s, -jnp.inf)
        m, l, acc = _softmax_update(s, vh, m_ref[hh], l_ref[hh], acc_ref[hh])
        m_ref[hh] = m
        l_ref[hh] = l
        acc_ref[hh] = acc

    def head_rows(refs, hh):
        parts = [r[0, pl.ds(hh, page, stride=n_heads), :] for r in refs]
        return jnp.concatenate(parts, axis=0).astype(BF16)

    for hh in range(n_heads):
        update(hh, head_rows(k_pages, hh), head_rows(v_pages, hh), None)

    @pl.when(c == pl.num_programs(1) - 1)
    def _():
        r = lax.broadcasted_iota(I32, (2 * t_new, LANES), 0)
        col = lax.broadcasted_iota(I32, (2 * t_new, LANES), 1)
        r = jnp.where(r >= t_new, r - t_new, r)
        mask = col <= r
        lam = _lambda_value(lam_ref, lam_init)
        pad = jnp.zeros((LANES - t_new, HEAD_W), F32)
        for hh in range(n_heads):
            sl = slice(hh * HEAD_W, (hh + 1) * HEAD_W)
            kn = jnp.concatenate([kn_ref[:, sl], pad], axis=0).astype(BF16)
            vn = jnp.concatenate([vn_ref[:, sl], pad], axis=0).astype(BF16)
            update(hh, kn, vn, mask)
            o_ref[:, sl] = _diff_finish(l_ref[hh], acc_ref[hh], lam, g_ref[...], t_new, lam_init)


def _attn_sample(page_table, q, k_new, v_new, cache_k, cache_v, lamp, g, *, t_new, lam_init):
    n, d_att = q.shape
    nh = d_att // HEAD_W
    db, n_pages = page_table.shape
    n_pool, page = cache_k.shape[0], cache_k.shape[1]
    pages = min(8, n_pages)
    ck = cache_k.reshape(n_pool, page * nh, HEAD_W)
    cv = cache_v.reshape(n_pool, page * nh, HEAD_W)

    def page_spec(i):
        return pl.BlockSpec((1, page * nh, HEAD_W), lambda b, c, pt: (pt[b, c * pages + i], 0, 0))

    tok = pl.BlockSpec((t_new, d_att), lambda b, c, pt: (b, 0))
    grid_spec = pltpu.PrefetchScalarGridSpec(
        num_scalar_prefetch=1,
        grid=(db, n_pages // pages),
        in_specs=[pl.BlockSpec((8, LANES), lambda b, c, pt: (0, 0)),
                  pl.BlockSpec((1, HEAD_W), lambda b, c, pt: (0, 0)),
                  tok, tok, tok]
                 + [page_spec(i) for i in range(pages)] + [page_spec(i) for i in range(pages)],
        out_specs=tok,
        scratch_shapes=[pltpu.VMEM((nh, 2 * t_new, HEAD_W), F32)] * 3,
    )
    return pl.pallas_call(
        functools.partial(_attn_sample_kernel, n_heads=nh, pages=pages, t_new=t_new, lam_init=lam_init),
        out_shape=jax.ShapeDtypeStruct((n, d_att), F32),
        grid_spec=grid_spec,
        compiler_params=_cparams(("arbitrary", "arbitrary")),
        name="attn_sample",
    )(page_table, lamp, g, q, k_new, v_new, *([ck] * pages), *([cv] * pages))


def _pool_kernel(halo_ref, cur_ref, w_ref, scale_ref, o_ref, ext_ref, *, pos0, tile_pos, zero_first_halo):
    i = pl.program_id(1)
    t = cur_ref.shape[1]
    halo = halo_ref[0]
    if zero_first_halo:
        halo = jnp.where(i == 0, 0.0, halo)
    cur = cur_ref[0]
    ext_ref[0:POOL_HALO, :] = halo
    ext_ref[POOL_HALO:POOL_HALO + t, :] = cur
    pos = pos0 + i * tile_pos + lax.broadcasted_iota(I32, (t, 1), 0)
    gw = cur.shape[1] // len(POOL_WINDOWS)
    for gi, w in enumerate(POOL_WINDOWS):
        cols = slice(gi * gw, (gi + 1) * gw)
        total = cur[:, cols]
        for j in range(1, w):
            total = total + ext_ref[POOL_HALO - j:POOL_HALO - j + t, cols]
        cnt = jnp.minimum(pos + 1, w).astype(F32)
        dlt = total / cnt - cur[:, cols]
        y = jnp.dot(dlt.astype(BF16), w_ref[gi], preferred_element_type=F32)
        o_ref[0, :, cols] = (y * scale_ref[:, cols]).astype(o_ref.dtype)


def _pool_mix(u3, w_pool_b, scale, *, tile, cur_block0, pos0, zero_first_halo):
    b, rows, c = u3.shape
    n_tiles = (rows - cur_block0 * tile) // tile
    ng, gw = w_pool_b.shape[0], w_pool_b.shape[1]
    assert (cur_block0 * tile) % POOL_HALO == 0 and (tile % POOL_HALO == 0 or n_tiles == 1)

    def halo_map(bi, i):
        return (bi, jnp.maximum(((cur_block0 + i) * tile) // POOL_HALO - 1, 0), 0)

    return pl.pallas_call(
        functools.partial(_pool_kernel, pos0=pos0, tile_pos=tile, zero_first_halo=zero_first_halo),
        out_shape=jax.ShapeDtypeStruct((b, n_tiles * tile, c), BF16),
        grid=(b, n_tiles),
        in_specs=[pl.BlockSpec((1, POOL_HALO, c), halo_map),
                  pl.BlockSpec((1, tile, c), lambda bi, i: (bi, cur_block0 + i, 0)),
                  pl.BlockSpec((ng, gw, gw), lambda bi, i: (0, 0, 0)),
                  pl.BlockSpec((1, c), lambda bi, i: (0, 0))],
        out_specs=pl.BlockSpec((1, tile, c), lambda bi, i: (bi, i, 0)),
        scratch_shapes=[pltpu.VMEM((POOL_HALO + tile, c), F32)],
        compiler_params=_cparams(("arbitrary", "arbitrary")),
        name="pool_mix",
    )(u3, u3, w_pool_b, scale)


def _split_bf16(x):
    hi = x.astype(BF16)
    lo = (x - hi.astype(F32)).astype(BF16)
    return hi, lo


def _round_up_f32(x, m):
    return jnp.floor((x + (m - 1.0)) * (1.0 / m)) * m


def _mix_out_kernel(*refs, n_experts, aliased):
    (x_ref, a_ref, pm_ref, gate_ref, shift_ref, scale_ref, g_ref, wo_ref, wr_ref, br_ref) = refs[:10]
    x2_ref, h2_ref, pk_ref, gt_ref, tc_ref = refs[10 + aliased:]
    d_att = a_ref.shape[1]
    mix = (jnp.dot(a_ref[...].astype(BF16), wo_ref[0:d_att, :], preferred_element_type=F32)
           + jnp.dot(pm_ref[...], wo_ref[d_att:, :], preferred_element_type=F32))
    x2 = x_ref[...] + gate_ref[0] * mix
    x2_ref[...] = x2
    h2 = _modulated_norm(x2, g_ref[...], shift_ref[0], scale_ref[0])
    h2_ref[...] = h2.astype(BF16)
    hh, hl = _split_bf16(h2)
    wh, wl = _split_bf16(wr_ref[...])
    logits = (jnp.dot(hh, wh, preferred_element_type=F32) + jnp.dot(hl, wh, preferred_element_type=F32)
              + jnp.dot(hh, wl, preferred_element_type=F32)) + br_ref[...]
    t = logits.shape[0]
    lane = lax.broadcasted_iota(I32, logits.shape, 1)
    lanef = lane.astype(F32)
    work = jnp.where(lane < n_experts, logits, -jnp.inf)
    vals, ids = [], []
    for _ in range(TOP_K):
        mx = jnp.max(work, axis=1, keepdims=True)
        ix = jnp.min(jnp.where(work == mx, lanef, float(LANES)), axis=1, keepdims=True)
        vals.append(mx)
        ids.append(ix)
        work = jnp.where(lanef == ix, -jnp.inf, work)
    es = [jnp.exp(v - vals[0]) for v in vals]
    den = es[0]
    for e in es[1:]:
        den = den + e
    sel = jnp.zeros(logits.shape, F32)
    for k in range(TOP_K):
        sel = jnp.where(lanef == ids[k], 1.0, sel)
    r = lax.broadcasted_iota(I32, (t, t), 0)
    c = lax.broadcasted_iota(I32, (t, t), 1)
    earlier = jnp.where(c < r, 1.0, 0.0).astype(BF16)
    local_rank = jnp.dot(earlier, sel.astype(BF16), preferred_element_type=F32)
    cnt = jnp.sum(sel, axis=0, keepdims=True)
    cnt8 = jnp.broadcast_to(_round_up_f32(cnt, float(SUBLANES)), (SUBLANES, LANES))
    er = lax.broadcasted_iota(I32, (LANES, LANES), 0)
    ec = lax.broadcasted_iota(I32, (LANES, LANES), 1)
    before = jnp.where(er < ec, 1.0, 0.0).astype(BF16)
    chunk_off = jnp.dot(cnt8.astype(BF16), before, preferred_element_type=F32)[0:1, :]
    pos = local_rank + chunk_off
    pk_out = jnp.zeros(logits.shape, F32)
    gt_out = jnp.zeros(logits.shape, F32)
    for k in range(TOP_K):
        pk = jnp.sum(jnp.where(lanef == ids[k], pos, 0.0), axis=1, keepdims=True)
        pk_out = jnp.where(lane == k, pk, pk_out)
        gt_out = jnp.where(lane == k, es[k] / den, gt_out)
    pk_ref[...] = pk_out
    gt_ref[...] = gt_out
    tc_ref[0] = jnp.broadcast_to(cnt, (SUBLANES, LANES))


def _mix_out(x, a, pm, gate, shift, scale, g, w_out_b, w_router_p, b_router_p, *, n_total, tile0,
             tiles_per_seq, n_experts, prev=None):
    n, d = x.shape
    tm = min(TOK_TILE, n)
    d_att = a.shape[1]
    mod_rows = gate.shape[1]
    mod_spec = pl.BlockSpec((1, mod_rows, d), lambda i: (i // tiles_per_seq, 0, 0))
    tok = lambda c: pl.BlockSpec((tm, c), lambda i: (i, 0))
    out_tok = lambda c: pl.BlockSpec((tm, c), lambda i: (tile0 + i, 0))
    out_shape = (jax.ShapeDtypeStruct((n_total, d), F32), jax.ShapeDtypeStruct((n_total, d), BF16),
                 jax.ShapeDtypeStruct((n_total, LANES), F32), jax.ShapeDtypeStruct((n_total, LANES), F32),
                 jax.ShapeDtypeStruct((n_total // tm, SUBLANES, LANES), F32))
    in_specs = [tok(d), tok(d_att), tok(pm.shape[1]), mod_spec, mod_spec, mod_spec,
                pl.BlockSpec((1, d), lambda i: (0, 0)),
                pl.BlockSpec(w_out_b.shape, lambda i: (0, 0)),
                pl.BlockSpec(w_router_p.shape, lambda i: (0, 0)),
                pl.BlockSpec((1, LANES), lambda i: (0, 0))]
    args = [x, a, pm, gate, shift, scale, g, w_out_b, w_router_p, b_router_p]
    aliases = {}
    n_alias = 0
    if prev is not None:
        n_alias = len(prev)
        in_specs += [pl.BlockSpec(memory_space=pl.ANY)] * n_alias
        aliases = {len(args) + k: k for k in range(n_alias)}
        args += list(prev)
    return pl.pallas_call(
        functools.partial(_mix_out_kernel, n_experts=n_experts, aliased=n_alias),
        out_shape=out_shape,
        grid=(n // tm,),
        in_specs=in_specs,
        out_specs=(out_tok(d), out_tok(d), out_tok(LANES), out_tok(LANES),
                   pl.BlockSpec((1, SUBLANES, LANES), lambda i: (tile0 + i, 0, 0))),
        input_output_aliases=aliases,
        compiler_params=_cparams(("arbitrary",)),
        name="mix_out_router",
    )(*args)


def _chunk_sizes(tile):
    sizes = []
    s = tile
    while s >= SUBLANES:
        sizes.append(s)
        s //= 2
    return sizes


def _for_each_chunk_piece(cnt_ref, base_ref, i, n_experts, tile, fn):
    def per_expert(e, off):
        l8 = (cnt_ref[i * n_experts + e] + (SUBLANES - 1)) & (-SUBLANES)
        dst0 = base_ref[i * n_experts + e]
        done = jnp.int32(0)
        for sz in _chunk_sizes(tile):
            take = l8 & sz

            @pl.when(take != 0)
            def _(done=done, sz=sz):
                fn(pl.multiple_of(off + done, SUBLANES), pl.multiple_of(dst0 + done, SUBLANES), sz)

            done = done + take
        return off + l8

    return lax.fori_loop(0, n_experts, per_expert, jnp.int32(0))


def _one_hot_rows(pk, rows, weights=None):
    t = pk.shape[0]
    col = lax.broadcasted_iota(I32, (t, rows), 1).astype(F32)
    out = jnp.zeros((t, rows), F32)
    for k in range(TOP_K):
        w = 1.0 if weights is None else weights[:, k:k + 1]
        out = jnp.where(col == pk[:, k:k + 1], w, out)
    return out.astype(BF16)


def _dispatch_kernel(cnt_ref, base_ref, tail_ref, has_ref, h_ref, pk_ref, xs_hbm, sorted_ref, zero_ref, sem,
                     *, n_experts, tile):
    i = pl.program_id(0)

    @pl.when(i == 0)
    def _():
        zero_ref[...] = jnp.zeros(zero_ref.shape, F32)
        for e in range(n_experts):
            @pl.when(has_ref[e] == 1)
            def _(e=e):
                cp = pltpu.make_async_copy(
                    zero_ref, xs_hbm.at[pl.ds(pl.multiple_of(tail_ref[e], EXP_TILE), EXP_TILE)], sem)
                cp.start()
                cp.wait()

    onehot = _one_hot_rows(pk_ref[...], sorted_ref.shape[0])
    sorted_ref[...] = lax.dot_general(onehot, h_ref[...], (((0,), (0,)), ((), ())), preferred_element_type=F32)

    def copy(src, dst, sz):
        return pltpu.make_async_copy(sorted_ref.at[pl.ds(src, sz)], xs_hbm.at[pl.ds(dst, sz)], sem)

    _for_each_chunk_piece(cnt_ref, base_ref, i, n_experts, tile, lambda s, d, sz: copy(s, d, sz).start())
    _for_each_chunk_piece(cnt_ref, base_ref, i, n_experts, tile, lambda s, d, sz: copy(s, d, sz).wait())


def _sorted_rows(tile, n_experts):
    rows = tile * TOP_K + n_experts * (SUBLANES - 1)
    return -(-rows // LANES) * LANES


def _dispatch(h2, pk4, cnt_flat, base_flat, tail_rows, has_tile, *, r_max, n_experts):
    n, d = h2.shape
    tile = min(TOK_TILE, n)
    grid_spec = pltpu.PrefetchScalarGridSpec(
        num_scalar_prefetch=4,
        grid=(n // tile,),
        in_specs=[pl.BlockSpec((tile, d), lambda i, *_: (i, 0)),
                  pl.BlockSpec((tile, LANES), lambda i, *_: (i, 0))],
        out_specs=pl.BlockSpec(memory_space=pl.ANY),
        scratch_shapes=[pltpu.VMEM((_sorted_rows(tile, n_experts), d), F32),
                        pltpu.VMEM((EXP_TILE, d), F32), pltpu.SemaphoreType.DMA],
    )
    return pl.pallas_call(
        functools.partial(_dispatch_kernel, n_experts=n_experts, tile=tile),
        out_shape=jax.ShapeDtypeStruct((r_max, d), F32),
        grid_spec=grid_spec,
        compiler_params=_cparams(("arbitrary",)),
        name="dispatch_rows",
    )(cnt_flat, base_flat, tail_rows, has_tile, h2, pk4)


def _expert_kernel(st_e, st_j, st_n, g_t0, g_gs, g_valid,
                   xs_hbm, wg_ref, wu_ref, wd_ref, bg_ref, bu_ref, bd_ref, ys_hbm,
                   xbuf, actbuf, wgb, wub, wdb, stage_in, stage_out, sem_in, sem_out, *, n_j, n_n, tm, tf):
    s = pl.program_id(0)
    steps = n_j + n_n
    q = s // steps
    ph = s - q * steps
    valid = g_valid[q] == 1
    gs = g_gs[q]
    row0 = g_t0[q] * tm

    @pl.when(jnp.logical_and(valid, ph == 0))
    def _():
        def in_copy(i):
            slot = i & 1
            return pltpu.make_async_copy(xs_hbm.at[pl.ds(pl.multiple_of(row0 + i * tm, tm), tm)],
                                         stage_in.at[slot], sem_in.at[slot])

        @pl.when(gs >= 1)
        def _():
            in_copy(0).start()

        def load(i, carry):
            @pl.when(i + 1 < gs)
            def _():
                in_copy(i + 1).start()

            in_copy(i).wait()
            xbuf[pl.ds(pl.multiple_of(i * tm, tm), tm), :] = stage_in[i & 1].astype(BF16)
            return carry

        lax.fori_loop(0, gs, load, 0)

    @pl.when(jnp.logical_and(valid, ph < n_j))
    def _():
        wgb[...] = wg_ref[0].astype(BF16)
        wub[...] = wu_ref[0].astype(BF16)

        def tile(i, carry):
            rows = pl.ds(pl.multiple_of(i * tm, tm), tm)
            x = xbuf[rows, :]
            gate = jnp.dot(x, wgb[...], preferred_element_type=F32) + bg_ref[0]
            up = jnp.dot(x, wub[...], preferred_element_type=F32) + bu_ref[0]
            gate = jnp.minimum(gate, SWIGLU_LIMIT)
            up = jnp.clip(up, -SWIGLU_LIMIT, SWIGLU_LIMIT)
            act = (up + 1.0) * (gate * jax.nn.sigmoid(SWIGLU_ALPHA * gate))
            actbuf[ph, rows, :] = act.astype(BF16)
            return carry

        lax.fori_loop(0, gs, tile, 0)

    @pl.when(jnp.logical_and(valid, ph >= n_j))
    def _():
        wdb[...] = wd_ref[0].astype(BF16)
        col0 = pl.multiple_of((ph - n_j) * tf, tf)

        def out_copy(slot, i):
            return pltpu.make_async_copy(
                stage_out.at[slot],
                ys_hbm.at[pl.ds(pl.multiple_of(row0 + i * tm, tm), tm), pl.ds(col0, tf)],
                sem_out.at[slot])

        def tile(i, carry):
            rows = pl.ds(pl.multiple_of(i * tm, tm), tm)
            slot = i & 1

            @pl.when(i >= 2)
            def _():
                out_copy(slot, i).wait()

            y = bd_ref[0] + jnp.dot(actbuf[0, rows, :], wdb[0:tf, :], preferred_element_type=F32)
            for j in range(1, n_j):
                y = y + jnp.dot(actbuf[j, rows, :], wdb[j * tf:(j + 1) * tf, :], preferred_element_type=F32)
            stage_out[slot] = y
            out_copy(slot, i).start()
            return carry

        lax.fori_loop(0, gs, tile, 0)

        @pl.when(gs >= 1)
        def _():
            out_copy((gs - 1) & 1, 0).wait()

        @pl.when(gs >= 2)
        def _():
            out_copy(gs & 1, 0).wait()


def _expert_ffn(tables, xs, w_gate, b_gate, w_up, b_up, w_down, b_down, *, n_steps):
    st_e, st_j, st_n, g_t0, g_gs, g_valid = tables
    r_max, d = xs.shape
    n_exp, _, d_ff = w_gate.shape
    tf = min(FF_TILE, d_ff, d)
    n_j = d_ff // tf
    n_n = d // tf
    tm = EXP_TILE
    rows_g = EXP_GROUP * tm
    grid_spec = pltpu.PrefetchScalarGridSpec(
        num_scalar_prefetch=6,
        grid=(n_steps,),
        in_specs=[pl.BlockSpec(memory_space=pl.ANY),
                  pl.BlockSpec((1, d, tf), lambda s, e, j, n, *_: (e[s], 0, j[s])),
                  pl.BlockSpec((1, d, tf), lambda s, e, j, n, *_: (e[s], 0, j[s])),
                  pl.BlockSpec((1, d_ff, tf), lambda s, e, j, n, *_: (e[s], 0, n[s])),
                  pl.BlockSpec((1, 1, tf), lambda s, e, j, n, *_: (e[s], 0, j[s])),
                  pl.BlockSpec((1, 1, tf), lambda s, e, j, n, *_: (e[s], 0, j[s])),
                  pl.BlockSpec((1, 1, tf), lambda s, e, j, n, *_: (e[s], 0, n[s]))],
        out_specs=pl.BlockSpec(memory_space=pl.ANY),
        scratch_shapes=[pltpu.VMEM((rows_g, d), BF16),
                        pltpu.VMEM((n_j, rows_g, tf), BF16),
                        pltpu.VMEM((d, tf), BF16), pltpu.VMEM((d, tf), BF16), pltpu.VMEM((d_ff, tf), BF16),
                        pltpu.VMEM((2, tm, d), F32), pltpu.VMEM((2, tm, tf), F32),
                        pltpu.SemaphoreType.DMA((2,)), pltpu.SemaphoreType.DMA((2,))],
    )
    return pl.pallas_call(
        functools.partial(_expert_kernel, n_j=n_j, n_n=n_n, tm=tm, tf=tf),
        out_shape=jax.ShapeDtypeStruct((r_max, d), F32),
        grid_spec=grid_spec,
        compiler_params=_cparams(("arbitrary",)),
        name="expert_ffn",
    )(st_e, st_j, st_n, g_t0, g_gs, g_valid, xs, w_gate, w_up, w_down,
      b_gate.reshape(n_exp, 1, d_ff), b_up.reshape(n_exp, 1, d_ff), b_down.reshape(n_exp, 1, d))


def _combine_kernel(cnt_ref, base_ref, ys_hbm, x2_ref, pk_ref, gt_ref, gate_p_ref, gate_s_ref, g_ref,
                    yp_ref, ysm_ref, rows_ref, sem, *, n_experts, tile, n_prompt_tiles):
    i = pl.program_id(0)

    @pl.when(i == 0)
    def _():
        rows_ref[...] = jnp.zeros(rows_ref.shape, F32)

    def copy(dst, src, sz):
        return pltpu.make_async_copy(ys_hbm.at[pl.ds(src, sz)], rows_ref.at[pl.ds(dst, sz)], sem)

    _for_each_chunk_piece(cnt_ref, base_ref, i, n_experts, tile, lambda s, d, sz: copy(s, d, sz).start())
    _for_each_chunk_piece(cnt_ref, base_ref, i, n_experts, tile, lambda s, d, sz: copy(s, d, sz).wait())
    weights = _one_hot_rows(pk_ref[...], rows_ref.shape[0], gt_ref[...])
    y = jnp.dot(weights, rows_ref[...].astype(BF16), preferred_element_type=F32)
    is_prompt = i < n_prompt_tiles
    gate = jnp.where(is_prompt, gate_p_ref[0], gate_s_ref[...])
    x3 = x2_ref[...] + gate * y
    ms = jnp.mean(x3 * x3, axis=-1, keepdims=True)
    out = x3 * lax.rsqrt(ms + NORM_EPS) * g_ref[...]

    @pl.when(is_prompt)
    def _():
        yp_ref[...] = out

    @pl.when(jnp.logical_not(is_prompt))
    def _():
        ysm_ref[...] = out


def _combine(ys, cnt_flat, base_flat, x2, pk4, gt, gate_p, gate_s, final_g, *, n_prompt, tiles_per_seq, n_experts):
    n, d = x2.shape
    tile = min(TOK_TILE, n_prompt)
    n_s = n - n_prompt
    assert n_s == tile and n_prompt % tile == 0
    npt = n_prompt // tile
    grid_spec = pltpu.PrefetchScalarGridSpec(
        num_scalar_prefetch=2,
        grid=(n // tile,),
        in_specs=[pl.BlockSpec(memory_space=pl.ANY),
                  pl.BlockSpec((tile, d), lambda i, *_: (i, 0)),
                  pl.BlockSpec((tile, LANES), lambda i, *_: (i, 0)),
                  pl.BlockSpec((tile, LANES), lambda i, *_: (i, 0)),
                  pl.BlockSpec((1, 1, d), lambda i, *_: (jnp.minimum(i, npt - 1) // tiles_per_seq, 0, 0)),
                  pl.BlockSpec((tile, d), lambda i, *_: (0, 0)),
                  pl.BlockSpec((1, d), lambda i, *_: (0, 0))],
        out_specs=(pl.BlockSpec((tile, d), lambda i, *_: (jnp.minimum(i, npt - 1), 0)),
                   pl.BlockSpec((tile, d), lambda i, *_: (0, 0))),
        scratch_shapes=[pltpu.VMEM((_sorted_rows(tile, n_experts), d), F32), pltpu.SemaphoreType.DMA],
    )
    return pl.pallas_call(
        functools.partial(_combine_kernel, n_experts=n_experts, tile=tile, n_prompt_tiles=npt),
        out_shape=(jax.ShapeDtypeStruct((n_prompt, d), F32), jax.ShapeDtypeStruct((n_s, d), F32)),
        grid_spec=grid_spec,
        compiler_params=_cparams(("arbitrary",)),
        name="combine_norm",
    )(cnt_flat, base_flat, ys, x2, pk4, gt, gate_p, gate_s, final_g)


def _expert_tables(tile_cnt, *, tile, n_j, n_n):
    n_tiles, n_experts = tile_cnt.shape
    tm, grp = EXP_TILE, EXP_GROUP
    cnt8 = (tile_cnt + (SUBLANES - 1)) // SUBLANES * SUBLANES
    rows_e = jnp.sum(cnt8, axis=0)
    max_rows = n_tiles * tile * TOP_K + n_experts * n_tiles * (SUBLANES - 1)
    t_max = -(-max_rows // tm) + n_experts
    ng_max = n_experts + t_max // grp
    ntile = (rows_e + tm - 1) // tm
    tile_start = jnp.cumsum(ntile) - ntile
    pstart = tile_start * tm
    base = pstart[None, :] + jnp.cumsum(cnt8, axis=0) - cnt8
    ng = (ntile + grp - 1) // grp
    cg = jnp.cumsum(ng)
    n_groups = cg[-1]
    q = jnp.arange(ng_max, dtype=I32)
    eq = jnp.minimum(jnp.searchsorted(cg, q, side="right").astype(I32), n_experts - 1)
    lg = q - (cg[eq] - ng[eq])
    t0 = tile_start[eq] + lg * grp
    gs = jnp.clip(ntile[eq] - lg * grp, 0, grp)
    valid = q < n_groups
    last = jnp.maximum(n_groups - 1, 0)
    eq = jnp.where(valid, eq, eq[last])
    t0 = jnp.where(valid, t0, t0[last])
    gs = jnp.where(valid, gs, 0)
    steps = n_j + n_n
    s = jnp.arange(ng_max * steps, dtype=I32)
    qs = s // steps
    ph = s - qs * steps
    vs = valid[qs]
    st_e = eq[qs]
    st_j = jnp.where(vs, jnp.minimum(ph, n_j - 1), n_j - 1)
    st_n = jnp.where(vs, jnp.maximum(ph - n_j, 0), n_n - 1)
    tail = jnp.maximum(pstart + (ntile - 1) * tm, 0)
    has = (ntile > 0).astype(I32)
    tabs = tuple(a.astype(I32) for a in (st_e, st_j, st_n, t0, gs, valid))
    return (tabs, tile_cnt.reshape(-1).astype(I32), base.reshape(-1).astype(I32), tail.astype(I32), has,
            t_max * tm, (n_groups * steps).astype(I32))


def _rope_tables(pos):
    inv = 1.0 / (ROPE_THETA ** (jnp.arange(0, HEAD_DIM, 2, dtype=F32) / HEAD_DIM))
    ang = pos.astype(F32)[:, None] * inv[None, :]
    reps = LANES // (HEAD_DIM // 2)
    return jnp.tile(jnp.cos(ang), (1, reps)), jnp.tile(jnp.sin(ang), (1, reps))


def kernel(x_prompt, x_sample, cache_k, cache_v, state_pool, page_table, c_prompt, c_sample, w_ada, b_ada, norm1_g, norm2_g, w_in, lam_q1, lam_k1, lam_q2, lam_k2, subln_g, w_pool, pool_scale, w_out, w_router, b_router, w_gate, b_gate, w_up, b_up, w_down, b_down, final_g):
    B, S, D = x_prompt.shape
    DB, T, _ = x_sample.shape
    depth = w_ada.shape[0]
    page = cache_k.shape[2]
    past = page_table.shape[1] * page
    d_att = D // 2
    n_heads = d_att // HEAD_W
    d_pool = w_in.shape[2] - 3 * d_att
    n_experts = w_router.shape[2]
    state_len = state_pool.shape[2]
    n_p, n_s = B * S, DB * T
    n_all = n_p + n_s
    tm = min(TOK_TILE, n_p)
    tiles_per_seq = S // tm
    assert depth == 1, "single-layer step"
    assert n_s == tm and S % tm == 0 and T % 8 == 0 and state_len < POOL_HALO <= tm

    cos_p, sin_p = _rope_tables(jnp.arange(S))
    cos_s, sin_s = _rope_tables(jnp.tile(past + jnp.arange(T), DB))

    l = 0
    lam_init = 0.8 - 0.6 * math.exp(-0.3 * l)
    lamp = jnp.zeros((8, LANES), F32)
    for r, vec in enumerate((lam_q1[l], lam_k1[l], lam_q2[l], lam_k2[l])):
        lamp = lamp.at[r, :HEAD_DIM].set(vec.astype(F32))
    subg = subln_g[l].reshape(1, HEAD_W)

    rows_c = -(-(B + DB) // 8) * 8
    c_all = jnp.zeros((rows_c, D), F32).at[:B].set(c_prompt).at[B:B + DB].set(c_sample)
    m_all = _adaln(c_all, w_ada[l], b_ada[l])
    mods_p = [m_all[:B, k * D:(k + 1) * D].reshape(B, 1, D) for k in range(N_ADA)]
    mods_s = [jnp.repeat(m_all[B:B + DB, k * D:(k + 1) * D], T, axis=0).reshape(1, n_s, D) for k in range(N_ADA)]

    w_in_b = _cast_bf16(w_in[l], 256)
    w_out_b = _cast_bf16(w_out[l], 256)
    ng, gw = w_pool.shape[1], w_pool.shape[2]
    w_pool_b = _cast_bf16(w_pool[l].reshape(ng * gw, gw), ng * gw).reshape(ng, gw, gw)
    g1 = norm1_g[l].reshape(1, D)
    g2 = norm2_g[l].reshape(1, D)
    pscale = pool_scale[l].reshape(1, d_pool)
    w_router_p = jnp.zeros((D, LANES), F32).at[:, :n_experts].set(w_router[l])
    b_router_p = jnp.zeros((1, LANES), F32).at[0, :n_experts].set(b_router[l].astype(F32))

    xp = x_prompt.reshape(n_p, D)
    q_p, k_p, v_p, u_p, kb_p, vb_p = _project(xp, mods_p[0], mods_p[1], g1, w_in_b, cos_p, sin_p,
                                              tiles_per_seq=tiles_per_seq, pos_tiles=tiles_per_seq)
    a_p = _attn_prompt(q_p, kb_p, vb_p, lamp, subg.reshape(HEAD_W, 1), batch=B, seq=S, lam_init=lam_init)
    pm_p = _pool_mix(u_p.reshape(B, S, d_pool), w_pool_b, pscale, tile=tm, cur_block0=0, pos0=0,
                     zero_first_halo=True).reshape(n_p, d_pool)

    xs_tok = x_sample.reshape(n_s, D)
    q_s, k_s, v_s, u_s, _, _ = _project(xs_tok, mods_s[0], mods_s[1], g1, w_in_b, cos_s, sin_s,
                                        tiles_per_seq=1, pos_tiles=1)
    a_s = _attn_sample(page_table, q_s, k_s, v_s, cache_k[l], cache_v[l], lamp, subg, t_new=T, lam_init=lam_init)
    u_ext = jnp.concatenate([jnp.zeros((DB, POOL_HALO - state_len, d_pool), F32),
                             state_pool[l].astype(F32), u_s.reshape(DB, T, d_pool)], axis=1)
    pm_s = _pool_mix(u_ext, w_pool_b, pscale, tile=T, cur_block0=POOL_HALO // T, pos0=past,
                     zero_first_halo=False).reshape(n_s, d_pool)

    outs = _mix_out(xp, a_p, pm_p, mods_p[2], mods_p[3], mods_p[4], g2, w_out_b, w_router_p, b_router_p,
                    n_total=n_all, tile0=0, tiles_per_seq=tiles_per_seq, n_experts=n_experts)
    x2, h2, pk4, gt4, tcnt = _mix_out(xs_tok, a_s, pm_s, mods_s[2], mods_s[3], mods_s[4], g2, w_out_b, w_router_p,
                                      b_router_p, n_total=n_all, tile0=n_p // tm, tiles_per_seq=1,
                                      n_experts=n_experts, prev=outs)

    d_ff = w_gate.shape[3]
    tf = min(FF_TILE, d_ff, D)
    tile_cnt = tcnt[:, 0, :n_experts].astype(I32)
    tabs, cnt_flat, base_flat, tail, has, r_max, n_steps = _expert_tables(
        tile_cnt, tile=tm, n_j=d_ff // tf, n_n=D // tf)
    xs_rows = _dispatch(h2, pk4, cnt_flat, base_flat, tail, has, r_max=r_max, n_experts=n_experts)
    ys_rows = _expert_ffn(tabs, xs_rows, w_gate[l], b_gate[l], w_up[l], b_up[l], w_down[l], b_down[l],
                          n_steps=n_steps)
    y_p, y_s = _combine(ys_rows, cnt_flat, base_flat, x2, pk4, gt4, mods_p[5], mods_s[5].reshape(n_s, D),
                        final_g.reshape(1, D), n_prompt=n_p, tiles_per_seq=tiles_per_seq, n_experts=n_experts)

    n_pages_p = S // page
    k_prompt = k_p.reshape(1, B, n_pages_p, page, n_heads, HEAD_W)
    v_prompt = v_p.reshape(1, B, n_pages_p, page, n_heads, HEAD_W)
    pool_prompt = u_p.reshape(B, S, d_pool)[:, S - state_len:][None]
    k_sample = k_s.reshape(1, DB, T, n_heads, HEAD_W)
    v_sample = v_s.reshape(1, DB, T, n_heads, HEAD_W)
    pool_sample = u_ext[:, -state_len:][None]
    return (y_p.reshape(B, S, D), y_s.reshape(DB, T, D), k_prompt, v_prompt, pool_prompt,
            k_sample, v_sample, pool_sample)
```

```python
import functools
import math

import jax
import jax.numpy as jnp
from jax import lax
from jax.experimental import pallas as pl
from jax.experimental.pallas import tpu as pltpu

F32 = jnp.float32
BF16 = jnp.bfloat16
I32 = jnp.int32

HEAD_DIM = 64
HEAD_W = 2 * HEAD_DIM
POOL_WINDOWS = (2, 4, 8, 16)
POOL_HALO = 16
TOP_K = 4
SWIGLU_LIMIT = 7.0
SWIGLU_ALPHA = 1.702
ROPE_THETA = 10000.0
NORM_EPS = 1e-5
N_ADA = 6
LANES = 128
SUBLANES = 8
V7X_VMEM_LIMIT = 58 * 1024 * 1024

TOK_TILE = 256
EXP_TILE = 256
EXP_GROUP = 6
FF_TILE = 512
Q_SCALE = (HEAD_DIM ** -0.5) * math.log2(math.e)


def _cparams(sem, vmem=V7X_VMEM_LIMIT):
    return pltpu.CompilerParams(dimension_semantics=sem, vmem_limit_bytes=vmem)


def _cast_kernel(x_ref, o_ref):
    o_ref[...] = x_ref[...].astype(o_ref.dtype)


def _cast_bf16(w, rows):
    r, c = w.shape
    return pl.pallas_call(
        _cast_kernel,
        out_shape=jax.ShapeDtypeStruct((r, c), BF16),
        grid=(r // rows,),
        in_specs=[pl.BlockSpec((rows, c), lambda i: (i, 0))],
        out_specs=pl.BlockSpec((rows, c), lambda i: (i, 0)),
        compiler_params=_cparams(("arbitrary",)),
        name="cast_bf16",
    )(w)


def _ada_kernel(c_ref, w_ref, b_ref, o_ref):
    c = c_ref[...]
    s = (c * jax.nn.sigmoid(c)).astype(BF16)
    o_ref[...] = jnp.dot(s, w_ref[...].astype(BF16), preferred_element_type=F32) + b_ref[...]


def _adaln(c_all, w_ada, b_ada):
    rows, d = c_all.shape
    n = w_ada.shape[1]
    tn = min(1024, n)
    return pl.pallas_call(
        _ada_kernel,
        out_shape=jax.ShapeDtypeStruct((rows, n), F32),
        grid=(n // tn,),
        in_specs=[pl.BlockSpec((rows, d), lambda j: (0, 0)),
                  pl.BlockSpec((d, tn), lambda j: (0, j)),
                  pl.BlockSpec((1, tn), lambda j: (0, j))],
        out_specs=pl.BlockSpec((rows, tn), lambda j: (0, j)),
        compiler_params=_cparams(("arbitrary",)),
        name="adaln",
    )(c_all, w_ada, b_ada.reshape(1, n))


def _modulated_norm(x, g, shift, scale):
    ms = jnp.mean(x * x, axis=-1, keepdims=True)
    return (x * lax.rsqrt(ms + NORM_EPS) * g) * (1.0 + scale) + shift


def _proj_kernel(x_ref, shift_ref, scale_ref, g_ref, w_ref, cos_ref, sin_ref,
                 q_ref, k_ref, v_ref, u_ref, kb_ref, vb_ref, *, d_att):
    h = _modulated_norm(x_ref[...], g_ref[...], shift_ref[0], scale_ref[0]).astype(BF16)
    cos = cos_ref[...]
    sin = sin_ref[...]
    lane = lax.broadcasted_iota(I32, cos.shape, 1)
    first_half = (lane & (HEAD_DIM - 1)) < (HEAD_DIM // 2)

    def rope(z):
        rot = jnp.where(first_half, -pltpu.roll(z, LANES - HEAD_DIM // 2, 1), pltpu.roll(z, HEAD_DIM // 2, 1))
        return z * cos + rot * sin

    zq = jnp.dot(h, w_ref[:, 0:d_att], preferred_element_type=F32)
    zk = jnp.dot(h, w_ref[:, d_att:2 * d_att], preferred_element_type=F32)
    for hh in range(d_att // HEAD_W):
        sl = slice(hh * HEAD_W, (hh + 1) * HEAD_W)
        q_ref[:, sl] = rope(zq[:, sl]) * Q_SCALE
        kr = rope(zk[:, sl])
        k_ref[:, sl] = kr
        kb_ref[:, sl] = kr.astype(BF16)
    zv = jnp.dot(h, w_ref[:, 2 * d_att:3 * d_att], preferred_element_type=F32)
    v_ref[...] = zv
    vb_ref[...] = zv.astype(BF16)
    u_ref[...] = jnp.dot(h, w_ref[:, 3 * d_att:], preferred_element_type=F32)


def _project(x, shift, scale, g, w_in_b, cos, sin, *, tiles_per_seq, pos_tiles):
    n, d = x.shape
    tm = min(TOK_TILE, n)
    d_in = w_in_b.shape[1]
    d_att = (d // 2)
    d_pool = d_in - 3 * d_att
    mod_rows = shift.shape[1]
    mod_spec = pl.BlockSpec((1, mod_rows, d), lambda i: (i // tiles_per_seq, 0, 0))
    tok = lambda c: pl.BlockSpec((tm, c), lambda i: (i, 0))
    return pl.pallas_call(
        functools.partial(_proj_kernel, d_att=d_att),
        out_shape=(jax.ShapeDtypeStruct((n, d_att), F32), jax.ShapeDtypeStruct((n, d_att), F32),
                   jax.ShapeDtypeStruct((n, d_att), F32), jax.ShapeDtypeStruct((n, d_pool), F32),
                   jax.ShapeDtypeStruct((n, d_att), BF16), jax.ShapeDtypeStruct((n, d_att), BF16)),
        grid=(n // tm,),
        in_specs=[tok(d), mod_spec, mod_spec,
                  pl.BlockSpec((1, d), lambda i: (0, 0)),
                  pl.BlockSpec((d, d_in), lambda i: (0, 0)),
                  pl.BlockSpec((tm, LANES), lambda i: (i % pos_tiles, 0)),
                  pl.BlockSpec((tm, LANES), lambda i: (i % pos_tiles, 0))],
        out_specs=(tok(d_att), tok(d_att), tok(d_att), tok(d_pool), tok(d_att), tok(d_att)),
        compiler_params=_cparams(("arbitrary",)),
        name="in_proj",
    )(x, shift, scale, g, w_in_b, cos, sin)


def _lambda_value(lam_ref, lam_init):
    lp = lam_ref[...]
    a = jnp.sum(lp[0:1] * lp[1:2], axis=1, keepdims=True)
    b = jnp.sum(lp[2:3] * lp[3:4], axis=1, keepdims=True)
    return jnp.exp(a) - jnp.exp(b) + lam_init


def _stack_maps(q):
    lane = lax.broadcasted_iota(I32, q.shape, 1)
    q1 = jnp.where(lane < HEAD_DIM, q, 0.0)
    q2 = jnp.where(lane >= HEAD_DIM, q, 0.0)
    return jnp.concatenate([q1, q2], axis=0).astype(BF16)


def _softmax_update(s, vb, m_prev, l_prev, acc_prev):
    reps = s.shape[1] // LANES
    m_cur = jnp.max(s, axis=1, keepdims=True)
    m_new = jnp.maximum(m_prev, m_cur)
    m_wide = m_new if reps == 1 else jnp.concatenate([m_new] * reps, axis=1)
    p = jnp.exp2(s - m_wide)
    alpha = jnp.exp2(m_prev - m_new)
    l_new = alpha * l_prev + jnp.sum(p, axis=1, keepdims=True)
    acc_new = acc_prev * alpha + jnp.dot(p.astype(BF16), vb, preferred_element_type=F32)
    return m_new, l_new, acc_new


def _diff_finish(l, acc, lam, g, t, lam_init):
    o = acc[:t] / l[:t] - lam * (acc[t:] / l[t:])
    ms = jnp.mean(o * o, axis=1, keepdims=True)
    return o * lax.rsqrt(ms + NORM_EPS) * g * (1.0 - lam_init)


_NT = (((1,), (1,)), ((), ()))


_TN = (((0,), (0,)), ((), ()))


def _attn_prompt_kernel(lam_ref, g_ref, q_ref, k_ref, k2_ref, v_ref, o_ref, m_ref, l_ref, acc_ref,
                        *, tq, tk, lam_init):
    qi = pl.program_id(2)
    heads = q_ref.shape[1] // HEAD_W
    qqs = [_stack_maps(q_ref[:, hh * HEAD_W:(hh + 1) * HEAD_W]) for hh in range(heads)]
    m_ref[...] = jnp.full(m_ref.shape, -jnp.inf, F32)
    l_ref[...] = jnp.zeros(l_ref.shape, F32)
    acc_ref[...] = jnp.zeros(acc_ref.shape, F32)

    def block(j, masked):
        rows = pl.ds(pl.multiple_of(j * tk, tk), tk)
        for hh in range(heads):
            cols = slice(hh * HEAD_W, (hh + 1) * HEAD_W)

            def scores(kref):
                st = lax.dot_general(kref[rows, cols], qqs[hh], _NT, preferred_element_type=F32)
                if masked:
                    kpos = j * tk + lax.broadcasted_iota(I32, st.shape, 0)
                    c = lax.broadcasted_iota(I32, st.shape, 1)
                    qpos = qi * tq + jnp.where(c >= tq, c - tq, c)
                    st = jnp.where(kpos <= qpos, st, -jnp.inf)
                return st

            m_prev = m_ref[hh]
            m_new = jnp.maximum(m_prev, jnp.max(scores(k_ref), axis=0, keepdims=True))
            p = jnp.exp2(scores(k2_ref) - m_new)
            alpha = jnp.exp2(m_prev - m_new)
            l_ref[hh] = alpha * l_ref[hh] + jnp.sum(p, axis=0, keepdims=True)
            pv = lax.dot_general(v_ref[rows, cols], p.astype(BF16), _TN, preferred_element_type=F32)
            acc_ref[hh] = acc_ref[hh] * alpha + pv
            m_ref[hh] = m_new

    n_full = (qi * tq) // tk

    def body(j, carry):
        block(j, False)
        return carry

    lax.fori_loop(0, n_full, body, 0)
    block(n_full, True)

    lam = _lambda_value(lam_ref, lam_init)
    for hh in range(heads):
        l = l_ref[hh]
        acc = acc_ref[hh]
        o = acc[:, :tq] / l[:, :tq] - lam * (acc[:, tq:] / l[:, tq:])
        ms = jnp.mean(o * o, axis=0, keepdims=True)
        a = o * lax.rsqrt(ms + NORM_EPS) * g_ref[...] * (1.0 - lam_init)
        o_ref[:, hh * HEAD_W:(hh + 1) * HEAD_W] = a.T.astype(o_ref.dtype)


def _attn_prompt(q, kb, vb, lamp, g_col, *, batch, seq, lam_init):
    n, d_att = q.shape
    nh = d_att // HEAD_W
    hp = 2 if nh % 2 == 0 else 1
    tq = min(256, seq)
    tk = min(512, seq)
    nq = seq // tq
    return pl.pallas_call(
        functools.partial(_attn_prompt_kernel, tq=tq, tk=tk, lam_init=lam_init),
        out_shape=jax.ShapeDtypeStruct((n, d_att), BF16),
        grid=(batch, nh // hp, nq),
        in_specs=[pl.BlockSpec((8, LANES), lambda b, h, i: (0, 0)),
                  pl.BlockSpec((HEAD_W, 1), lambda b, h, i: (0, 0)),
                  pl.BlockSpec((tq, hp * HEAD_W), lambda b, h, i: (b * nq + i, h)),
                  pl.BlockSpec((seq, hp * HEAD_W), lambda b, h, i: (b, h)),
                  pl.BlockSpec((seq, hp * HEAD_W), lambda b, h, i: (b, h)),
                  pl.BlockSpec((seq, hp * HEAD_W), lambda b, h, i: (b, h))],
        out_specs=pl.BlockSpec((tq, hp * HEAD_W), lambda b, h, i: (b * nq + i, h)),
        scratch_shapes=[pltpu.VMEM((hp, 1, 2 * tq), F32), pltpu.VMEM((hp, 1, 2 * tq), F32),
                        pltpu.VMEM((hp, HEAD_W, 2 * tq), F32)],
        compiler_params=_cparams(("arbitrary", "arbitrary", "arbitrary")),
        name="attn_prompt",
    )(lamp, g_col, q, kb, kb, vb)


def _attn_sample_kernel(pt_ref, lam_ref, g_ref, q_ref, kn_ref, vn_ref, *rest, n_heads, pages, t_new, lam_init):
    k_pages = rest[:pages]
    v_pages = rest[pages:2 * pages]
    o_ref = rest[2 * pages]
    m_ref, l_ref, acc_ref = rest[2 * pages + 1:]
    c = pl.program_id(1)
    page = k_pages[0].shape[1] // n_heads

    @pl.when(c == 0)
    def _():
        m_ref[...] = jnp.full(m_ref.shape, -jnp.inf, F32)
        l_ref[...] = jnp.zeros(l_ref.shape, F32)
        acc_ref[...] = jnp.zeros(acc_ref.shape, F32)

    q = q_ref[...]
    qqs = [_stack_maps(q[:, hh * HEAD_W:(hh + 1) * HEAD_W]) for hh in range(n_heads)]

    def update(hh, kh, vh, mask):
        s = lax.dot_general(qqs[hh], kh, _NT, preferred_element_type=F32)
        if mask is not None:
            s = jnp.where(mask, s, -jnp.inf)
        m, l, acc = _softmax_update(s, vh, m_ref[hh], l_ref[hh], acc_ref[hh])
        m_ref[hh] = m
        l_ref[hh] = l
        acc_ref[hh] = acc

    def head_rows(refs, hh):
        parts = [r[0, pl.ds(hh, page, stride=n_heads), :] for r in refs]
        return jnp.concatenate(parts, axis=0).astype(BF16)

    for hh in range(n_heads):
        update(hh, head_rows(k_pages, hh), head_rows(v_pages, hh), None)

    @pl.when(c == pl.num_programs(1) - 1)
    def _():
        r = lax.broadcasted_iota(I32, (2 * t_new, LANES), 0)
        col = lax.broadcasted_iota(I32, (2 * t_new, LANES), 1)
        r = jnp.where(r >= t_new, r - t_new, r)
        mask = col <= r
        lam = _lambda_value(lam_ref, lam_init)
        pad = jnp.zeros((LANES - t_new, HEAD_W), F32)
        for hh in range(n_heads):
            sl = slice(hh * HEAD_W, (hh + 1) * HEAD_W)
            kn = jnp.concatenate([kn_ref[:, sl], pad], axis=0).astype(BF16)
            vn = jnp.concatenate([vn_ref[:, sl], pad], axis=0).astype(BF16)
            update(hh, kn, vn, mask)
            o_ref[:, sl] = _diff_finish(l_ref[hh], acc_ref[hh], lam, g_ref[...], t_new, lam_init)


def _attn_sample(page_table, q, k_new, v_new, cache_k, cache_v, lamp, g, *, t_new, lam_init):
    n, d_att = q.shape
    nh = d_att // HEAD_W
    db, n_pages = page_table.shape
    n_pool, page = cache_k.shape[0], cache_k.shape[1]
    pages = min(8, n_pages)
    ck = cache_k.reshape(n_pool, page * nh, HEAD_W)
    cv = cache_v.reshape(n_pool, page * nh, HEAD_W)

    def page_spec(i):
        return pl.BlockSpec((1, page * nh, HEAD_W), lambda b, c, pt: (pt[b, c * pages + i], 0, 0))

    tok = pl.BlockSpec((t_new, d_att), lambda b, c, pt: (b, 0))
    grid_spec = pltpu.PrefetchScalarGridSpec(
        num_scalar_prefetch=1,
        grid=(db, n_pages // pages),
        in_specs=[pl.BlockSpec((8, LANES), lambda b, c, pt: (0, 0)),
                  pl.BlockSpec((1, HEAD_W), lambda b, c, pt: (0, 0)),
                  tok, tok, tok]
                 + [page_spec(i) for i in range(pages)] + [page_spec(i) for i in range(pages)],
        out_specs=tok,
        scratch_shapes=[pltpu.VMEM((nh, 2 * t_new, HEAD_W), F32)] * 3,
    )
    return pl.pallas_call(
        functools.partial(_attn_sample_kernel, n_heads=nh, pages=pages, t_new=t_new, lam_init=lam_init),
        out_shape=jax.ShapeDtypeStruct((n, d_att), F32),
        grid_spec=grid_spec,
        compiler_params=_cparams(("arbitrary", "arbitrary")),
        name="attn_sample",
    )(page_table, lamp, g, q, k_new, v_new, *([ck] * pages), *([cv] * pages))


def _pool_kernel(halo_ref, cur_ref, w_ref, scale_ref, o_ref, ext_ref, *, pos0, tile_pos, zero_first_halo):
    i = pl.program_id(1)
    t = cur_ref.shape[1]
    halo = halo_ref[0]
    if zero_first_halo:
        halo = jnp.where(i == 0, 0.0, halo)
    cur = cur_ref[0]
    ext_ref[0:POOL_HALO, :] = halo
    ext_ref[POOL_HALO:POOL_HALO + t, :] = cur
    pos = pos0 + i * tile_pos + lax.broadcasted_iota(I32, (t, 1), 0)
    gw = cur.shape[1] // len(POOL_WINDOWS)
    for gi, w in enumerate(POOL_WINDOWS):
        cols = slice(gi * gw, (gi + 1) * gw)
        total = cur[:, cols]
        for j in range(1, w):
            total = total + ext_ref[POOL_HALO - j:POOL_HALO - j + t, cols]
        cnt = jnp.minimum(pos + 1, w).astype(F32)
        dlt = total / cnt - cur[:, cols]
        y = jnp.dot(dlt.astype(BF16), w_ref[gi], preferred_element_type=F32)
        o_ref[0, :, cols] = (y * scale_ref[:, cols]).astype(o_ref.dtype)


def _pool_mix(u3, w_pool_b, scale, *, tile, cur_block0, pos0, zero_first_halo):
    b, rows, c = u3.shape
    n_tiles = (rows - cur_block0 * tile) // tile
    ng, gw = w_pool_b.shape[0], w_pool_b.shape[1]
    assert (cur_block0 * tile) % POOL_HALO == 0 and (tile % POOL_HALO == 0 or n_tiles == 1)

    def halo_map(bi, i):
        return (bi, jnp.maximum(((cur_block0 + i) * tile) // POOL_HALO - 1, 0), 0)

    return pl.pallas_call(
        functools.partial(_pool_kernel, pos0=pos0, tile_pos=tile, zero_first_halo=zero_first_halo),
        out_shape=jax.ShapeDtypeStruct((b, n_tiles * tile, c), BF16),
        grid=(b, n_tiles),
        in_specs=[pl.BlockSpec((1, POOL_HALO, c), halo_map),
                  pl.BlockSpec((1, tile, c), lambda bi, i: (bi, cur_block0 + i, 0)),
                  pl.BlockSpec((ng, gw, gw), lambda bi, i: (0, 0, 0)),
                  pl.BlockSpec((1, c), lambda bi, i: (0, 0))],
        out_specs=pl.BlockSpec((1, tile, c), lambda bi, i: (bi, i, 0)),
        scratch_shapes=[pltpu.VMEM((POOL_HALO + tile, c), F32)],
        compiler_params=_cparams(("arbitrary", "arbitrary")),
        name="pool_mix",
    )(u3, u3, w_pool_b, scale)


def _split_bf16(x):
    hi = x.astype(BF16)
    lo = (x - hi.astype(F32)).astype(BF16)
    return hi, lo


def _round_up_f32(x, m):
    return jnp.floor((x + (m - 1.0)) * (1.0 / m)) * m


def _mix_out_kernel(*refs, n_experts, aliased):
    (x_ref, a_ref, pm_ref, gate_ref, shift_ref, scale_ref, g_ref, wo_ref, wr_ref, br_ref) = refs[:10]
    x2_ref, h2_ref, pk_ref, gt_ref, tc_ref = refs[10 + aliased:]
    d_att = a_ref.shape[1]
    mix = (jnp.dot(a_ref[...].astype(BF16), wo_ref[0:d_att, :], preferred_element_type=F32)
           + jnp.dot(pm_ref[...], wo_ref[d_att:, :], preferred_element_type=F32))
    x2 = x_ref[...] + gate_ref[0] * mix
    x2_ref[...] = x2
    h2 = _modulated_norm(x2, g_ref[...], shift_ref[0], scale_ref[0])
    h2_ref[...] = h2.astype(BF16)
    hh, hl = _split_bf16(h2)
    wh, wl = _split_bf16(wr_ref[...])
    logits = (jnp.dot(hh, wh, preferred_element_type=F32) + jnp.dot(hl, wh, preferred_element_type=F32)
              + jnp.dot(hh, wl, preferred_element_type=F32)) + br_ref[...]
    t = logits.shape[0]
    lane = lax.broadcasted_iota(I32, logits.shape, 1)
    lanef = lane.astype(F32)
    work = jnp.where(lane < n_experts, logits, -jnp.inf)
    vals, ids = [], []
    for _ in range(TOP_K):
        mx = jnp.max(work, axis=1, keepdims=True)
        ix = jnp.min(jnp.where(work == mx, lanef, float(LANES)), axis=1, keepdims=True)
        vals.append(mx)
        ids.append(ix)
        work = jnp.where(lanef == ix, -jnp.inf, work)
    es = [jnp.exp(v - vals[0]) for v in vals]
    den = es[0]
    for e in es[1:]:
        den = den + e
    sel = jnp.zeros(logits.shape, F32)
    for k in range(TOP_K):
        sel = jnp.where(lanef == ids[k], 1.0, sel)
    r = lax.broadcasted_iota(I32, (t, t), 0)
    c = lax.broadcasted_iota(I32, (t, t), 1)
    earlier = jnp.where(c < r, 1.0, 0.0).astype(BF16)
    local_rank = jnp.dot(earlier, sel.astype(BF16), preferred_element_type=F32)
    cnt = jnp.sum(sel, axis=0, keepdims=True)
    cnt8 = jnp.broadcast_to(_round_up_f32(cnt, float(SUBLANES)), (SUBLANES, LANES))
    er = lax.broadcasted_iota(I32, (LANES, LANES), 0)
    ec = lax.broadcasted_iota(I32, (LANES, LANES), 1)
    before = jnp.where(er < ec, 1.0, 0.0).astype(BF16)
    chunk_off = jnp.dot(cnt8.astype(BF16), before, preferred_element_type=F32)[0:1, :]
    pos = local_rank + chunk_off
    pk_out = jnp.zeros(logits.shape, F32)
    gt_out = jnp.zeros(logits.shape, F32)
    for k in range(TOP_K):
        pk = jnp.sum(jnp.where(lanef == ids[k], pos, 0.0), axis=1, keepdims=True)
        pk_out = jnp.where(lane == k, pk, pk_out)
        gt_out = jnp.where(lane == k, es[k] / den, gt_out)
    pk_ref[...] = pk_out
    gt_ref[...] = gt_out
    tc_ref[0] = jnp.broadcast_to(cnt, (SUBLANES, LANES))


def _mix_out(x, a, pm, gate, shift, scale, g, w_out_b, w_router_p, b_router_p, *, n_total, tile0,
             tiles_per_seq, n_experts, prev=None):
    n, d = x.shape
    tm = min(TOK_TILE, n)
    d_att = a.shape[1]
    mod_rows = gate.shape[1]
    mod_spec = pl.BlockSpec((1, mod_rows, d), lambda i: (i // tiles_per_seq, 0, 0))
    tok = lambda c: pl.BlockSpec((tm, c), lambda i: (i, 0))
    out_tok = lambda c: pl.BlockSpec((tm, c), lambda i: (tile0 + i, 0))
    out_shape = (jax.ShapeDtypeStruct((n_total, d), F32), jax.ShapeDtypeStruct((n_total, d), BF16),
                 jax.ShapeDtypeStruct((n_total, LANES), F32), jax.ShapeDtypeStruct((n_total, LANES), F32),
                 jax.ShapeDtypeStruct((n_total // tm, SUBLANES, LANES), F32))
    in_specs = [tok(d), tok(d_att), tok(pm.shape[1]), mod_spec, mod_spec, mod_spec,
                pl.BlockSpec((1, d), lambda i: (0, 0)),
                pl.BlockSpec(w_out_b.shape, lambda i: (0, 0)),
                pl.BlockSpec(w_router_p.shape, lambda i: (0, 0)),
                pl.BlockSpec((1, LANES), lambda i: (0, 0))]
    args = [x, a, pm, gate, shift, scale, g, w_out_b, w_router_p, b_router_p]
    aliases = {}
    n_alias = 0
    if prev is not None:
        n_alias = len(prev)
        in_specs += [pl.BlockSpec(memory_space=pl.ANY)] * n_alias
        aliases = {len(args) + k: k for k in range(n_alias)}
        args += list(prev)
    return pl.pallas_call(
        functools.partial(_mix_out_kernel, n_experts=n_experts, aliased=n_alias),
        out_shape=out_shape,
        grid=(n // tm,),
        in_specs=in_specs,
        out_specs=(out_tok(d), out_tok(d), out_tok(LANES), out_tok(LANES),
                   pl.BlockSpec((1, SUBLANES, LANES), lambda i: (tile0 + i, 0, 0))),
        input_output_aliases=aliases,
        compiler_params=_cparams(("arbitrary",)),
        name="mix_out_router",
    )(*args)


def _chunk_sizes(tile):
    sizes = []
    s = tile
    while s >= SUBLANES:
        sizes.append(s)
        s //= 2
    return sizes


def _for_each_chunk_piece(cnt_ref, base_ref, i, n_experts, tile, fn):
    def per_expert(e, off):
        l8 = (cnt_ref[i * n_experts + e] + (SUBLANES - 1)) & (-SUBLANES)
        dst0 = base_ref[i * n_experts + e]
        done = jnp.int32(0)
        for sz in _chunk_sizes(tile):
            take = l8 & sz

            @pl.when(take != 0)
            def _(done=done, sz=sz):
                fn(pl.multiple_of(off + done, SUBLANES), pl.multiple_of(dst0 + done, SUBLANES), sz)

            done = done + take
        return off + l8

    return lax.fori_loop(0, n_experts, per_expert, jnp.int32(0))


def _one_hot_rows(pk, rows, weights=None):
    t = pk.shape[0]
    col = lax.broadcasted_iota(I32, (t, rows), 1).astype(F32)
    out = jnp.zeros((t, rows), F32)
    for k in range(TOP_K):
        w = 1.0 if weights is None else weights[:, k:k + 1]
        out = jnp.where(col == pk[:, k:k + 1], w, out)
    return out.astype(BF16)


def _dispatch_kernel(cnt_ref, base_ref, tail_ref, has_ref, h_ref, pk_ref, xs_hbm, sorted_ref, zero_ref, sem,
                     zero_sem, *, n_experts, tile):
    i = pl.program_id(0)
    last = pl.num_programs(0) - 1

    def zero_copy(e):
        return pltpu.make_async_copy(
            zero_ref, xs_hbm.at[pl.ds(pl.multiple_of(tail_ref[e], EXP_TILE), EXP_TILE)], zero_sem)

    @pl.when(i == 0)
    def _():
        zero_ref[...] = jnp.zeros(zero_ref.shape, F32)
        for e in range(n_experts):
            @pl.when(has_ref[e] == 1)
            def _(e=e):
                zero_copy(e).start()
        for e in range(n_experts):
            @pl.when(has_ref[e] == 1)
            def _(e=e):
                zero_copy(e).wait()

    def copy(step, src, dst, sz):
        slot = step & 1
        return pltpu.make_async_copy(sorted_ref.at[slot, pl.ds(src, sz)], xs_hbm.at[pl.ds(dst, sz)], sem.at[slot])

    onehot = _one_hot_rows(pk_ref[...], sorted_ref.shape[1])
    sorted_ref[i & 1] = lax.dot_general(onehot, h_ref[...], _TN, preferred_element_type=F32)
    _for_each_chunk_piece(cnt_ref, base_ref, i, n_experts, tile, lambda s, d, sz: copy(i, s, d, sz).start())

    @pl.when(i > 0)
    def _():
        _for_each_chunk_piece(cnt_ref, base_ref, i - 1, n_experts, tile,
                              lambda s, d, sz: copy(i - 1, s, d, sz).wait())

    @pl.when(i == last)
    def _():
        _for_each_chunk_piece(cnt_ref, base_ref, i, n_experts, tile, lambda s, d, sz: copy(i, s, d, sz).wait())


def _sorted_rows(tile, n_experts):
    rows = tile * TOP_K + n_experts * (SUBLANES - 1)
    return -(-rows // LANES) * LANES


def _dispatch(h2, pk4, cnt_flat, base_flat, tail_rows, has_tile, *, r_max, n_experts):
    n, d = h2.shape
    tile = min(TOK_TILE, n)
    grid_spec = pltpu.PrefetchScalarGridSpec(
        num_scalar_prefetch=4,
        grid=(n // tile,),
        in_specs=[pl.BlockSpec((tile, d), lambda i, *_: (i, 0)),
                  pl.BlockSpec((tile, LANES), lambda i, *_: (i, 0))],
        out_specs=pl.BlockSpec(memory_space=pl.ANY),
        scratch_shapes=[pltpu.VMEM((2, _sorted_rows(tile, n_experts), d), F32),
                        pltpu.VMEM((EXP_TILE, d), F32), pltpu.SemaphoreType.DMA((2,)), pltpu.SemaphoreType.DMA],
    )
    return pl.pallas_call(
        functools.partial(_dispatch_kernel, n_experts=n_experts, tile=tile),
        out_shape=jax.ShapeDtypeStruct((r_max, d), F32),
        grid_spec=grid_spec,
        compiler_params=_cparams(("arbitrary",)),
        name="dispatch_rows",
    )(cnt_flat, base_flat, tail_rows, has_tile, h2, pk4)


def _expert_kernel(st_e, st_j, st_n, g_t0, g_gs, g_valid,
                   xs_hbm, wg_ref, wu_ref, wd_ref, bg_ref, bu_ref, bd_ref, ys_hbm,
                   xbuf, actbuf, wgb, wub, wdb, stage_in, stage_out, pending, sem_in, sem_out,
                   *, n_j, n_n, tm, tf):
    s = pl.program_id(0)
    steps = n_j + n_n
    q = s // steps
    ph = s - q * steps
    valid = g_valid[q] == 1
    gs = g_gs[q]
    row0 = g_t0[q] * tm

    @pl.when(s == 0)
    def _():
        pending[0] = 0
        pending[1] = 0

    @pl.when(jnp.logical_and(valid, ph == 0))
    def _():
        def in_copy(i):
            slot = i & 1
            return pltpu.make_async_copy(xs_hbm.at[pl.ds(pl.multiple_of(row0 + i * tm, tm), tm)],
                                         stage_in.at[slot], sem_in.at[slot])

        @pl.when(gs >= 1)
        def _():
            in_copy(0).start()

        def load(i, carry):
            @pl.when(i + 1 < gs)
            def _():
                in_copy(i + 1).start()

            in_copy(i).wait()
            xbuf[pl.ds(pl.multiple_of(i * tm, tm), tm), :] = stage_in[i & 1].astype(BF16)
            return carry

        lax.fori_loop(0, gs, load, 0)

    @pl.when(jnp.logical_and(valid, ph < n_j))
    def _():
        wgb[...] = wg_ref[0].astype(BF16)
        wub[...] = wu_ref[0].astype(BF16)

        def tile(i, carry):
            rows = pl.ds(pl.multiple_of(i * tm, tm), tm)
            x = xbuf[rows, :]
            gate = jnp.dot(x, wgb[...], preferred_element_type=F32) + bg_ref[0]
            up = jnp.dot(x, wub[...], preferred_element_type=F32) + bu_ref[0]
            gate = jnp.minimum(gate, SWIGLU_LIMIT)
            up = jnp.clip(up, -SWIGLU_LIMIT, SWIGLU_LIMIT)
            act = (up + 1.0) * (gate * jax.nn.sigmoid(SWIGLU_ALPHA * gate))
            actbuf[ph, rows, :] = act.astype(BF16)
            return carry

        lax.fori_loop(0, gs, tile, 0)

    @pl.when(jnp.logical_and(valid, ph >= n_j))
    def _():
        wdb[...] = wd_ref[0].astype(BF16)
        col0 = pl.multiple_of((ph - n_j) * tf, tf)

        def out_copy(slot, i):
            return pltpu.make_async_copy(
                stage_out.at[slot],
                ys_hbm.at[pl.ds(pl.multiple_of(row0 + i * tm, tm), tm), pl.ds(col0, tf)],
                sem_out.at[slot])

        def wait_slot(slot):
            @pl.when(pending[slot] == 1)
            def _():
                out_copy(slot, 0).wait()
                pending[slot] = 0

        def tile(i, carry):
            rows = pl.ds(pl.multiple_of(i * tm, tm), tm)
            slot = i & 1
            wait_slot(slot)
            y = bd_ref[0] + jnp.dot(actbuf[0, rows, :], wdb[0:tf, :], preferred_element_type=F32)
            for j in range(1, n_j):
                y = y + jnp.dot(actbuf[j, rows, :], wdb[j * tf:(j + 1) * tf, :], preferred_element_type=F32)
            stage_out[slot] = y
            out_copy(slot, i).start()
            pending[slot] = 1
            return carry

        lax.fori_loop(0, gs, tile, 0)

        @pl.when(s == pl.num_programs(0) - 1)
        def _():
            wait_slot(0)
            wait_slot(1)


def _expert_ffn(tables, xs, w_gate, b_gate, w_up, b_up, w_down, b_down, *, n_steps):
    st_e, st_j, st_n, g_t0, g_gs, g_valid = tables
    r_max, d = xs.shape
    n_exp, _, d_ff = w_gate.shape
    tf = min(FF_TILE, d_ff, d)
    n_j = d_ff // tf
    n_n = d // tf
    tm = EXP_TILE
    rows_g = EXP_GROUP * tm
    grid_spec = pltpu.PrefetchScalarGridSpec(
        num_scalar_prefetch=6,
        grid=(n_steps,),
        in_specs=[pl.BlockSpec(memory_space=pl.ANY),
                  pl.BlockSpec((1, d, tf), lambda s, e, j, n, *_: (e[s], 0, j[s])),
                  pl.BlockSpec((1, d, tf), lambda s, e, j, n, *_: (e[s], 0, j[s])),
                  pl.BlockSpec((1, d_ff, tf), lambda s, e, j, n, *_: (e[s], 0, n[s])),
                  pl.BlockSpec((1, 1, tf), lambda s, e, j, n, *_: (e[s], 0, j[s])),
                  pl.BlockSpec((1, 1, tf), lambda s, e, j, n, *_: (e[s], 0, j[s])),
                  pl.BlockSpec((1, 1, tf), lambda s, e, j, n, *_: (e[s], 0, n[s]))],
        out_specs=pl.BlockSpec(memory_space=pl.ANY),
        scratch_shapes=[pltpu.VMEM((rows_g, d), BF16),
                        pltpu.VMEM((n_j, rows_g, tf), BF16),
                        pltpu.VMEM((d, tf), BF16), pltpu.VMEM((d, tf), BF16), pltpu.VMEM((d_ff, tf), BF16),
                        pltpu.VMEM((2, tm, d), F32), pltpu.VMEM((2, tm, tf), F32), pltpu.SMEM((2,), I32),
                        pltpu.SemaphoreType.DMA((2,)), pltpu.SemaphoreType.DMA((2,))],
    )
    return pl.pallas_call(
        functools.partial(_expert_kernel, n_j=n_j, n_n=n_n, tm=tm, tf=tf),
        out_shape=jax.ShapeDtypeStruct((r_max, d), F32),
        grid_spec=grid_spec,
        compiler_params=_cparams(("arbitrary",)),
        name="expert_ffn",
    )(st_e, st_j, st_n, g_t0, g_gs, g_valid, xs, w_gate, w_up, w_down,
      b_gate.reshape(n_exp, 1, d_ff), b_up.reshape(n_exp, 1, d_ff), b_down.reshape(n_exp, 1, d))


def _combine_kernel(cnt_ref, base_ref, ys_hbm, x2_ref, pk_ref, gt_ref, gate_p_ref, gate_s_ref, g_ref,
                    yp_ref, ysm_ref, rows_ref, sem, *, n_experts, tile, n_prompt_tiles):
    i = pl.program_id(0)
    last = pl.num_programs(0) - 1

    def copy(step, dst, src, sz):
        slot = step & 1
        return pltpu.make_async_copy(ys_hbm.at[pl.ds(src, sz)], rows_ref.at[slot, pl.ds(dst, sz)], sem.at[slot])

    def fetch(step):
        _for_each_chunk_piece(cnt_ref, base_ref, step, n_experts, tile,
                              lambda s, d, sz: copy(step, s, d, sz).start())

    @pl.when(i == 0)
    def _():
        rows_ref[...] = jnp.zeros(rows_ref.shape, F32)
        fetch(i)

    @pl.when(i < last)
    def _():
        fetch(i + 1)

    _for_each_chunk_piece(cnt_ref, base_ref, i, n_experts, tile, lambda s, d, sz: copy(i, s, d, sz).wait())
    weights = _one_hot_rows(pk_ref[...], rows_ref.shape[1], gt_ref[...])
    y = jnp.dot(weights, rows_ref[i & 1].astype(BF16), preferred_element_type=F32)
    is_prompt = i < n_prompt_tiles
    gate = jnp.where(is_prompt, gate_p_ref[0], gate_s_ref[...])
    x3 = x2_ref[...] + gate * y
    ms = jnp.mean(x3 * x3, axis=-1, keepdims=True)
    out = x3 * lax.rsqrt(ms + NORM_EPS) * g_ref[...]

    @pl.when(is_prompt)
    def _():
        yp_ref[...] = out

    @pl.when(jnp.logical_not(is_prompt))
    def _():
        ysm_ref[...] = out


def _combine(ys, cnt_flat, base_flat, x2, pk4, gt, gate_p, gate_s, final_g, *, n_prompt, tiles_per_seq, n_experts):
    n, d = x2.shape
    tile = min(TOK_TILE, n_prompt)
    n_s = n - n_prompt
    assert n_s == tile and n_prompt % tile == 0
    npt = n_prompt // tile
    grid_spec = pltpu.PrefetchScalarGridSpec(
        num_scalar_prefetch=2,
        grid=(n // tile,),
        in_specs=[pl.BlockSpec(memory_space=pl.ANY),
                  pl.BlockSpec((tile, d), lambda i, *_: (i, 0)),
                  pl.BlockSpec((tile, LANES), lambda i, *_: (i, 0)),
                  pl.BlockSpec((tile, LANES), lambda i, *_: (i, 0)),
                  pl.BlockSpec((1, 1, d), lambda i, *_: (jnp.minimum(i, npt - 1) // tiles_per_seq, 0, 0)),
                  pl.BlockSpec((tile, d), lambda i, *_: (0, 0)),
                  pl.BlockSpec((1, d), lambda i, *_: (0, 0))],
        out_specs=(pl.BlockSpec((tile, d), lambda i, *_: (jnp.minimum(i, npt - 1), 0)),
                   pl.BlockSpec((tile, d), lambda i, *_: (0, 0))),
        scratch_shapes=[pltpu.VMEM((2, _sorted_rows(tile, n_experts), d), F32), pltpu.SemaphoreType.DMA((2,))],
    )
    return pl.pallas_call(
        functools.partial(_combine_kernel, n_experts=n_experts, tile=tile, n_prompt_tiles=npt),
        out_shape=(jax.ShapeDtypeStruct((n_prompt, d), F32), jax.ShapeDtypeStruct((n_s, d), F32)),
        grid_spec=grid_spec,
        compiler_params=_cparams(("arbitrary",)),
        name="combine_norm",
    )(cnt_flat, base_flat, ys, x2, pk4, gt, gate_p, gate_s, final_g)


def _expert_tables(tile_cnt, *, tile, n_j, n_n):
    n_tiles, n_experts = tile_cnt.shape
    tm, grp = EXP_TILE, EXP_GROUP
    cnt8 = (tile_cnt + (SUBLANES - 1)) // SUBLANES * SUBLANES
    rows_e = jnp.sum(cnt8, axis=0)
    max_rows = n_tiles * tile * TOP_K + n_experts * n_tiles * (SUBLANES - 1)
    t_max = -(-max_rows // tm) + n_experts
    ng_max = n_experts + t_max // grp
    ntile = (rows_e + tm - 1) // tm
    tile_start = jnp.cumsum(ntile) - ntile
    pstart = tile_start * tm
    base = pstart[None, :] + jnp.cumsum(cnt8, axis=0) - cnt8
    ng = (ntile + grp - 1) // grp
    cg = jnp.cumsum(ng)
    n_groups = cg[-1]
    q = jnp.arange(ng_max, dtype=I32)
    eq = jnp.minimum(jnp.sum((q[:, None] >= cg[None, :]).astype(I32), axis=1), n_experts - 1)
    onehot = eq[:, None] == jnp.arange(n_experts, dtype=I32)[None, :]

    def pick(v):
        return jnp.sum(jnp.where(onehot, v[None, :], 0), axis=1)

    lg = q - (pick(cg) - pick(ng))
    t0 = pick(tile_start) + lg * grp
    gs = jnp.clip(pick(ntile) - lg * grp, 0, grp)
    valid = q < n_groups
    is_last = q == jnp.maximum(n_groups - 1, 0)
    eq = jnp.where(valid, eq, jnp.sum(jnp.where(is_last, eq, 0)))
    t0 = jnp.where(valid, t0, jnp.sum(jnp.where(is_last, t0, 0)))
    gs = jnp.where(valid, gs, 0)
    steps = n_j + n_n
    ph = jnp.tile(jnp.arange(steps, dtype=I32), ng_max)
    vs = jnp.repeat(valid, steps)
    st_e = jnp.repeat(eq, steps)
    st_j = jnp.where(vs, jnp.minimum(ph, n_j - 1), n_j - 1)
    st_n = jnp.where(vs, jnp.maximum(ph - n_j, 0), n_n - 1)
    tail = jnp.maximum(pstart + (ntile - 1) * tm, 0)
    has = (ntile > 0).astype(I32)
    tabs = tuple(a.astype(I32) for a in (st_e, st_j, st_n, t0, gs, valid))
    return (tabs, tile_cnt.reshape(-1).astype(I32), base.reshape(-1).astype(I32), tail.astype(I32), has,
            t_max * tm, (n_groups * steps).astype(I32))


def _rope_tables(pos):
    inv = 1.0 / (ROPE_THETA ** (jnp.arange(0, HEAD_DIM, 2, dtype=F32) / HEAD_DIM))
    ang = pos.astype(F32)[:, None] * inv[None, :]
    reps = LANES // (HEAD_DIM // 2)
    return jnp.tile(jnp.cos(ang), (1, reps)), jnp.tile(jnp.sin(ang), (1, reps))


def kernel(x_prompt, x_sample, cache_k, cache_v, state_pool, page_table, c_prompt, c_sample, w_ada, b_ada, norm1_g, norm2_g, w_in, lam_q1, lam_k1, lam_q2, lam_k2, subln_g, w_pool, pool_scale, w_out, w_router, b_router, w_gate, b_gate, w_up, b_up, w_down, b_down, final_g):
    B, S, D = x_prompt.shape
    DB, T, _ = x_sample.shape
    depth = w_ada.shape[0]
    page = cache_k.shape[2]
    past = page_table.shape[1] * page
    d_att = D // 2
    n_heads = d_att // HEAD_W
    d_pool = w_in.shape[2] - 3 * d_att
    n_experts = w_router.shape[2]
    state_len = state_pool.shape[2]
    n_p, n_s = B * S, DB * T
    n_all = n_p + n_s
    tm = min(TOK_TILE, n_p)
    tiles_per_seq = S // tm
    assert depth == 1, "single-layer step"
    assert n_s == tm and S % tm == 0 and T % 8 == 0 and state_len < POOL_HALO <= tm

    cos_p, sin_p = _rope_tables(jnp.arange(S))
    cos_s, sin_s = _rope_tables(jnp.tile(past + jnp.arange(T), DB))

    l = 0
    lam_init = 0.8 - 0.6 * math.exp(-0.3 * l)
    lamp = jnp.zeros((8, LANES), F32)
    for r, vec in enumerate((lam_q1[l], lam_k1[l], lam_q2[l], lam_k2[l])):
        lamp = lamp.at[r, :HEAD_DIM].set(vec.astype(F32))
    subg = subln_g[l].reshape(1, HEAD_W)

    rows_c = -(-(B + DB) // 8) * 8
    c_all = jnp.zeros((rows_c, D), F32).at[:B].set(c_prompt).at[B:B + DB].set(c_sample)
    m_all = _adaln(c_all, w_ada[l], b_ada[l])
    mods_p = [m_all[:B, k * D:(k + 1) * D].reshape(B, 1, D) for k in range(N_ADA)]
    mods_s = [jnp.repeat(m_all[B:B + DB, k * D:(k + 1) * D], T, axis=0).reshape(1, n_s, D) for k in range(N_ADA)]

    w_in_b = _cast_bf16(w_in[l], 256)
    w_out_b = _cast_bf16(w_out[l], 256)
    ng, gw = w_pool.shape[1], w_pool.shape[2]
    w_pool_b = _cast_bf16(w_pool[l].reshape(ng * gw, gw), ng * gw).reshape(ng, gw, gw)
    g1 = norm1_g[l].reshape(1, D)
    g2 = norm2_g[l].reshape(1, D)
    pscale = pool_scale[l].reshape(1, d_pool)
    w_router_p = jnp.zeros((D, LANES), F32).at[:, :n_experts].set(w_router[l])
    b_router_p = jnp.zeros((1, LANES), F32).at[0, :n_experts].set(b_router[l].astype(F32))

    xp = x_prompt.reshape(n_p, D)
    q_p, k_p, v_p, u_p, kb_p, vb_p = _project(xp, mods_p[0], mods_p[1], g1, w_in_b, cos_p, sin_p,
                                              tiles_per_seq=tiles_per_seq, pos_tiles=tiles_per_seq)
    a_p = _attn_prompt(q_p, kb_p, vb_p, lamp, subg.reshape(HEAD_W, 1), batch=B, seq=S, lam_init=lam_init)
    pm_p = _pool_mix(u_p.reshape(B, S, d_pool), w_pool_b, pscale, tile=tm, cur_block0=0, pos0=0,
                     zero_first_halo=True).reshape(n_p, d_pool)

    xs_tok = x_sample.reshape(n_s, D)
    q_s, k_s, v_s, u_s, _, _ = _project(xs_tok, mods_s[0], mods_s[1], g1, w_in_b, cos_s, sin_s,
                                        tiles_per_seq=1, pos_tiles=1)
    a_s = _attn_sample(page_table, q_s, k_s, v_s, cache_k[l], cache_v[l], lamp, subg, t_new=T, lam_init=lam_init)
    u_ext = jnp.concatenate([jnp.zeros((DB, POOL_HALO - state_len, d_pool), F32),
                             state_pool[l].astype(F32), u_s.reshape(DB, T, d_pool)], axis=1)
    pm_s = _pool_mix(u_ext, w_pool_b, pscale, tile=T, cur_block0=POOL_HALO // T, pos0=past,
                     zero_first_halo=False).reshape(n_s, d_pool)

    outs = _mix_out(xp, a_p, pm_p, mods_p[2], mods_p[3], mods_p[4], g2, w_out_b, w_router_p, b_router_p,
                    n_total=n_all, tile0=0, tiles_per_seq=tiles_per_seq, n_experts=n_experts)
    x2, h2, pk4, gt4, tcnt = _mix_out(xs_tok, a_s, pm_s, mods_s[2], mods_s[3], mods_s[4], g2, w_out_b, w_router_p,
                                      b_router_p, n_total=n_all, tile0=n_p // tm, tiles_per_seq=1,
                                      n_experts=n_experts, prev=outs)

    d_ff = w_gate.shape[3]
    tf = min(FF_TILE, d_ff, D)
    tile_cnt = tcnt[:, 0, :n_experts].astype(I32)
    tabs, cnt_flat, base_flat, tail, has, r_max, n_steps = _expert_tables(
        tile_cnt, tile=tm, n_j=d_ff // tf, n_n=D // tf)
    xs_rows = _dispatch(h2, pk4, cnt_flat, base_flat, tail, has, r_max=r_max, n_experts=n_experts)
    ys_rows = _expert_ffn(tabs, xs_rows, w_gate[l], b_gate[l], w_up[l], b_up[l], w_down[l], b_down[l],
                          n_steps=n_steps)
    y_p, y_s = _combine(ys_rows, cnt_flat, base_flat, x2, pk4, gt4, mods_p[5], mods_s[5].reshape(n_s, D),
                        final_g.reshape(1, D), n_prompt=n_p, tiles_per_seq=tiles_per_seq, n_experts=n_experts)

    n_pages_p = S // page
    k_prompt = k_p.reshape(1, B, n_pages_p, page, n_heads, HEAD_W)
    v_prompt = v_p.reshape(1, B, n_pages_p, page, n_heads, HEAD_W)
    pool_prompt = u_p.reshape(B, S, d_pool)[:, S - state_len:][None]
    k_sample = k_s.reshape(1, DB, T, n_heads, HEAD_W)
    v_sample = v_s.reshape(1, DB, T, n_heads, HEAD_W)
    pool_sample = u_ext[:, -state_len:][None]
    return (y_p.reshape(B, S, D), y_s.reshape(DB, T, D), k_prompt, v_prompt, pool_prompt,
            k_sample, v_sample, pool_sample)
```

```python
import functools
import math

import jax
import jax.numpy as jnp
from jax import lax
from jax.experimental import pallas as pl
from jax.experimental.pallas import tpu as pltpu

F32 = jnp.float32
BF16 = jnp.bfloat16
I32 = jnp.int32

HEAD_DIM = 64
HEAD_W = 2 * HEAD_DIM
POOL_WINDOWS = (2, 4, 8, 16)
POOL_HALO = 16
TOP_K = 4
SWIGLU_LIMIT = 7.0
SWIGLU_ALPHA = 1.702
ROPE_THETA = 10000.0
NORM_EPS = 1e-5
N_ADA = 6
LANES = 128
SUBLANES = 8
V7X_VMEM_LIMIT = 58 * 1024 * 1024

TOK_TILE = 256
EXP_TILE = 256
EXP_GROUP = 6
FF_TILE = 512
Q_SCALE = (HEAD_DIM ** -0.5) * math.log2(math.e)


def _cparams(sem, vmem=V7X_VMEM_LIMIT):
    return pltpu.CompilerParams(dimension_semantics=sem, vmem_limit_bytes=vmem)


def _cast_kernel(x_ref, o_ref):
    o_ref[...] = x_ref[...].astype(o_ref.dtype)


def _cast_bf16(w, rows):
    r, c = w.shape
    return pl.pallas_call(
        _cast_kernel,
        out_shape=jax.ShapeDtypeStruct((r, c), BF16),
        grid=(r // rows,),
        in_specs=[pl.BlockSpec((rows, c), lambda i: (i, 0))],
        out_specs=pl.BlockSpec((rows, c), lambda i: (i, 0)),
        compiler_params=_cparams(("arbitrary",)),
        name="cast_bf16",
    )(w)


def _ada_kernel(c_ref, w_ref, b_ref, o_ref):
    c = c_ref[...]
    s = (c * jax.nn.sigmoid(c)).astype(BF16)
    o_ref[...] = jnp.dot(s, w_ref[...].astype(BF16), preferred_element_type=F32) + b_ref[...]


def _adaln(c_all, w_ada, b_ada):
    rows, d = c_all.shape
    n = w_ada.shape[1]
    tn = min(1024, n)
    return pl.pallas_call(
        _ada_kernel,
        out_shape=jax.ShapeDtypeStruct((rows, n), F32),
        grid=(n // tn,),
        in_specs=[pl.BlockSpec((rows, d), lambda j: (0, 0)),
                  pl.BlockSpec((d, tn), lambda j: (0, j)),
                  pl.BlockSpec((1, tn), lambda j: (0, j))],
        out_specs=pl.BlockSpec((rows, tn), lambda j: (0, j)),
        compiler_params=_cparams(("arbitrary",)),
        name="adaln",
    )(c_all, w_ada, b_ada.reshape(1, n))


def _modulated_norm(x, g, shift, scale):
    ms = jnp.mean(x * x, axis=-1, keepdims=True)
    return (x * lax.rsqrt(ms + NORM_EPS) * g) * (1.0 + scale) + shift


def _proj_kernel(x_ref, shift_ref, scale_ref, g_ref, w_ref, cos_ref, sin_ref,
                 q_ref, k_ref, v_ref, u_ref, kb_ref, vb_ref, *, d_att):
    h = _modulated_norm(x_ref[...], g_ref[...], shift_ref[0], scale_ref[0]).astype(BF16)
    cos = cos_ref[...]
    sin = sin_ref[...]
    lane = lax.broadcasted_iota(I32, cos.shape, 1)
    first_half = (lane & (HEAD_DIM - 1)) < (HEAD_DIM // 2)

    def rope(z):
        rot = jnp.where(first_half, -pltpu.roll(z, LANES - HEAD_DIM // 2, 1), pltpu.roll(z, HEAD_DIM // 2, 1))
        return z * cos + rot * sin

    zq = jnp.dot(h, w_ref[:, 0:d_att], preferred_element_type=F32)
    zk = jnp.dot(h, w_ref[:, d_att:2 * d_att], preferred_element_type=F32)
    for hh in range(d_att // HEAD_W):
        sl = slice(hh * HEAD_W, (hh + 1) * HEAD_W)
        q_ref[:, sl] = rope(zq[:, sl]) * Q_SCALE
        kr = rope(zk[:, sl])
        k_ref[:, sl] = kr
        kb_ref[:, sl] = kr.astype(BF16)
    zv = jnp.dot(h, w_ref[:, 2 * d_att:3 * d_att], preferred_element_type=F32)
    v_ref[...] = zv
    vb_ref[...] = zv.astype(BF16)
    u_ref[...] = jnp.dot(h, w_ref[:, 3 * d_att:], preferred_element_type=F32)


def _project(x, shift, scale, g, w_in_b, cos, sin, *, tiles_per_seq, pos_tiles):
    n, d = x.shape
    tm = min(TOK_TILE, n)
    d_in = w_in_b.shape[1]
    d_att = (d // 2)
    d_pool = d_in - 3 * d_att
    mod_rows = shift.shape[1]
    mod_spec = pl.BlockSpec((1, mod_rows, d), lambda i: (i // tiles_per_seq, 0, 0))
    tok = lambda c: pl.BlockSpec((tm, c), lambda i: (i, 0))
    return pl.pallas_call(
        functools.partial(_proj_kernel, d_att=d_att),
        out_shape=(jax.ShapeDtypeStruct((n, d_att), F32), jax.ShapeDtypeStruct((n, d_att), F32),
                   jax.ShapeDtypeStruct((n, d_att), F32), jax.ShapeDtypeStruct((n, d_pool), F32),
                   jax.ShapeDtypeStruct((n, d_att), BF16), jax.ShapeDtypeStruct((n, d_att), BF16)),
        grid=(n // tm,),
        in_specs=[tok(d), mod_spec, mod_spec,
                  pl.BlockSpec((1, d), lambda i: (0, 0)),
                  pl.BlockSpec((d, d_in), lambda i: (0, 0)),
                  pl.BlockSpec((tm, LANES), lambda i: (i % pos_tiles, 0)),
                  pl.BlockSpec((tm, LANES), lambda i: (i % pos_tiles, 0))],
        out_specs=(tok(d_att), tok(d_att), tok(d_att), tok(d_pool), tok(d_att), tok(d_att)),
        compiler_params=_cparams(("arbitrary",)),
        name="in_proj",
    )(x, shift, scale, g, w_in_b, cos, sin)


def _lambda_value(lam_ref, lam_init):
    lp = lam_ref[...]
    a = jnp.sum(lp[0:1] * lp[1:2], axis=1, keepdims=True)
    b = jnp.sum(lp[2:3] * lp[3:4], axis=1, keepdims=True)
    return jnp.exp(a) - jnp.exp(b) + lam_init


def _stack_maps(q):
    lane = lax.broadcasted_iota(I32, q.shape, 1)
    q1 = jnp.where(lane < HEAD_DIM, q, 0.0)
    q2 = jnp.where(lane >= HEAD_DIM, q, 0.0)
    return jnp.concatenate([q1, q2], axis=0).astype(BF16)


_NT = (((1,), (1,)), ((), ()))


_TN = (((0,), (0,)), ((), ()))


def _attn_prompt_kernel(lam_ref, g_ref, q_ref, k_ref, k2_ref, v_ref, o_ref, m_ref, l_ref, acc_ref,
                        *, tq, tk, lam_init):
    qi = pl.program_id(2)
    heads = q_ref.shape[1] // HEAD_W
    qqs = [_stack_maps(q_ref[:, hh * HEAD_W:(hh + 1) * HEAD_W]) for hh in range(heads)]
    m_ref[...] = jnp.full(m_ref.shape, -jnp.inf, F32)
    l_ref[...] = jnp.zeros(l_ref.shape, F32)
    acc_ref[...] = jnp.zeros(acc_ref.shape, F32)

    def block(j, masked):
        rows = pl.ds(pl.multiple_of(j * tk, tk), tk)
        for hh in range(heads):
            cols = slice(hh * HEAD_W, (hh + 1) * HEAD_W)

            def scores(kref):
                st = lax.dot_general(kref[rows, cols], qqs[hh], _NT, preferred_element_type=F32)
                if masked:
                    kpos = j * tk + lax.broadcasted_iota(I32, st.shape, 0)
                    c = lax.broadcasted_iota(I32, st.shape, 1)
                    qpos = qi * tq + jnp.where(c >= tq, c - tq, c)
                    st = jnp.where(kpos <= qpos, st, -jnp.inf)
                return st

            m_prev = m_ref[hh]
            m_new = jnp.maximum(m_prev, jnp.max(scores(k_ref), axis=0, keepdims=True))
            p = jnp.exp2(scores(k2_ref) - m_new)
            alpha = jnp.exp2(m_prev - m_new)
            l_ref[hh] = alpha * l_ref[hh] + jnp.sum(p, axis=0, keepdims=True)
            pv = lax.dot_general(v_ref[rows, cols], p.astype(BF16), _TN, preferred_element_type=F32)
            acc_ref[hh] = acc_ref[hh] * alpha + pv
            m_ref[hh] = m_new

    n_full = (qi * tq) // tk

    def body(j, carry):
        block(j, False)
        return carry

    lax.fori_loop(0, n_full, body, 0)
    block(n_full, True)

    lam = _lambda_value(lam_ref, lam_init)
    for hh in range(heads):
        l = l_ref[hh]
        acc = acc_ref[hh]
        o = acc[:, :tq] / l[:, :tq] - lam * (acc[:, tq:] / l[:, tq:])
        ms = jnp.mean(o * o, axis=0, keepdims=True)
        a = o * lax.rsqrt(ms + NORM_EPS) * g_ref[...] * (1.0 - lam_init)
        o_ref[:, hh * HEAD_W:(hh + 1) * HEAD_W] = a.T.astype(o_ref.dtype)


def _attn_prompt(q, kb, vb, lamp, g_col, *, batch, seq, lam_init):
    n, d_att = q.shape
    nh = d_att // HEAD_W
    hp = 4 if nh % 4 == 0 else 1
    tq = min(256, seq)
    tk = min(512, seq)
    nq = seq // tq
    return pl.pallas_call(
        functools.partial(_attn_prompt_kernel, tq=tq, tk=tk, lam_init=lam_init),
        out_shape=jax.ShapeDtypeStruct((n, d_att), BF16),
        grid=(batch, nh // hp, nq),
        in_specs=[pl.BlockSpec((8, LANES), lambda b, h, i: (0, 0)),
                  pl.BlockSpec((HEAD_W, 1), lambda b, h, i: (0, 0)),
                  pl.BlockSpec((tq, hp * HEAD_W), lambda b, h, i: (b * nq + i, h)),
                  pl.BlockSpec((seq, hp * HEAD_W), lambda b, h, i: (b, h)),
                  pl.BlockSpec((seq, hp * HEAD_W), lambda b, h, i: (b, h)),
                  pl.BlockSpec((seq, hp * HEAD_W), lambda b, h, i: (b, h))],
        out_specs=pl.BlockSpec((tq, hp * HEAD_W), lambda b, h, i: (b * nq + i, h)),
        scratch_shapes=[pltpu.VMEM((hp, 1, 2 * tq), F32), pltpu.VMEM((hp, 1, 2 * tq), F32),
                        pltpu.VMEM((hp, HEAD_W, 2 * tq), F32)],
        compiler_params=_cparams(("arbitrary", "arbitrary", "arbitrary")),
        name="attn_prompt",
    )(lamp, g_col, q, kb, kb, vb)


def _attn_sample_kernel(pt_ref, lam_ref, g_ref, q_ref, kn_ref, vn_ref, *rest, n_heads, pages, t_new, lam_init):
    k_pages = rest[:pages]
    v_pages = rest[pages:2 * pages]
    o_ref = rest[2 * pages]
    m_ref, l_ref, acc_ref = rest[2 * pages + 1:]
    c = pl.program_id(1)
    rows_h = 2 * t_new
    assert n_heads == SUBLANES and n_heads * rows_h == LANES

    @pl.when(c == 0)
    def _():
        m_ref[...] = jnp.full(m_ref.shape, -jnp.inf, F32)
        l_ref[...] = jnp.zeros(l_ref.shape, F32)
        acc_ref[...] = jnp.zeros(acc_ref.shape, F32)

    q = q_ref[...]
    q_all = jnp.concatenate([_stack_maps(q[:, hh * HEAD_W:(hh + 1) * HEAD_W]) for hh in range(n_heads)], axis=0)
    sub = lax.broadcasted_iota(I32, (SUBLANES, LANES), 0)
    lane = lax.broadcasted_iota(I32, (SUBLANES, LANES), 1)
    own_head = (lane // rows_h) == sub

    def update(k_rows, v_rows, valid):
        n_pos = k_rows.shape[0] // n_heads
        r = lax.dot_general(k_rows.astype(BF16), q_all, _NT, preferred_element_type=F32)
        r = r.reshape(n_pos, n_heads, LANES)
        if valid is not None:
            r = jnp.where(valid, r, -jnp.inf)
        m_old = m_ref[...]
        m_new = jnp.maximum(m_old, jnp.max(r, axis=0))
        p = jnp.exp2(r - m_new[None])
        alpha = jnp.exp2(m_old - m_new)
        l_ref[...] = alpha * l_ref[...] + jnp.sum(p, axis=0)
        p_own = jnp.where(own_head[None], p, 0.0).reshape(n_pos * n_heads, LANES).astype(BF16)
        alpha_row = jnp.sum(jnp.where(own_head, alpha, 0.0), axis=0, keepdims=True)
        pv = lax.dot_general(v_rows.astype(BF16), p_own, _TN, preferred_element_type=F32)
        acc_ref[...] = acc_ref[...] * alpha_row + pv
        m_ref[...] = m_new

    for i in range(pages):
        update(k_pages[i][0], v_pages[i][0], None)

    @pl.when(c == pl.num_programs(1) - 1)
    def _():
        t_key = lax.broadcasted_iota(I32, (t_new, n_heads, LANES), 0)
        t_query = lax.broadcasted_iota(I32, (t_new, n_heads, LANES), 2) % t_new
        update(kn_ref[0], vn_ref[0], t_key <= t_query)
        l_row = jnp.sum(jnp.where(own_head, l_ref[...], 0.0), axis=0, keepdims=True)
        o_all = (acc_ref[...] / l_row).T
        lam = _lambda_value(lam_ref, lam_init)
        for hh in range(n_heads):
            rows = o_all[hh * rows_h:(hh + 1) * rows_h]
            o = rows[:t_new] - lam * rows[t_new:]
            ms = jnp.mean(o * o, axis=1, keepdims=True)
            o_ref[:, hh * HEAD_W:(hh + 1) * HEAD_W] = o * lax.rsqrt(ms + NORM_EPS) * g_ref[...] * (1.0 - lam_init)


def _attn_sample(page_table, q, k_new, v_new, cache_k, cache_v, lamp, g, *, t_new, lam_init):
    n, d_att = q.shape
    nh = d_att // HEAD_W
    db, n_pages = page_table.shape
    n_pool, page = cache_k.shape[0], cache_k.shape[1]
    pages = min(8, n_pages)
    ck = cache_k.reshape(n_pool, page * nh, HEAD_W)
    cv = cache_v.reshape(n_pool, page * nh, HEAD_W)

    def page_spec(i):
        return pl.BlockSpec((1, page * nh, HEAD_W), lambda b, c, pt: (pt[b, c * pages + i], 0, 0))

    tok = pl.BlockSpec((t_new, d_att), lambda b, c, pt: (b, 0))
    new_rows = pl.BlockSpec((1, t_new * nh, HEAD_W), lambda b, c, pt: (b, 0, 0))
    k_new = k_new.reshape(db, t_new, nh, HEAD_W).reshape(db, t_new * nh, HEAD_W)
    v_new = v_new.reshape(db, t_new, nh, HEAD_W).reshape(db, t_new * nh, HEAD_W)
    grid_spec = pltpu.PrefetchScalarGridSpec(
        num_scalar_prefetch=1,
        grid=(db, n_pages // pages),
        in_specs=[pl.BlockSpec((8, LANES), lambda b, c, pt: (0, 0)),
                  pl.BlockSpec((1, HEAD_W), lambda b, c, pt: (0, 0)),
                  tok, new_rows, new_rows]
                 + [page_spec(i) for i in range(pages)] + [page_spec(i) for i in range(pages)],
        out_specs=tok,
        scratch_shapes=[pltpu.VMEM((nh, LANES), F32), pltpu.VMEM((nh, LANES), F32),
                        pltpu.VMEM((HEAD_W, LANES), F32)],
    )
    return pl.pallas_call(
        functools.partial(_attn_sample_kernel, n_heads=nh, pages=pages, t_new=t_new, lam_init=lam_init),
        out_shape=jax.ShapeDtypeStruct((n, d_att), F32),
        grid_spec=grid_spec,
        compiler_params=_cparams(("arbitrary", "arbitrary")),
        name="attn_sample",
    )(page_table, lamp, g, q, k_new, v_new, *([ck] * pages), *([cv] * pages))


def _pool_kernel(halo_ref, cur_ref, w_ref, scale_ref, o_ref, ext_ref, *, pos0, tile_pos, zero_first_halo):
    i = pl.program_id(1)
    t = cur_ref.shape[1]
    halo = halo_ref[0]
    if zero_first_halo:
        halo = jnp.where(i == 0, 0.0, halo)
    cur = cur_ref[0]
    ext_ref[0:POOL_HALO, :] = halo
    ext_ref[POOL_HALO:POOL_HALO + t, :] = cur
    pos = pos0 + i * tile_pos + lax.broadcasted_iota(I32, (t, 1), 0)
    gw = cur.shape[1] // len(POOL_WINDOWS)
    for gi, w in enumerate(POOL_WINDOWS):
        cols = slice(gi * gw, (gi + 1) * gw)
        total = cur[:, cols]
        for j in range(1, w):
            total = total + ext_ref[POOL_HALO - j:POOL_HALO - j + t, cols]
        cnt = jnp.minimum(pos + 1, w).astype(F32)
        dlt = total / cnt - cur[:, cols]
        y = jnp.dot(dlt.astype(BF16), w_ref[gi], preferred_element_type=F32)
        o_ref[0, :, cols] = (y * scale_ref[:, cols]).astype(o_ref.dtype)


def _pool_mix(u3, w_pool_b, scale, *, tile, cur_block0, pos0, zero_first_halo):
    b, rows, c = u3.shape
    n_tiles = (rows - cur_block0 * tile) // tile
    ng, gw = w_pool_b.shape[0], w_pool_b.shape[1]
    assert (cur_block0 * tile) % POOL_HALO == 0 and (tile % POOL_HALO == 0 or n_tiles == 1)

    def halo_map(bi, i):
        return (bi, jnp.maximum(((cur_block0 + i) * tile) // POOL_HALO - 1, 0), 0)

    return pl.pallas_call(
        functools.partial(_pool_kernel, pos0=pos0, tile_pos=tile, zero_first_halo=zero_first_halo),
        out_shape=jax.ShapeDtypeStruct((b, n_tiles * tile, c), BF16),
        grid=(b, n_tiles),
        in_specs=[pl.BlockSpec((1, POOL_HALO, c), halo_map),
                  pl.BlockSpec((1, tile, c), lambda bi, i: (bi, cur_block0 + i, 0)),
                  pl.BlockSpec((ng, gw, gw), lambda bi, i: (0, 0, 0)),
                  pl.BlockSpec((1, c), lambda bi, i: (0, 0))],
        out_specs=pl.BlockSpec((1, tile, c), lambda bi, i: (bi, i, 0)),
        scratch_shapes=[pltpu.VMEM((POOL_HALO + tile, c), F32)],
        compiler_params=_cparams(("arbitrary", "arbitrary")),
        name="pool_mix",
    )(u3, u3, w_pool_b, scale)


def _split_bf16(x):
    hi = x.astype(BF16)
    lo = (x - hi.astype(F32)).astype(BF16)
    return hi, lo


def _round_up_f32(x, m):
    return jnp.floor((x + (m - 1.0)) * (1.0 / m)) * m


def _mix_out_kernel(*refs, n_experts, aliased):
    (x_ref, a_ref, pm_ref, gate_ref, shift_ref, scale_ref, g_ref, wo_ref, wr_ref, br_ref) = refs[:10]
    x2_ref, h2_ref, pk_ref, gt_ref, tc_ref = refs[10 + aliased:]
    d_att = a_ref.shape[1]
    mix = (jnp.dot(a_ref[...].astype(BF16), wo_ref[0:d_att, :], preferred_element_type=F32)
           + jnp.dot(pm_ref[...], wo_ref[d_att:, :], preferred_element_type=F32))
    x2 = x_ref[...] + gate_ref[0] * mix
    x2_ref[...] = x2
    h2 = _modulated_norm(x2, g_ref[...], shift_ref[0], scale_ref[0])
    h2_ref[...] = h2.astype(BF16)
    hh, hl = _split_bf16(h2)
    wh, wl = _split_bf16(wr_ref[...])
    logits = (jnp.dot(hh, wh, preferred_element_type=F32) + jnp.dot(hl, wh, preferred_element_type=F32)
              + jnp.dot(hh, wl, preferred_element_type=F32)) + br_ref[...]
    t = logits.shape[0]
    lane = lax.broadcasted_iota(I32, logits.shape, 1)
    lanef = lane.astype(F32)
    work = jnp.where(lane < n_experts, logits, -jnp.inf)
    vals, ids = [], []
    for _ in range(TOP_K):
        mx = jnp.max(work, axis=1, keepdims=True)
        ix = jnp.min(jnp.where(work == mx, lanef, float(LANES)), axis=1, keepdims=True)
        vals.append(mx)
        ids.append(ix)
        work = jnp.where(lanef == ix, -jnp.inf, work)
    es = [jnp.exp(v - vals[0]) for v in vals]
    den = es[0]
    for e in es[1:]:
        den = den + e
    sel = jnp.zeros(logits.shape, F32)
    for k in range(TOP_K):
        sel = jnp.where(lanef == ids[k], 1.0, sel)
    r = lax.broadcasted_iota(I32, (t, t), 0)
    c = lax.broadcasted_iota(I32, (t, t), 1)
    earlier = jnp.where(c < r, 1.0, 0.0).astype(BF16)
    local_rank = jnp.dot(earlier, sel.astype(BF16), preferred_element_type=F32)
    cnt = jnp.sum(sel, axis=0, keepdims=True)
    cnt8 = jnp.broadcast_to(_round_up_f32(cnt, float(SUBLANES)), (SUBLANES, LANES))
    er = lax.broadcasted_iota(I32, (LANES, LANES), 0)
    ec = lax.broadcasted_iota(I32, (LANES, LANES), 1)
    before = jnp.where(er < ec, 1.0, 0.0).astype(BF16)
    chunk_off = jnp.dot(cnt8.astype(BF16), before, preferred_element_type=F32)[0:1, :]
    pos = local_rank + chunk_off
    pk_out = jnp.zeros(logits.shape, F32)
    gt_out = jnp.zeros(logits.shape, F32)
    for k in range(TOP_K):
        pk = jnp.sum(jnp.where(lanef == ids[k], pos, 0.0), axis=1, keepdims=True)
        pk_out = jnp.where(lane == k, pk, pk_out)
        gt_out = jnp.where(lane == k, es[k] / den, gt_out)
    pk_ref[...] = pk_out
    gt_ref[...] = gt_out
    tc_ref[0] = jnp.broadcast_to(cnt, (SUBLANES, LANES))


def _mix_out(x, a, pm, gate, shift, scale, g, w_out_b, w_router_p, b_router_p, *, n_total, tile0,
             tiles_per_seq, n_experts, prev=None):
    n, d = x.shape
    tm = min(TOK_TILE, n)
    d_att = a.shape[1]
    mod_rows = gate.shape[1]
    mod_spec = pl.BlockSpec((1, mod_rows, d), lambda i: (i // tiles_per_seq, 0, 0))
    tok = lambda c: pl.BlockSpec((tm, c), lambda i: (i, 0))
    out_tok = lambda c: pl.BlockSpec((tm, c), lambda i: (tile0 + i, 0))
    out_shape = (jax.ShapeDtypeStruct((n_total, d), F32), jax.ShapeDtypeStruct((n_total, d), BF16),
                 jax.ShapeDtypeStruct((n_total, LANES), F32), jax.ShapeDtypeStruct((n_total, LANES), F32),
                 jax.ShapeDtypeStruct((n_total // tm, SUBLANES, LANES), F32))
    in_specs = [tok(d), tok(d_att), tok(pm.shape[1]), mod_spec, mod_spec, mod_spec,
                pl.BlockSpec((1, d), lambda i: (0, 0)),
                pl.BlockSpec(w_out_b.shape, lambda i: (0, 0)),
                pl.BlockSpec(w_router_p.shape, lambda i: (0, 0)),
                pl.BlockSpec((1, LANES), lambda i: (0, 0))]
    args = [x, a, pm, gate, shift, scale, g, w_out_b, w_router_p, b_router_p]
    aliases = {}
    n_alias = 0
    if prev is not None:
        n_alias = len(prev)
        in_specs += [pl.BlockSpec(memory_space=pl.ANY)] * n_alias
        aliases = {len(args) + k: k for k in range(n_alias)}
        args += list(prev)
    return pl.pallas_call(
        functools.partial(_mix_out_kernel, n_experts=n_experts, aliased=n_alias),
        out_shape=out_shape,
        grid=(n // tm,),
        in_specs=in_specs,
        out_specs=(out_tok(d), out_tok(d), out_tok(LANES), out_tok(LANES),
                   pl.BlockSpec((1, SUBLANES, LANES), lambda i: (tile0 + i, 0, 0))),
        input_output_aliases=aliases,
        compiler_params=_cparams(("arbitrary",)),
        name="mix_out_router",
    )(*args)


def _chunk_sizes(tile):
    sizes = []
    s = tile
    while s >= SUBLANES:
        sizes.append(s)
        s //= 2
    return sizes


def _for_each_chunk_piece(cnt_ref, base_ref, i, n_experts, tile, fn):
    def per_expert(e, off):
        l8 = (cnt_ref[i * n_experts + e] + (SUBLANES - 1)) & (-SUBLANES)
        dst0 = base_ref[i * n_experts + e]
        done = jnp.int32(0)
        for sz in _chunk_sizes(tile):
            take = l8 & sz

            @pl.when(take != 0)
            def _(done=done, sz=sz):
                fn(pl.multiple_of(off + done, SUBLANES), pl.multiple_of(dst0 + done, SUBLANES), sz)

            done = done + take
        return off + l8

    return lax.fori_loop(0, n_experts, per_expert, jnp.int32(0))


def _one_hot_rows(pk, rows, weights=None):
    t = pk.shape[0]
    col = lax.broadcasted_iota(I32, (t, rows), 1).astype(F32)
    out = jnp.zeros((t, rows), F32)
    for k in range(TOP_K):
        w = 1.0 if weights is None else weights[:, k:k + 1]
        out = jnp.where(col == pk[:, k:k + 1], w, out)
    return out.astype(BF16)


def _dispatch_kernel(cnt_ref, base_ref, tail_ref, has_ref, h_ref, pk_ref, xs_hbm, sorted_ref, zero_ref, sem,
                     zero_sem, *, n_experts, tile):
    i = pl.program_id(0)
    last = pl.num_programs(0) - 1

    def zero_copy(e):
        return pltpu.make_async_copy(
            zero_ref, xs_hbm.at[pl.ds(pl.multiple_of(tail_ref[e], EXP_TILE), EXP_TILE)], zero_sem)

    @pl.when(i == 0)
    def _():
        zero_ref[...] = jnp.zeros(zero_ref.shape, F32)
        for e in range(n_experts):
            @pl.when(has_ref[e] == 1)
            def _(e=e):
                zero_copy(e).start()
        for e in range(n_experts):
            @pl.when(has_ref[e] == 1)
            def _(e=e):
                zero_copy(e).wait()

    def copy(step, src, dst, sz):
        slot = step & 1
        return pltpu.make_async_copy(sorted_ref.at[slot, pl.ds(src, sz)], xs_hbm.at[pl.ds(dst, sz)], sem.at[slot])

    onehot = _one_hot_rows(pk_ref[...], sorted_ref.shape[1])
    sorted_ref[i & 1] = lax.dot_general(onehot, h_ref[...], _TN, preferred_element_type=F32)
    _for_each_chunk_piece(cnt_ref, base_ref, i, n_experts, tile, lambda s, d, sz: copy(i, s, d, sz).start())

    @pl.when(i > 0)
    def _():
        _for_each_chunk_piece(cnt_ref, base_ref, i - 1, n_experts, tile,
                              lambda s, d, sz: copy(i - 1, s, d, sz).wait())

    @pl.when(i == last)
    def _():
        _for_each_chunk_piece(cnt_ref, base_ref, i, n_experts, tile, lambda s, d, sz: copy(i, s, d, sz).wait())


def _sorted_rows(tile, n_experts):
    rows = tile * TOP_K + n_experts * (SUBLANES - 1)
    return -(-rows // LANES) * LANES


def _dispatch(h2, pk4, cnt_flat, base_flat, tail_rows, has_tile, *, r_max, n_experts):
    n, d = h2.shape
    tile = min(TOK_TILE, n)
    grid_spec = pltpu.PrefetchScalarGridSpec(
        num_scalar_prefetch=4,
        grid=(n // tile,),
        in_specs=[pl.BlockSpec((tile, d), lambda i, *_: (i, 0)),
                  pl.BlockSpec((tile, LANES), lambda i, *_: (i, 0))],
        out_specs=pl.BlockSpec(memory_space=pl.ANY),
        scratch_shapes=[pltpu.VMEM((2, _sorted_rows(tile, n_experts), d), F32),
                        pltpu.VMEM((EXP_TILE, d), F32), pltpu.SemaphoreType.DMA((2,)), pltpu.SemaphoreType.DMA],
    )
    return pl.pallas_call(
        functools.partial(_dispatch_kernel, n_experts=n_experts, tile=tile),
        out_shape=jax.ShapeDtypeStruct((r_max, d), F32),
        grid_spec=grid_spec,
        compiler_params=_cparams(("arbitrary",)),
        name="dispatch_rows",
    )(cnt_flat, base_flat, tail_rows, has_tile, h2, pk4)


def _expert_kernel(st_e, st_j, st_n, g_t0, g_gs, g_valid,
                   xs_hbm, wg_ref, wu_ref, wd_ref, bg_ref, bu_ref, bd_ref, ys_hbm,
                   xbuf, actbuf, wgb, wub, wdb, stage_in, stage_out, pending, sem_in, sem_out,
                   *, n_j, n_n, tm, tf):
    s = pl.program_id(0)
    steps = n_j + n_n
    q = s // steps
    ph = s - q * steps
    valid = g_valid[q] == 1
    gs = g_gs[q]
    row0 = g_t0[q] * tm

    @pl.when(s == 0)
    def _():
        pending[0] = 0
        pending[1] = 0

    @pl.when(jnp.logical_and(valid, ph == 0))
    def _():
        def in_copy(i):
            slot = i & 1
            return pltpu.make_async_copy(xs_hbm.at[pl.ds(pl.multiple_of(row0 + i * tm, tm), tm)],
                                         stage_in.at[slot], sem_in.at[slot])

        @pl.when(gs >= 1)
        def _():
            in_copy(0).start()

        def load(i, carry):
            @pl.when(i + 1 < gs)
            def _():
                in_copy(i + 1).start()

            in_copy(i).wait()
            xbuf[pl.ds(pl.multiple_of(i * tm, tm), tm), :] = stage_in[i & 1].astype(BF16)
            return carry

        lax.fori_loop(0, gs, load, 0)

    @pl.when(jnp.logical_and(valid, ph < n_j))
    def _():
        def tile(i, wg, wu):
            rows = pl.ds(pl.multiple_of(i * tm, tm), tm)
            x = xbuf[rows, :]
            gate = jnp.dot(x, wg, preferred_element_type=F32) + bg_ref[0]
            up = jnp.dot(x, wu, preferred_element_type=F32) + bu_ref[0]
            gate = jnp.minimum(gate, SWIGLU_LIMIT)
            up = jnp.clip(up, -SWIGLU_LIMIT, SWIGLU_LIMIT)
            act = (up + 1.0) * (gate * jax.nn.sigmoid(SWIGLU_ALPHA * gate))
            actbuf[ph, rows, :] = act.astype(BF16)

        wg_cast = wg_ref[0].astype(BF16)
        wu_cast = wu_ref[0].astype(BF16)
        wgb[...] = wg_cast
        wub[...] = wu_cast
        tile(0, wg_cast, wu_cast)

        def rest(i, carry):
            tile(i, wgb[...], wub[...])
            return carry

        lax.fori_loop(1, gs, rest, 0)

    @pl.when(jnp.logical_and(valid, ph >= n_j))
    def _():
        col0 =pl.multiple_of((ph - n_j) * tf, tf)

        def out_copy(slot, i):
            return pltpu.make_async_copy(
                stage_out.at[slot],
                ys_hbm.at[pl.ds(pl.multiple_of(row0 + i * tm, tm), tm), pl.ds(col0, tf)],
                sem_out.at[slot])

        def wait_slot(slot):
            @pl.when(pending[slot] == 1)
            def _():
                out_copy(slot, 0).wait()
                pending[slot] = 0

        def tile(i, wd):
            rows = pl.ds(pl.multiple_of(i * tm, tm), tm)
            slot = i & 1
            wait_slot(slot)
            y = bd_ref[0] + jnp.dot(actbuf[0, rows, :], wd(0), preferred_element_type=F32)
            for j in range(1, n_j):
                y = y + jnp.dot(actbuf[j, rows, :], wd(j), preferred_element_type=F32)
            stage_out[slot] = y
            out_copy(slot, i).start()
            pending[slot] = 1

        wd_cast = wd_ref[0].astype(BF16)
        wdb[...] = wd_cast
        tile(0, lambda j: wd_cast[j * tf:(j + 1) * tf, :])

        def rest(i, carry):
            tile(i, lambda j: wdb[j * tf:(j + 1) * tf, :])
            return carry

        lax.fori_loop(1, gs, rest, 0)

        @pl.when(s == pl.num_programs(0) - 1)
        def _():
            wait_slot(0)
            wait_slot(1)


def _expert_ffn(tables, xs, w_gate, b_gate, w_up, b_up, w_down, b_down, *, n_steps):
    st_e, st_j, st_n, g_t0, g_gs, g_valid = tables
    r_max, d = xs.shape
    n_exp, _, d_ff = w_gate.shape
    tf = min(FF_TILE, d_ff, d)
    n_j = d_ff // tf
    n_n = d // tf
    tm = EXP_TILE
    rows_g = EXP_GROUP * tm
    grid_spec = pltpu.PrefetchScalarGridSpec(
        num_scalar_prefetch=6,
        grid=(n_steps,),
        in_specs=[pl.BlockSpec(memory_space=pl.ANY),
                  pl.BlockSpec((1, d, tf), lambda s, e, j, n, *_: (e[s], 0, j[s])),
                  pl.BlockSpec((1, d, tf), lambda s, e, j, n, *_: (e[s], 0, j[s])),
                  pl.BlockSpec((1, d_ff, tf), lambda s, e, j, n, *_: (e[s], 0, n[s])),
                  pl.BlockSpec((1, 1, tf), lambda s, e, j, n, *_: (e[s], 0, j[s])),
                  pl.BlockSpec((1, 1, tf), lambda s, e, j, n, *_: (e[s], 0, j[s])),
                  pl.BlockSpec((1, 1, tf), lambda s, e, j, n, *_: (e[s], 0, n[s]))],
        out_specs=pl.BlockSpec(memory_space=pl.ANY),
        scratch_shapes=[pltpu.VMEM((rows_g, d), BF16),
                        pltpu.VMEM((n_j, rows_g, tf), BF16),
                        pltpu.VMEM((d, tf), BF16), pltpu.VMEM((d, tf), BF16), pltpu.VMEM((d_ff, tf), BF16),
                        pltpu.VMEM((2, tm, d), F32), pltpu.VMEM((2, tm, tf), F32), pltpu.SMEM((2,), I32),
                        pltpu.SemaphoreType.DMA((2,)), pltpu.SemaphoreType.DMA((2,))],
    )
    return pl.pallas_call(
        functools.partial(_expert_kernel, n_j=n_j, n_n=n_n, tm=tm, tf=tf),
        out_shape=jax.ShapeDtypeStruct((r_max, d), F32),
        grid_spec=grid_spec,
        compiler_params=_cparams(("arbitrary",)),
        name="expert_ffn",
    )(st_e, st_j, st_n, g_t0, g_gs, g_valid, xs, w_gate, w_up, w_down,
      b_gate.reshape(n_exp, 1, d_ff), b_up.reshape(n_exp, 1, d_ff), b_down.reshape(n_exp, 1, d))


def _combine_kernel(cnt_ref, base_ref, ys_hbm, x2_ref, pk_ref, gt_ref, gate_p_ref, gate_s_ref, g_ref,
                    yp_ref, ysm_ref, rows_ref, sem, *, n_experts, tile, n_prompt_tiles):
    i = pl.program_id(0)
    last = pl.num_programs(0) - 1

    def copy(step, dst, src, sz):
        slot = step & 1
        return pltpu.make_async_copy(ys_hbm.at[pl.ds(src, sz)], rows_ref.at[slot, pl.ds(dst, sz)], sem.at[slot])

    def fetch(step):
        _for_each_chunk_piece(cnt_ref, base_ref, step, n_experts, tile,
                              lambda s, d, sz: copy(step, s, d, sz).start())

    @pl.when(i == 0)
    def _():
        rows_ref[...] = jnp.zeros(rows_ref.shape, F32)
        fetch(i)

    @pl.when(i < last)
    def _():
        fetch(i + 1)

    _for_each_chunk_piece(cnt_ref, base_ref, i, n_experts, tile, lambda s, d, sz: copy(i, s, d, sz).wait())
    weights = _one_hot_rows(pk_ref[...], rows_ref.shape[1], gt_ref[...])
    y = jnp.dot(weights, rows_ref[i & 1].astype(BF16), preferred_element_type=F32)
    is_prompt = i < n_prompt_tiles
    gate = jnp.where(is_prompt, gate_p_ref[0], gate_s_ref[...])
    x3 = x2_ref[...] + gate * y
    ms = jnp.mean(x3 * x3, axis=-1, keepdims=True)
    out = x3 * lax.rsqrt(ms + NORM_EPS) * g_ref[...]

    @pl.when(is_prompt)
    def _():
        yp_ref[...] = out

    @pl.when(jnp.logical_not(is_prompt))
    def _():
        ysm_ref[...] = out


def _combine(ys, cnt_flat, base_flat, x2, pk4, gt, gate_p, gate_s, final_g, *, n_prompt, tiles_per_seq, n_experts):
    n, d = x2.shape
    tile = min(TOK_TILE, n_prompt)
    n_s = n - n_prompt
    assert n_s == tile and n_prompt % tile == 0
    npt = n_prompt // tile
    grid_spec = pltpu.PrefetchScalarGridSpec(
        num_scalar_prefetch=2,
        grid=(n // tile,),
        in_specs=[pl.BlockSpec(memory_space=pl.ANY),
                  pl.BlockSpec((tile, d), lambda i, *_: (i, 0)),
                  pl.BlockSpec((tile, LANES), lambda i, *_: (i, 0)),
                  pl.BlockSpec((tile, LANES), lambda i, *_: (i, 0)),
                  pl.BlockSpec((1, 1, d), lambda i, *_: (jnp.minimum(i, npt - 1) // tiles_per_seq, 0, 0)),
                  pl.BlockSpec((tile, d), lambda i, *_: (0, 0)),
                  pl.BlockSpec((1, d), lambda i, *_: (0, 0))],
        out_specs=(pl.BlockSpec((tile, d), lambda i, *_: (jnp.minimum(i, npt - 1), 0)),
                   pl.BlockSpec((tile, d), lambda i, *_: (0, 0))),
        scratch_shapes=[pltpu.VMEM((2, _sorted_rows(tile, n_experts), d), F32), pltpu.SemaphoreType.DMA((2,))],
    )
    return pl.pallas_call(
        functools.partial(_combine_kernel, n_experts=n_experts, tile=tile, n_prompt_tiles=npt),
        out_shape=(jax.ShapeDtypeStruct((n_prompt, d), F32), jax.ShapeDtypeStruct((n_s, d), F32)),
        grid_spec=grid_spec,
        compiler_params=_cparams(("arbitrary",)),
        name="combine_norm",
    )(cnt_flat, base_flat, ys, x2, pk4, gt, gate_p, gate_s, final_g)


def _expert_tables(tile_cnt, *, tile, n_j, n_n):
    n_tiles, n_experts = tile_cnt.shape
    tm, grp = EXP_TILE, EXP_GROUP
    cnt8 = (tile_cnt + (SUBLANES - 1)) // SUBLANES * SUBLANES
    rows_e = jnp.sum(cnt8, axis=0)
    max_rows = n_tiles * tile * TOP_K + n_experts * n_tiles * (SUBLANES - 1)
    t_max = -(-max_rows // tm) + n_experts
    ng_max = n_experts + t_max // grp
    ntile = (rows_e + tm - 1) // tm
    tile_start = jnp.cumsum(ntile) - ntile
    pstart = tile_start * tm
    base = pstart[None, :] + jnp.cumsum(cnt8, axis=0) - cnt8
    ng = (ntile + grp - 1) // grp
    cg = jnp.cumsum(ng)
    n_groups = cg[-1]
    q = jnp.arange(ng_max, dtype=I32)
    eq = jnp.minimum(jnp.sum((q[:, None] >= cg[None, :]).astype(I32), axis=1), n_experts - 1)
    onehot = eq[:, None] == jnp.arange(n_experts, dtype=I32)[None, :]

    def pick(v):
        return jnp.sum(jnp.where(onehot, v[None, :], 0), axis=1)

    lg = q - (pick(cg) - pick(ng))
    t0 = pick(tile_start) + lg * grp
    gs = jnp.clip(pick(ntile) - lg * grp, 0, grp)
    valid = q < n_groups
    is_last = q == jnp.maximum(n_groups - 1, 0)
    eq = jnp.where(valid, eq, jnp.sum(jnp.where(is_last, eq, 0)))
    t0 = jnp.where(valid, t0, jnp.sum(jnp.where(is_last, t0, 0)))
    gs = jnp.where(valid, gs, 0)
    steps = n_j + n_n
    ph = jnp.tile(jnp.arange(steps, dtype=I32), ng_max)
    vs = jnp.repeat(valid, steps)
    st_e = jnp.repeat(eq, steps)
    st_j = jnp.where(vs, jnp.minimum(ph, n_j - 1), n_j - 1)
    st_n = jnp.where(vs, jnp.maximum(ph - n_j, 0), n_n - 1)
    tail = jnp.maximum(pstart + (ntile - 1) * tm, 0)
    has = (ntile > 0).astype(I32)
    tabs = tuple(a.astype(I32) for a in (st_e, st_j, st_n, t0, gs, valid))
    return (tabs, tile_cnt.reshape(-1).astype(I32), base.reshape(-1).astype(I32), tail.astype(I32), has,
            t_max * tm, (n_groups * steps).astype(I32))


def _rope_tables(pos):
    inv = 1.0 / (ROPE_THETA ** (jnp.arange(0, HEAD_DIM, 2, dtype=F32) / HEAD_DIM))
    ang = pos.astype(F32)[:, None] * inv[None, :]
    reps = LANES // (HEAD_DIM // 2)
    return jnp.tile(jnp.cos(ang), (1, reps)), jnp.tile(jnp.sin(ang), (1, reps))


def kernel(x_prompt, x_sample, cache_k, cache_v, state_pool, page_table, c_prompt, c_sample, w_ada, b_ada, norm1_g, norm2_g, w_in, lam_q1, lam_k1, lam_q2, lam_k2, subln_g, w_pool, pool_scale, w_out, w_router, b_router, w_gate, b_gate, w_up, b_up, w_down, b_down, final_g):
    B, S, D = x_prompt.shape
    DB, T, _ = x_sample.shape
    depth = w_ada.shape[0]
    page = cache_k.shape[2]
    past = page_table.shape[1] * page
    d_att = D // 2
    n_heads = d_att // HEAD_W
    d_pool = w_in.shape[2] - 3 * d_att
    n_experts = w_router.shape[2]
    state_len = state_pool.shape[2]
    n_p, n_s = B * S, DB * T
    n_all = n_p + n_s
    tm = min(TOK_TILE, n_p)
    tiles_per_seq = S // tm
    assert depth == 1, "single-layer step"
    assert n_s == tm and S % tm == 0 and T % 8 == 0 and state_len < POOL_HALO <= tm

    cos_p, sin_p = _rope_tables(jnp.arange(S))
    cos_s, sin_s = _rope_tables(jnp.tile(past + jnp.arange(T), DB))

    l = 0
    lam_init = 0.8 - 0.6 * math.exp(-0.3 * l)
    lamp = jnp.zeros((8, LANES), F32)
    for r, vec in enumerate((lam_q1[l], lam_k1[l], lam_q2[l], lam_k2[l])):
        lamp = lamp.at[r, :HEAD_DIM].set(vec.astype(F32))
    subg = subln_g[l].reshape(1, HEAD_W)

    rows_c = -(-(B + DB) // 8) * 8
    c_all = jnp.zeros((rows_c, D), F32).at[:B].set(c_prompt).at[B:B + DB].set(c_sample)
    m_all = _adaln(c_all, w_ada[l], b_ada[l])
    mods_p = [m_all[:B, k * D:(k + 1) * D].reshape(B, 1, D) for k in range(N_ADA)]
    mods_s = [jnp.repeat(m_all[B:B + DB, k * D:(k + 1) * D], T, axis=0).reshape(1, n_s, D) for k in range(N_ADA)]

    w_in_b = _cast_bf16(w_in[l], 256)
    w_out_b = _cast_bf16(w_out[l], 256)
    ng, gw = w_pool.shape[1], w_pool.shape[2]
    w_pool_b = _cast_bf16(w_pool[l].reshape(ng * gw, gw), ng * gw).reshape(ng, gw, gw)
    g1 = norm1_g[l].reshape(1, D)
    g2 = norm2_g[l].reshape(1, D)
    pscale = pool_scale[l].reshape(1, d_pool)
    w_router_p = jnp.zeros((D, LANES), F32).at[:, :n_experts].set(w_router[l])
    b_router_p = jnp.zeros((1, LANES), F32).at[0, :n_experts].set(b_router[l].astype(F32))

    xp = x_prompt.reshape(n_p, D)
    q_p, k_p, v_p, u_p, kb_p, vb_p = _project(xp, mods_p[0], mods_p[1], g1, w_in_b, cos_p, sin_p,
                                              tiles_per_seq=tiles_per_seq, pos_tiles=tiles_per_seq)
    a_p = _attn_prompt(q_p, kb_p, vb_p, lamp, subg.reshape(HEAD_W, 1), batch=B, seq=S, lam_init=lam_init)
    pm_p = _pool_mix(u_p.reshape(B, S, d_pool), w_pool_b, pscale, tile=tm, cur_block0=0, pos0=0,
                     zero_first_halo=True).reshape(n_p, d_pool)

    xs_tok = x_sample.reshape(n_s, D)
    q_s, k_s, v_s, u_s, _, _ = _project(xs_tok, mods_s[0], mods_s[1], g1, w_in_b, cos_s, sin_s,
                                        tiles_per_seq=1, pos_tiles=1)
    a_s = _attn_sample(page_table, q_s, k_s, v_s, cache_k[l], cache_v[l], lamp, subg, t_new=T, lam_init=lam_init)
    u_ext = jnp.concatenate([jnp.zeros((DB, POOL_HALO - state_len, d_pool), F32),
                             state_pool[l].astype(F32), u_s.reshape(DB, T, d_pool)], axis=1)
    pm_s = _pool_mix(u_ext, w_pool_b, pscale, tile=T, cur_block0=POOL_HALO // T, pos0=past,
                     zero_first_halo=False).reshape(n_s, d_pool)

    outs = _mix_out(xp, a_p, pm_p, mods_p[2], mods_p[3], mods_p[4], g2, w_out_b, w_router_p, b_router_p,
                    n_total=n_all, tile0=0, tiles_per_seq=tiles_per_seq, n_experts=n_experts)
    x2, h2, pk4, gt4, tcnt = _mix_out(xs_tok, a_s, pm_s, mods_s[2], mods_s[3], mods_s[4], g2, w_out_b, w_router_p,
                                      b_router_p, n_total=n_all, tile0=n_p // tm, tiles_per_seq=1,
                                      n_experts=n_experts, prev=outs)

    d_ff = w_gate.shape[3]
    tf = min(FF_TILE, d_ff, D)
    tile_cnt = tcnt[:, 0, :n_experts].astype(I32)
    tabs, cnt_flat, base_flat, tail, has, r_max, n_steps = _expert_tables(
        tile_cnt, tile=tm, n_j=d_ff // tf, n_n=D // tf)
    xs_rows = _dispatch(h2, pk4, cnt_flat, base_flat, tail, has, r_max=r_max, n_experts=n_experts)
    ys_rows = _expert_ffn(tabs, xs_rows, w_gate[l], b_gate[l], w_up[l], b_up[l], w_down[l], b_down[l],
                          n_steps=n_steps)
    y_p, y_s = _combine(ys_rows, cnt_flat, base_flat, x2, pk4, gt4, mods_p[5], mods_s[5].reshape(n_s, D),
                        final_g.reshape(1, D), n_prompt=n_p, tiles_per_seq=tiles_per_seq, n_experts=n_experts)

    n_pages_p = S // page
    k_prompt = k_p.reshape(1, B, n_pages_p, page, n_heads, HEAD_W)
    v_prompt = v_p.reshape(1, B, n_pages_p, page, n_heads, HEAD_W)
    pool_prompt = u_p.reshape(B, S, d_pool)[:, S - state_len:][None]
    k_sample = k_s.reshape(1, DB, T, n_heads, HEAD_W)
    v_sample = v_s.reshape(1, DB, T, n_heads, HEAD_W)
    pool_sample = u_ext[:, -state_len:][None]
    return (y_p.reshape(B, S, D), y_s.reshape(DB, T, D), k_prompt, v_prompt, pool_prompt,
            k_sample, v_sample, pool_sample)
```

```python
import functools
import math

import jax
import jax.numpy as jnp
from jax import lax
from jax.experimental import pallas as pl
from jax.experimental.pallas import tpu as pltpu

F32 = jnp.float32
BF16 = jnp.bfloat16
I32 = jnp.int32

HEAD_DIM = 64
HEAD_W = 2 * HEAD_DIM
POOL_WINDOWS = (2, 4, 8, 16)
POOL_HALO = 16
TOP_K = 4
SWIGLU_LIMIT = 7.0
SWIGLU_ALPHA = 1.702
ROPE_THETA = 10000.0
NORM_EPS = 1e-5
N_ADA = 6
LANES = 128
SUBLANES = 8
V7X_VMEM_LIMIT = 58 * 1024 * 1024

TOK_TILE = 256
EXP_TILE = 256
EXP_GROUP = 6
FF_TILE = 512
Q_SCALE = (HEAD_DIM ** -0.5) * math.log2(math.e)


def _cparams(sem, vmem=V7X_VMEM_LIMIT):
    return pltpu.CompilerParams(dimension_semantics=sem, vmem_limit_bytes=vmem)


def _cast_kernel(x_ref, o_ref):
    o_ref[...] = x_ref[...].astype(o_ref.dtype)


def _cast_bf16(w, rows):
    r, c = w.shape
    return pl.pallas_call(
        _cast_kernel,
        out_shape=jax.ShapeDtypeStruct((r, c), BF16),
        grid=(r // rows,),
        in_specs=[pl.BlockSpec((rows, c), lambda i: (i, 0))],
        out_specs=pl.BlockSpec((rows, c), lambda i: (i, 0)),
        compiler_params=_cparams(("arbitrary",)),
        name="cast_bf16",
    )(w)


def _ada_kernel(c_ref, w_ref, b_ref, o_ref):
    c = c_ref[...]
    s = (c * jax.nn.sigmoid(c)).astype(BF16)
    o_ref[...] = jnp.dot(s, w_ref[...].astype(BF16), preferred_element_type=F32) + b_ref[...]


def _adaln(c_all, w_ada, b_ada):
    rows, d = c_all.shape
    n = w_ada.shape[1]
    tn = min(1024, n)
    return pl.pallas_call(
        _ada_kernel,
        out_shape=jax.ShapeDtypeStruct((rows, n), F32),
        grid=(n // tn,),
        in_specs=[pl.BlockSpec((rows, d), lambda j: (0, 0)),
                  pl.BlockSpec((d, tn), lambda j: (0, j)),
                  pl.BlockSpec((1, tn), lambda j: (0, j))],
        out_specs=pl.BlockSpec((rows, tn), lambda j: (0, j)),
        compiler_params=_cparams(("arbitrary",)),
        name="adaln",
    )(c_all, w_ada, b_ada.reshape(1, n))


def _modulated_norm(x, g, shift, scale):
    ms = jnp.mean(x * x, axis=-1, keepdims=True)
    return (x * lax.rsqrt(ms + NORM_EPS) * g) * (1.0 + scale) + shift


def _proj_kernel(x_ref, shift_ref, scale_ref, g_ref, w_ref, cos_ref, sin_ref,
                 q_ref, k_ref, v_ref, u_ref, kb_ref, vb_ref, *, d_att):
    h = _modulated_norm(x_ref[...], g_ref[...], shift_ref[0], scale_ref[0]).astype(BF16)
    cos = cos_ref[...]
    sin = sin_ref[...]
    lane = lax.broadcasted_iota(I32, cos.shape, 1)
    first_half = (lane & (HEAD_DIM - 1)) < (HEAD_DIM // 2)

    def rope(z):
        rot = jnp.where(first_half, -pltpu.roll(z, LANES - HEAD_DIM // 2, 1), pltpu.roll(z, HEAD_DIM // 2, 1))
        return z * cos + rot * sin

    zq = jnp.dot(h, w_ref[:, 0:d_att], preferred_element_type=F32)
    zk = jnp.dot(h, w_ref[:, d_att:2 * d_att], preferred_element_type=F32)
    for hh in range(d_att // HEAD_W):
        sl = slice(hh * HEAD_W, (hh + 1) * HEAD_W)
        q_ref[:, sl] = rope(zq[:, sl]) * Q_SCALE
        kr = rope(zk[:, sl])
        k_ref[:, sl] = kr
        kb_ref[:, sl] = kr.astype(BF16)
    zv = jnp.dot(h, w_ref[:, 2 * d_att:3 * d_att], preferred_element_type=F32)
    v_ref[...] = zv
    vb_ref[...] = zv.astype(BF16)
    u_ref[...] = jnp.dot(h, w_ref[:, 3 * d_att:], preferred_element_type=F32)


def _project(x, shift, scale, g, w_in_b, cos, sin, *, tiles_per_seq, pos_tiles):
    n, d = x.shape
    tm = min(TOK_TILE, n)
    d_in = w_in_b.shape[1]
    d_att = (d // 2)
    d_pool = d_in - 3 * d_att
    mod_rows = shift.shape[1]
    mod_spec = pl.BlockSpec((1, mod_rows, d), lambda i: (i // tiles_per_seq, 0, 0))
    tok = lambda c: pl.BlockSpec((tm, c), lambda i: (i, 0))
    return pl.pallas_call(
        functools.partial(_proj_kernel, d_att=d_att),
        out_shape=(jax.ShapeDtypeStruct((n, d_att), F32), jax.ShapeDtypeStruct((n, d_att), F32),
                   jax.ShapeDtypeStruct((n, d_att), F32), jax.ShapeDtypeStruct((n, d_pool), F32),
                   jax.ShapeDtypeStruct((n, d_att), BF16), jax.ShapeDtypeStruct((n, d_att), BF16)),
        grid=(n // tm,),
        in_specs=[tok(d), mod_spec, mod_spec,
                  pl.BlockSpec((1, d), lambda i: (0, 0)),
                  pl.BlockSpec((d, d_in), lambda i: (0, 0)),
                  pl.BlockSpec((tm, LANES), lambda i: (i % pos_tiles, 0)),
                  pl.BlockSpec((tm, LANES), lambda i: (i % pos_tiles, 0))],
        out_specs=(tok(d_att), tok(d_att), tok(d_att), tok(d_pool), tok(d_att), tok(d_att)),
        compiler_params=_cparams(("arbitrary",)),
        name="in_proj",
    )(x, shift, scale, g, w_in_b, cos, sin)


def _lambda_value(lam_ref, lam_init):
    lp = lam_ref[...]
    a = jnp.sum(lp[0:1] * lp[1:2], axis=1, keepdims=True)
    b = jnp.sum(lp[2:3] * lp[3:4], axis=1, keepdims=True)
    return jnp.exp(a) - jnp.exp(b) + lam_init


def _stack_maps(q):
    lane = lax.broadcasted_iota(I32, q.shape, 1)
    q1 = jnp.where(lane < HEAD_DIM, q, 0.0)
    q2 = jnp.where(lane >= HEAD_DIM, q, 0.0)
    return jnp.concatenate([q1, q2], axis=0).astype(BF16)


_NT = (((1,), (1,)), ((), ()))


_TN = (((0,), (0,)), ((), ()))


def _attn_prompt_kernel(lam_ref, g_ref, q_ref, k_ref, k2_ref, v_ref, o_ref, m_ref, l_ref, acc_ref,
                        *, tq, tk, lam_init):
    qi = pl.program_id(2)
    heads = q_ref.shape[1] // HEAD_W
    qqs = [_stack_maps(q_ref[:, hh * HEAD_W:(hh + 1) * HEAD_W]) for hh in range(heads)]
    m_ref[...] = jnp.full(m_ref.shape, -jnp.inf, F32)
    l_ref[...] = jnp.zeros(l_ref.shape, F32)
    acc_ref[...] = jnp.zeros(acc_ref.shape, F32)

    def block(j, masked):
        rows = pl.ds(pl.multiple_of(j * tk, tk), tk)
        for hh in range(heads):
            cols = slice(hh * HEAD_W, (hh + 1) * HEAD_W)

            def scores(kref):
                st = lax.dot_general(kref[rows, cols], qqs[hh], _NT, preferred_element_type=F32)
                if masked:
                    kpos = j * tk + lax.broadcasted_iota(I32, st.shape, 0)
                    c = lax.broadcasted_iota(I32, st.shape, 1)
                    qpos = qi * tq + jnp.where(c >= tq, c - tq, c)
                    st = jnp.where(kpos <= qpos, st, -jnp.inf)
                return st

            m_prev = m_ref[hh]
            m_new = jnp.maximum(m_prev, jnp.max(scores(k_ref), axis=0, keepdims=True))
            p = jnp.exp2(scores(k2_ref) - m_new)
            alpha = jnp.exp2(m_prev - m_new)
            l_ref[hh] = alpha * l_ref[hh] + jnp.sum(p, axis=0, keepdims=True)
            pv = lax.dot_general(v_ref[rows, cols], p.astype(BF16), _TN, preferred_element_type=F32)
            acc_ref[hh] = acc_ref[hh] * alpha + pv
            m_ref[hh] = m_new

    n_full = (qi * tq) // tk

    def body(j, carry):
        block(j, False)
        return carry

    lax.fori_loop(0, n_full, body, 0)
    block(n_full, True)

    lam = _lambda_value(lam_ref, lam_init)
    for hh in range(heads):
        l = l_ref[hh]
        acc = acc_ref[hh]
        o = acc[:, :tq] / l[:, :tq] - lam * (acc[:, tq:] / l[:, tq:])
        ms = jnp.mean(o * o, axis=0, keepdims=True)
        a = o * lax.rsqrt(ms + NORM_EPS) * g_ref[...] * (1.0 - lam_init)
        o_ref[:, hh * HEAD_W:(hh + 1) * HEAD_W] = a.T.astype(o_ref.dtype)


def _attn_prompt(q, kb, vb, lamp, g_col, *, batch, seq, lam_init):
    n, d_att = q.shape
    nh = d_att // HEAD_W
    hp = 4 if nh % 4 == 0 else 1
    tq = min(256, seq)
    tk = min(512, seq)
    nq = seq // tq
    return pl.pallas_call(
        functools.partial(_attn_prompt_kernel, tq=tq, tk=tk, lam_init=lam_init),
        out_shape=jax.ShapeDtypeStruct((n, d_att), BF16),
        grid=(batch, nh // hp, nq),
        in_specs=[pl.BlockSpec((8, LANES), lambda b, h, i: (0, 0)),
                  pl.BlockSpec((HEAD_W, 1), lambda b, h, i: (0, 0)),
                  pl.BlockSpec((tq, hp * HEAD_W), lambda b, h, i: (b * nq + i, h)),
                  pl.BlockSpec((seq, hp * HEAD_W), lambda b, h, i: (b, h)),
                  pl.BlockSpec((seq, hp * HEAD_W), lambda b, h, i: (b, h)),
                  pl.BlockSpec((seq, hp * HEAD_W), lambda b, h, i: (b, h))],
        out_specs=pl.BlockSpec((tq, hp * HEAD_W), lambda b, h, i: (b * nq + i, h)),
        scratch_shapes=[pltpu.VMEM((hp, 1, 2 * tq), F32), pltpu.VMEM((hp, 1, 2 * tq), F32),
                        pltpu.VMEM((hp, HEAD_W, 2 * tq), F32)],
        compiler_params=_cparams(("arbitrary", "arbitrary", "arbitrary")),
        name="attn_prompt",
    )(lamp, g_col, q, kb, kb, vb)


def _attn_sample_kernel(pt_ref, lam_ref, g_ref, q_ref, kn_ref, vn_ref, *rest, n_heads, pages, t_new, lam_init):
    k_pages = rest[:pages]
    v_pages = rest[pages:2 * pages]
    o_ref = rest[2 * pages]
    m_ref, l_ref, acc_ref = rest[2 * pages + 1:]
    c = pl.program_id(1)
    rows_h = 2 * t_new
    assert n_heads == SUBLANES and n_heads * rows_h == LANES

    @pl.when(c == 0)
    def _():
        m_ref[...] = jnp.full(m_ref.shape, -jnp.inf, F32)
        l_ref[...] = jnp.zeros(l_ref.shape, F32)
        acc_ref[...] = jnp.zeros(acc_ref.shape, F32)

    q = q_ref[...]
    q_all = jnp.concatenate([_stack_maps(q[:, hh * HEAD_W:(hh + 1) * HEAD_W]) for hh in range(n_heads)], axis=0)
    sub = lax.broadcasted_iota(I32, (SUBLANES, LANES), 0)
    lane = lax.broadcasted_iota(I32, (SUBLANES, LANES), 1)
    own_head = (lane // rows_h) == sub

    def update(k_rows, v_rows, valid):
        n_pos = k_rows.shape[0] // n_heads
        r = lax.dot_general(k_rows.astype(BF16), q_all, _NT, preferred_element_type=F32)
        r = r.reshape(n_pos, n_heads, LANES)
        if valid is not None:
            r = jnp.where(valid, r, -jnp.inf)
        m_old = m_ref[...]
        m_new = jnp.maximum(m_old, jnp.max(r, axis=0))
        p = jnp.exp2(r - m_new[None])
        alpha = jnp.exp2(m_old - m_new)
        l_ref[...] = alpha * l_ref[...] + jnp.sum(p, axis=0)
        p_own = jnp.where(own_head[None], p, 0.0).reshape(n_pos * n_heads, LANES).astype(BF16)
        alpha_row = jnp.sum(jnp.where(own_head, alpha, 0.0), axis=0, keepdims=True)
        pv = lax.dot_general(v_rows.astype(BF16), p_own, _TN, preferred_element_type=F32)
        acc_ref[...] = acc_ref[...] * alpha_row + pv
        m_ref[...] = m_new

    for i in range(pages):
        update(k_pages[i][0], v_pages[i][0], None)

    @pl.when(c == pl.num_programs(1) - 1)
    def _():
        t_key = lax.broadcasted_iota(I32, (t_new, n_heads, LANES), 0)
        t_query = lax.broadcasted_iota(I32, (t_new, n_heads, LANES), 2) % t_new
        update(kn_ref[0], vn_ref[0], t_key <= t_query)
        l_row = jnp.sum(jnp.where(own_head, l_ref[...], 0.0), axis=0, keepdims=True)
        o_all = (acc_ref[...] / l_row).T
        lam = _lambda_value(lam_ref, lam_init)
        for hh in range(n_heads):
            rows = o_all[hh * rows_h:(hh + 1) * rows_h]
            o = rows[:t_new] - lam * rows[t_new:]
            ms = jnp.mean(o * o, axis=1, keepdims=True)
            o_ref[:, hh * HEAD_W:(hh + 1) * HEAD_W] = o * lax.rsqrt(ms + NORM_EPS) * g_ref[...] * (1.0 - lam_init)


def _attn_sample(page_table, q, k_new, v_new, cache_k, cache_v, lamp, g, *, t_new, lam_init):
    n, d_att = q.shape
    nh = d_att // HEAD_W
    db, n_pages = page_table.shape
    n_pool, page = cache_k.shape[0], cache_k.shape[1]
    pages = min(8, n_pages)
    ck = cache_k.reshape(n_pool, page * nh, HEAD_W)
    cv = cache_v.reshape(n_pool, page * nh, HEAD_W)

    def page_spec(i):
        return pl.BlockSpec((1, page * nh, HEAD_W), lambda b, c, pt: (pt[b, c * pages + i], 0, 0))

    tok = pl.BlockSpec((t_new, d_att), lambda b, c, pt: (b, 0))
    new_rows = pl.BlockSpec((1, t_new * nh, HEAD_W), lambda b, c, pt: (b, 0, 0))
    k_new = k_new.reshape(db, t_new, nh, HEAD_W).reshape(db, t_new * nh, HEAD_W)
    v_new = v_new.reshape(db, t_new, nh, HEAD_W).reshape(db, t_new * nh, HEAD_W)
    grid_spec = pltpu.PrefetchScalarGridSpec(
        num_scalar_prefetch=1,
        grid=(db, n_pages // pages),
        in_specs=[pl.BlockSpec((8, LANES), lambda b, c, pt: (0, 0)),
                  pl.BlockSpec((1, HEAD_W), lambda b, c, pt: (0, 0)),
                  tok, new_rows, new_rows]
                 + [page_spec(i) for i in range(pages)] + [page_spec(i) for i in range(pages)],
        out_specs=tok,
        scratch_shapes=[pltpu.VMEM((nh, LANES), F32), pltpu.VMEM((nh, LANES), F32),
                        pltpu.VMEM((HEAD_W, LANES), F32)],
    )
    return pl.pallas_call(
        functools.partial(_attn_sample_kernel, n_heads=nh, pages=pages, t_new=t_new, lam_init=lam_init),
        out_shape=jax.ShapeDtypeStruct((n, d_att), F32),
        grid_spec=grid_spec,
        compiler_params=_cparams(("arbitrary", "arbitrary")),
        name="attn_sample",
    )(page_table, lamp, g, q, k_new, v_new, *([ck] * pages), *([cv] * pages))


def _pool_kernel(halo_ref, cur_ref, w_ref, scale_ref, o_ref, ext_ref, *, pos0, tile_pos, zero_first_halo):
    i = pl.program_id(1)
    t = cur_ref.shape[1]
    halo = halo_ref[0]
    if zero_first_halo:
        halo = jnp.where(i == 0, 0.0, halo)
    cur = cur_ref[0]
    ext_ref[0:POOL_HALO, :] = halo
    ext_ref[POOL_HALO:POOL_HALO + t, :] = cur
    pos = pos0 + i * tile_pos + lax.broadcasted_iota(I32, (t, 1), 0)
    gw = cur.shape[1] // len(POOL_WINDOWS)
    for gi, w in enumerate(POOL_WINDOWS):
        cols = slice(gi * gw, (gi + 1) * gw)
        total = cur[:, cols]
        for j in range(1, w):
            total = total + ext_ref[POOL_HALO - j:POOL_HALO - j + t, cols]
        cnt = jnp.minimum(pos + 1, w).astype(F32)
        dlt = total / cnt - cur[:, cols]
        y = jnp.dot(dlt.astype(BF16), w_ref[gi], preferred_element_type=F32)
        o_ref[0, :, cols] = (y * scale_ref[:, cols]).astype(o_ref.dtype)


def _pool_mix(u3, w_pool_b, scale, *, tile, cur_block0, pos0, zero_first_halo):
    b, rows, c = u3.shape
    n_tiles = (rows - cur_block0 * tile) // tile
    ng, gw = w_pool_b.shape[0], w_pool_b.shape[1]
    assert (cur_block0 * tile) % POOL_HALO == 0 and (tile % POOL_HALO == 0 or n_tiles == 1)

    def halo_map(bi, i):
        return (bi, jnp.maximum(((cur_block0 + i) * tile) // POOL_HALO - 1, 0), 0)

    return pl.pallas_call(
        functools.partial(_pool_kernel, pos0=pos0, tile_pos=tile, zero_first_halo=zero_first_halo),
        out_shape=jax.ShapeDtypeStruct((b, n_tiles * tile, c), BF16),
        grid=(b, n_tiles),
        in_specs=[pl.BlockSpec((1, POOL_HALO, c), halo_map),
                  pl.BlockSpec((1, tile, c), lambda bi, i: (bi, cur_block0 + i, 0)),
                  pl.BlockSpec((ng, gw, gw), lambda bi, i: (0, 0, 0)),
                  pl.BlockSpec((1, c), lambda bi, i: (0, 0))],
        out_specs=pl.BlockSpec((1, tile, c), lambda bi, i: (bi, i, 0)),
        scratch_shapes=[pltpu.VMEM((POOL_HALO + tile, c), F32)],
        compiler_params=_cparams(("arbitrary", "arbitrary")),
        name="pool_mix",
    )(u3, u3, w_pool_b, scale)


def _split_bf16(x):
    hi = x.astype(BF16)
    lo = (x - hi.astype(F32)).astype(BF16)
    return hi, lo


def _round_up_f32(x, m):
    return jnp.floor((x + (m - 1.0)) * (1.0 / m)) * m


def _mix_out_kernel(*refs, n_experts, aliased):
    (x_ref, a_ref, pm_ref, gate_ref, shift_ref, scale_ref, g_ref, wo_ref, wr_ref, br_ref) = refs[:10]
    x2_ref, h2_ref, pk_ref, gt_ref, tc_ref = refs[10 + aliased:]
    d_att = a_ref.shape[1]
    mix = (jnp.dot(a_ref[...].astype(BF16), wo_ref[0:d_att, :], preferred_element_type=F32)
           + jnp.dot(pm_ref[...], wo_ref[d_att:, :], preferred_element_type=F32))
    x2 = x_ref[...] + gate_ref[0] * mix
    x2_ref[...] = x2
    h2 = _modulated_norm(x2, g_ref[...], shift_ref[0], scale_ref[0])
    h2_ref[...] = h2.astype(BF16)
    hh, hl = _split_bf16(h2)
    wh, wl = _split_bf16(wr_ref[...])
    logits = (jnp.dot(hh, wh, preferred_element_type=F32) + jnp.dot(hl, wh, preferred_element_type=F32)
              + jnp.dot(hh, wl, preferred_element_type=F32)) + br_ref[...]
    t = logits.shape[0]
    lane = lax.broadcasted_iota(I32, logits.shape, 1)
    lanef = lane.astype(F32)
    work = jnp.where(lane < n_experts, logits, -jnp.inf)
    vals, ids = [], []
    for _ in range(TOP_K):
        mx = jnp.max(work, axis=1, keepdims=True)
        ix = jnp.min(jnp.where(work == mx, lanef, float(LANES)), axis=1, keepdims=True)
        vals.append(mx)
        ids.append(ix)
        work = jnp.where(lanef == ix, -jnp.inf, work)
    es = [jnp.exp(v - vals[0]) for v in vals]
    den = es[0]
    for e in es[1:]:
        den = den + e
    sel = jnp.zeros(logits.shape, F32)
    for k in range(TOP_K):
        sel = jnp.where(lanef == ids[k], 1.0, sel)
    r = lax.broadcasted_iota(I32, (t, t), 0)
    c = lax.broadcasted_iota(I32, (t, t), 1)
    earlier = jnp.where(c < r, 1.0, 0.0).astype(BF16)
    local_rank = jnp.dot(earlier, sel.astype(BF16), preferred_element_type=F32)
    cnt = jnp.sum(sel, axis=0, keepdims=True)
    cnt8 = jnp.broadcast_to(_round_up_f32(cnt, float(SUBLANES)), (SUBLANES, LANES))
    er = lax.broadcasted_iota(I32, (LANES, LANES), 0)
    ec = lax.broadcasted_iota(I32, (LANES, LANES), 1)
    before = jnp.where(er < ec, 1.0, 0.0).astype(BF16)
    chunk_off = jnp.dot(cnt8.astype(BF16), before, preferred_element_type=F32)[0:1, :]
    pos = local_rank + chunk_off
    pk_out = jnp.zeros(logits.shape, F32)
    gt_out = jnp.zeros(logits.shape, F32)
    for k in range(TOP_K):
        pk = jnp.sum(jnp.where(lanef == ids[k], pos, 0.0), axis=1, keepdims=True)
        pk_out = jnp.where(lane == k, pk, pk_out)
        gt_out = jnp.where(lane == k, es[k] / den, gt_out)
    pk_ref[...] = pk_out
    gt_ref[...] = gt_out
    tc_ref[0] = jnp.broadcast_to(cnt, (SUBLANES, LANES))


def _mix_out(x, a, pm, gate, shift, scale, g, w_out_b, w_router_p, b_router_p, *, n_total, tile0,
             tiles_per_seq, n_experts, prev=None):
    n, d = x.shape
    tm = min(TOK_TILE, n)
    d_att = a.shape[1]
    mod_rows = gate.shape[1]
    mod_spec = pl.BlockSpec((1, mod_rows, d), lambda i: (i // tiles_per_seq, 0, 0))
    tok = lambda c: pl.BlockSpec((tm, c), lambda i: (i, 0))
    out_tok = lambda c: pl.BlockSpec((tm, c), lambda i: (tile0 + i, 0))
    out_shape = (jax.ShapeDtypeStruct((n_total, d), F32), jax.ShapeDtypeStruct((n_total, d), BF16),
                 jax.ShapeDtypeStruct((n_total, LANES), F32), jax.ShapeDtypeStruct((n_total, LANES), F32),
                 jax.ShapeDtypeStruct((n_total // tm, SUBLANES, LANES), F32))
    in_specs = [tok(d), tok(d_att), tok(pm.shape[1]), mod_spec, mod_spec, mod_spec,
                pl.BlockSpec((1, d), lambda i: (0, 0)),
                pl.BlockSpec(w_out_b.shape, lambda i: (0, 0)),
                pl.BlockSpec(w_router_p.shape, lambda i: (0, 0)),
                pl.BlockSpec((1, LANES), lambda i: (0, 0))]
    args = [x, a, pm, gate, shift, scale, g, w_out_b, w_router_p, b_router_p]
    aliases = {}
    n_alias = 0
    if prev is not None:
        n_alias = len(prev)
        in_specs += [pl.BlockSpec(memory_space=pl.ANY)] * n_alias
        aliases = {len(args) + k: k for k in range(n_alias)}
        args += list(prev)
    return pl.pallas_call(
        functools.partial(_mix_out_kernel, n_experts=n_experts, aliased=n_alias),
        out_shape=out_shape,
        grid=(n // tm,),
        in_specs=in_specs,
        out_specs=(out_tok(d), out_tok(d), out_tok(LANES), out_tok(LANES),
                   pl.BlockSpec((1, SUBLANES, LANES), lambda i: (tile0 + i, 0, 0))),
        input_output_aliases=aliases,
        compiler_params=_cparams(("arbitrary",)),
        name="mix_out_router",
    )(*args)


def _chunk_sizes(tile):
    sizes = []
    s = tile
    while s >= SUBLANES:
        sizes.append(s)
        s //= 2
    return sizes


def _for_each_chunk_piece(cnt_ref, base_ref, i, n_experts, tile, fn):
    def per_expert(e, off):
        l8 = (cnt_ref[i * n_experts + e] + (SUBLANES - 1)) & (-SUBLANES)
        dst0 = base_ref[i * n_experts + e]
        done = jnp.int32(0)
        for sz in _chunk_sizes(tile):
            take = l8 & sz

            @pl.when(take != 0)
            def _(done=done, sz=sz):
                fn(pl.multiple_of(off + done, SUBLANES), pl.multiple_of(dst0 + done, SUBLANES), sz)

            done = done + take
        return off + l8

    return lax.fori_loop(0, n_experts, per_expert, jnp.int32(0))


def _one_hot_rows(pk, rows, weights=None):
    t = pk.shape[0]
    col = lax.broadcasted_iota(I32, (t, rows), 1).astype(F32)
    out = jnp.zeros((t, rows), F32)
    for k in range(TOP_K):
        w = 1.0 if weights is None else weights[:, k:k + 1]
        out = jnp.where(col == pk[:, k:k + 1], w, out)
    return out.astype(BF16)


def _dispatch_kernel(cnt_ref, base_ref, tail_ref, has_ref, h_ref, pk_ref, xs_hbm, sorted_ref, zero_ref, sem,
                     zero_sem, *, n_experts, tile):
    i = pl.program_id(0)
    last = pl.num_programs(0) - 1

    def zero_copy(e):
        return pltpu.make_async_copy(
            zero_ref, xs_hbm.at[pl.ds(pl.multiple_of(tail_ref[e], EXP_TILE), EXP_TILE)], zero_sem)

    @pl.when(i == 0)
    def _():
        zero_ref[...] = jnp.zeros(zero_ref.shape, F32)
        for e in range(n_experts):
            @pl.when(has_ref[e] == 1)
            def _(e=e):
                zero_copy(e).start()
        for e in range(n_experts):
            @pl.when(has_ref[e] == 1)
            def _(e=e):
                zero_copy(e).wait()

    def copy(step, src, dst, sz):
        slot = step & 1
        return pltpu.make_async_copy(sorted_ref.at[slot, pl.ds(src, sz)], xs_hbm.at[pl.ds(dst, sz)], sem.at[slot])

    onehot = _one_hot_rows(pk_ref[...], sorted_ref.shape[1])
    sorted_ref[i & 1] = lax.dot_general(onehot, h_ref[...], _TN, preferred_element_type=F32)
    _for_each_chunk_piece(cnt_ref, base_ref, i, n_experts, tile, lambda s, d, sz: copy(i, s, d, sz).start())

    @pl.when(i > 0)
    def _():
        _for_each_chunk_piece(cnt_ref, base_ref, i - 1, n_experts, tile,
                              lambda s, d, sz: copy(i - 1, s, d, sz).wait())

    @pl.when(i == last)
    def _():
        _for_each_chunk_piece(cnt_ref, base_ref, i, n_experts, tile, lambda s, d, sz: copy(i, s, d, sz).wait())


def _sorted_rows(tile, n_experts):
    rows = tile * TOP_K + n_experts * (SUBLANES - 1)
    return -(-rows // LANES) * LANES


def _dispatch(h2, pk4, cnt_flat, base_flat, tail_rows, has_tile, *, r_max, n_experts):
    n, d = h2.shape
    tile = min(TOK_TILE, n)
    grid_spec = pltpu.PrefetchScalarGridSpec(
        num_scalar_prefetch=4,
        grid=(n // tile,),
        in_specs=[pl.BlockSpec((tile, d), lambda i, *_: (i, 0)),
                  pl.BlockSpec((tile, LANES), lambda i, *_: (i, 0))],
        out_specs=pl.BlockSpec(memory_space=pl.ANY),
        scratch_shapes=[pltpu.VMEM((2, _sorted_rows(tile, n_experts), d), F32),
                        pltpu.VMEM((EXP_TILE, d), F32), pltpu.SemaphoreType.DMA((2,)), pltpu.SemaphoreType.DMA],
    )
    return pl.pallas_call(
        functools.partial(_dispatch_kernel, n_experts=n_experts, tile=tile),
        out_shape=jax.ShapeDtypeStruct((r_max, d), F32),
        grid_spec=grid_spec,
        compiler_params=_cparams(("arbitrary",)),
        name="dispatch_rows",
    )(cnt_flat, base_flat, tail_rows, has_tile, h2, pk4)


def _expert_kernel(st_e, st_j, st_n, g_t0, g_gs, g_valid,
                   xs_hbm, wg_ref, wu_ref, wd_ref, bg_ref, bu_ref, bd_ref, ys_hbm,
                   xbuf, actbuf, wgb, wub, wdb, stage_in, stage_out, pending, sem_in, sem_out,
                   *, n_j, n_n, tm, tf):
    s = pl.program_id(0)
    steps = n_j + n_n
    q = s // steps
    ph = s - q * steps
    valid = g_valid[q] == 1
    gs = g_gs[q]
    row0 = g_t0[q] * tm

    @pl.when(s == 0)
    def _():
        pending[0] = 0
        pending[1] = 0

    @pl.when(jnp.logical_and(valid, ph == 0))
    def _():
        def in_copy(i):
            slot = i & 1
            return pltpu.make_async_copy(xs_hbm.at[pl.ds(pl.multiple_of(row0 + i * tm, tm), tm)],
                                         stage_in.at[slot], sem_in.at[slot])

        @pl.when(gs >= 1)
        def _():
            in_copy(0).start()

        def load(i, carry):
            @pl.when(i + 1 < gs)
            def _():
                in_copy(i + 1).start()

            in_copy(i).wait()
            xbuf[pl.ds(pl.multiple_of(i * tm, tm), tm), :] = stage_in[i & 1].astype(BF16)
            return carry

        lax.fori_loop(0, gs, load, 0)

    n_pairs = gs // 2
    has_tail = (gs & 1) == 1
    tail_row = pl.multiple_of(n_pairs * 2 * tm, tm)

    @pl.when(jnp.logical_and(valid, ph < n_j))
    def _():
        wgb[...] = wg_ref[0].astype(BF16)
        wub[...] = wu_ref[0].astype(BF16)

        def tile(r0, size):
            rows = pl.ds(r0, size)
            x = xbuf[rows, :]
            gate = jnp.dot(x, wgb[...], preferred_element_type=F32) + bg_ref[0]
            up = jnp.dot(x, wub[...], preferred_element_type=F32) + bu_ref[0]
            gate = jnp.minimum(gate, SWIGLU_LIMIT)
            up = jnp.clip(up, -SWIGLU_LIMIT, SWIGLU_LIMIT)
            act = (up + 1.0) * (gate * jax.nn.sigmoid(SWIGLU_ALPHA * gate))
            actbuf[ph, rows, :] = act.astype(BF16)

        def pair(i, carry):
            tile(pl.multiple_of(i * 2 * tm, 2 * tm), 2 * tm)
            return carry

        lax.fori_loop(0, n_pairs, pair, 0)

        @pl.when(has_tail)
        def _():
            tile(tail_row, tm)

    @pl.when(jnp.logical_and(valid, ph >= n_j))
    def _():
        wdb[...] = wd_ref[0].astype(BF16)
        col0 = pl.multiple_of((ph - n_j) * tf, tf)

        def out_copy(slot, r0, size):
            return pltpu.make_async_copy(
                stage_out.at[slot, pl.ds(0, size)],
                ys_hbm.at[pl.ds(pl.multiple_of(row0 + r0, tm), size), pl.ds(col0, tf)],
                sem_out.at[slot])

        def wait_slot(slot):
            for n_tiles in (1, 2):
                @pl.when(pending[slot] == n_tiles)
                def _(n_tiles=n_tiles):
                    out_copy(slot, 0, n_tiles * tm).wait()
                    pending[slot] = 0

        def tile(r0, size, slot):
            rows = pl.ds(r0, size)
            wait_slot(slot)
            y = bd_ref[0] + jnp.dot(actbuf[0, rows, :], wdb[0:tf, :], preferred_element_type=F32)
            for j in range(1, n_j):
                y = y + jnp.dot(actbuf[j, rows, :], wdb[j * tf:(j + 1) * tf, :], preferred_element_type=F32)
            stage_out[slot, 0:size] = y
            out_copy(slot, r0, size).start()
            pending[slot] = size // tm

        def pair(i, carry):
            tile(pl.multiple_of(i * 2 * tm, 2 * tm), 2 * tm, i & 1)
            return carry

        lax.fori_loop(0, n_pairs, pair, 0)

        @pl.when(has_tail)
        def _():
            tile(tail_row, tm, n_pairs & 1)

        @pl.when(s == pl.num_programs(0) - 1)
        def _():
            wait_slot(0)
            wait_slot(1)


def _expert_ffn(tables, xs, w_gate, b_gate, w_up, b_up, w_down, b_down, *, n_steps):
    st_e, st_j, st_n, g_t0, g_gs, g_valid = tables
    r_max, d = xs.shape
    n_exp, _, d_ff = w_gate.shape
    tf = min(FF_TILE, d_ff, d)
    n_j = d_ff // tf
    n_n = d // tf
    tm = EXP_TILE
    rows_g = EXP_GROUP * tm
    grid_spec = pltpu.PrefetchScalarGridSpec(
        num_scalar_prefetch=6,
        grid=(n_steps,),
        in_specs=[pl.BlockSpec(memory_space=pl.ANY),
                  pl.BlockSpec((1, d, tf), lambda s, e, j, n, *_: (e[s], 0, j[s])),
                  pl.BlockSpec((1, d, tf), lambda s, e, j, n, *_: (e[s], 0, j[s])),
                  pl.BlockSpec((1, d_ff, tf), lambda s, e, j, n, *_: (e[s], 0, n[s])),
                  pl.BlockSpec((1, 1, tf), lambda s, e, j, n, *_: (e[s], 0, j[s])),
                  pl.BlockSpec((1, 1, tf), lambda s, e, j, n, *_: (e[s], 0, j[s])),
                  pl.BlockSpec((1, 1, tf), lambda s, e, j, n, *_: (e[s], 0, n[s]))],
        out_specs=pl.BlockSpec(memory_space=pl.ANY),
        scratch_shapes=[pltpu.VMEM((rows_g, d), BF16),
                        pltpu.VMEM((n_j, rows_g, tf), BF16),
                        pltpu.VMEM((d, tf), BF16), pltpu.VMEM((d, tf), BF16), pltpu.VMEM((d_ff, tf), BF16),
                        pltpu.VMEM((2, tm, d), F32), pltpu.VMEM((2, 2 * tm, tf), F32), pltpu.SMEM((2,), I32),
                        pltpu.SemaphoreType.DMA((2,)), pltpu.SemaphoreType.DMA((2,))],
    )
    return pl.pallas_call(
        functools.partial(_expert_kernel, n_j=n_j, n_n=n_n, tm=tm, tf=tf),
        out_shape=jax.ShapeDtypeStruct((r_max, d), F32),
        grid_spec=grid_spec,
        compiler_params=_cparams(("arbitrary",)),
        name="expert_ffn",
    )(st_e, st_j, st_n, g_t0, g_gs, g_valid, xs, w_gate, w_up, w_down,
      b_gate.reshape(n_exp, 1, d_ff), b_up.reshape(n_exp, 1, d_ff), b_down.reshape(n_exp, 1, d))


def _combine_kernel(cnt_ref, base_ref, ys_hbm, x2_ref, pk_ref, gt_ref, gate_p_ref, gate_s_ref, g_ref,
                    yp_ref, ysm_ref, rows_ref, sem, *, n_experts, tile, n_prompt_tiles):
    i = pl.program_id(0)
    last = pl.num_programs(0) - 1

    def copy(step, dst, src, sz):
        slot = step & 1
        return pltpu.make_async_copy(ys_hbm.at[pl.ds(src, sz)], rows_ref.at[slot, pl.ds(dst, sz)], sem.at[slot])

    def fetch(step):
        _for_each_chunk_piece(cnt_ref, base_ref, step, n_experts, tile,
                              lambda s, d, sz: copy(step, s, d, sz).start())

    @pl.when(i == 0)
    def _():
        rows_ref[...] = jnp.zeros(rows_ref.shape, F32)
        fetch(i)

    @pl.when(i < last)
    def _():
        fetch(i + 1)

    _for_each_chunk_piece(cnt_ref, base_ref, i, n_experts, tile, lambda s, d, sz: copy(i, s, d, sz).wait())
    weights = _one_hot_rows(pk_ref[...], rows_ref.shape[1], gt_ref[...])
    y = jnp.dot(weights, rows_ref[i & 1].astype(BF16), preferred_element_type=F32)
    is_prompt = i < n_prompt_tiles
    gate = jnp.where(is_prompt, gate_p_ref[0], gate_s_ref[...])
    x3 = x2_ref[...] + gate * y
    ms = jnp.mean(x3 * x3, axis=-1, keepdims=True)
    out = x3 * lax.rsqrt(ms + NORM_EPS) * g_ref[...]

    @pl.when(is_prompt)
    def _():
        yp_ref[...] = out

    @pl.when(jnp.logical_not(is_prompt))
    def _():
        ysm_ref[...] = out


def _combine(ys, cnt_flat, base_flat, x2, pk4, gt, gate_p, gate_s, final_g, *, n_prompt, tiles_per_seq, n_experts):
    n, d = x2.shape
    tile = min(TOK_TILE, n_prompt)
    n_s = n - n_prompt
    assert n_s == tile and n_prompt % tile == 0
    npt = n_prompt // tile
    grid_spec = pltpu.PrefetchScalarGridSpec(
        num_scalar_prefetch=2,
        grid=(n // tile,),
        in_specs=[pl.BlockSpec(memory_space=pl.ANY),
                  pl.BlockSpec((tile, d), lambda i, *_: (i, 0)),
                  pl.BlockSpec((tile, LANES), lambda i, *_: (i, 0)),
                  pl.BlockSpec((tile, LANES), lambda i, *_: (i, 0)),
                  pl.BlockSpec((1, 1, d), lambda i, *_: (jnp.minimum(i, npt - 1) // tiles_per_seq, 0, 0)),
                  pl.BlockSpec((tile, d), lambda i, *_: (0, 0)),
                  pl.BlockSpec((1, d), lambda i, *_: (0, 0))],
        out_specs=(pl.BlockSpec((tile, d), lambda i, *_: (jnp.minimum(i, npt - 1), 0)),
                   pl.BlockSpec((tile, d), lambda i, *_: (0, 0))),
        scratch_shapes=[pltpu.VMEM((2, _sorted_rows(tile, n_experts), d), F32), pltpu.SemaphoreType.DMA((2,))],
    )
    return pl.pallas_call(
        functools.partial(_combine_kernel, n_experts=n_experts, tile=tile, n_prompt_tiles=npt),
        out_shape=(jax.ShapeDtypeStruct((n_prompt, d), F32), jax.ShapeDtypeStruct((n_s, d), F32)),
        grid_spec=grid_spec,
        compiler_params=_cparams(("arbitrary",)),
        name="combine_norm",
    )(cnt_flat, base_flat, ys, x2, pk4, gt, gate_p, gate_s, final_g)


def _expert_tables(tile_cnt, *, tile, n_j, n_n):
    n_tiles, n_experts = tile_cnt.shape
    tm, grp = EXP_TILE, EXP_GROUP
    cnt8 = (tile_cnt + (SUBLANES - 1)) // SUBLANES * SUBLANES
    rows_e = jnp.sum(cnt8, axis=0)
    max_rows = n_tiles * tile * TOP_K + n_experts * n_tiles * (SUBLANES - 1)
    t_max = -(-max_rows // tm) + n_experts
    ng_max = n_experts + t_max // grp
    ntile = (rows_e + tm - 1) // tm
    tile_start = jnp.cumsum(ntile) - ntile
    pstart = tile_start * tm
    base = pstart[None, :] + jnp.cumsum(cnt8, axis=0) - cnt8
    ng = (ntile + grp - 1) // grp
    cg = jnp.cumsum(ng)
    n_groups = cg[-1]
    q = jnp.arange(ng_max, dtype=I32)
    eq = jnp.minimum(jnp.sum((q[:, None] >= cg[None, :]).astype(I32), axis=1), n_experts - 1)
    onehot = eq[:, None] == jnp.arange(n_experts, dtype=I32)[None, :]

    def pick(v):
        return jnp.sum(jnp.where(onehot, v[None, :], 0), axis=1)

    lg = q - (pick(cg) - pick(ng))
    t0 = pick(tile_start) + lg * grp
    gs = jnp.clip(pick(ntile) - lg * grp, 0, grp)
    valid = q < n_groups
    is_last = q == jnp.maximum(n_groups - 1, 0)
    eq = jnp.where(valid, eq, jnp.sum(jnp.where(is_last, eq, 0)))
    t0 = jnp.where(valid, t0, jnp.sum(jnp.where(is_last, t0, 0)))
    gs = jnp.where(valid, gs, 0)
    steps = n_j + n_n
    ph = jnp.tile(jnp.arange(steps, dtype=I32), ng_max)
    vs = jnp.repeat(valid, steps)
    st_e = jnp.repeat(eq, steps)
    st_j = jnp.where(vs, jnp.minimum(ph, n_j - 1), n_j - 1)
    st_n = jnp.where(vs, jnp.maximum(ph - n_j, 0), n_n - 1)
    tail = jnp.maximum(pstart + (ntile - 1) * tm, 0)
    has = (ntile > 0).astype(I32)
    tabs = tuple(a.astype(I32) for a in (st_e, st_j, st_n, t0, gs, valid))
    return (tabs, tile_cnt.reshape(-1).astype(I32), base.reshape(-1).astype(I32), tail.astype(I32), has,
            t_max * tm, (n_groups * steps).astype(I32))


def _rope_tables(pos):
    inv = 1.0 / (ROPE_THETA ** (jnp.arange(0, HEAD_DIM, 2, dtype=F32) / HEAD_DIM))
    ang = pos.astype(F32)[:, None] * inv[None, :]
    reps = LANES // (HEAD_DIM // 2)
    return jnp.tile(jnp.cos(ang), (1, reps)), jnp.tile(jnp.sin(ang), (1, reps))


def kernel(x_prompt, x_sample, cache_k, cache_v, state_pool, page_table, c_prompt, c_sample, w_ada, b_ada, norm1_g, norm2_g, w_in, lam_q1, lam_k1, lam_q2, lam_k2, subln_g, w_pool, pool_scale, w_out, w_router, b_router, w_gate, b_gate, w_up, b_up, w_down, b_down, final_g):
    B, S, D = x_prompt.shape
    DB, T, _ = x_sample.shape
    depth = w_ada.shape[0]
    page = cache_k.shape[2]
    past = page_table.shape[1] * page
    d_att = D // 2
    n_heads = d_att // HEAD_W
    d_pool = w_in.shape[2] - 3 * d_att
    n_experts = w_router.shape[2]
    state_len = state_pool.shape[2]
    n_p, n_s = B * S, DB * T
    n_all = n_p + n_s
    tm = min(TOK_TILE, n_p)
    tiles_per_seq = S // tm
    assert depth == 1, "single-layer step"
    assert n_s == tm and S % tm == 0 and T % 8 == 0 and state_len < POOL_HALO <= tm

    cos_p, sin_p = _rope_tables(jnp.arange(S))
    cos_s, sin_s = _rope_tables(jnp.tile(past + jnp.arange(T), DB))

    l = 0
    lam_init = 0.8 - 0.6 * math.exp(-0.3 * l)
    lamp = jnp.zeros((8, LANES), F32)
    for r, vec in enumerate((lam_q1[l], lam_k1[l], lam_q2[l], lam_k2[l])):
        lamp = lamp.at[r, :HEAD_DIM].set(vec.astype(F32))
    subg = subln_g[l].reshape(1, HEAD_W)

    rows_c = -(-(B + DB) // 8) * 8
    c_all = jnp.zeros((rows_c, D), F32).at[:B].set(c_prompt).at[B:B + DB].set(c_sample)
    m_all = _adaln(c_all, w_ada[l], b_ada[l])
    mods_p = [m_all[:B, k * D:(k + 1) * D].reshape(B, 1, D) for k in range(N_ADA)]
    mods_s = [jnp.repeat(m_all[B:B + DB, k * D:(k + 1) * D], T, axis=0).reshape(1, n_s, D) for k in range(N_ADA)]

    w_in_b = _cast_bf16(w_in[l], 256)
    w_out_b = _cast_bf16(w_out[l], 256)
    ng, gw = w_pool.shape[1], w_pool.shape[2]
    w_pool_b = _cast_bf16(w_pool[l].reshape(ng * gw, gw), ng * gw).reshape(ng, gw, gw)
    g1 = norm1_g[l].reshape(1, D)
    g2 = norm2_g[l].reshape(1, D)
    pscale = pool_scale[l].reshape(1, d_pool)
    w_router_p = jnp.zeros((D, LANES), F32).at[:, :n_experts].set(w_router[l])
    b_router_p = jnp.zeros((1, LANES), F32).at[0, :n_experts].set(b_router[l].astype(F32))

    xp = x_prompt.reshape(n_p, D)
    q_p, k_p, v_p, u_p, kb_p, vb_p = _project(xp, mods_p[0], mods_p[1], g1, w_in_b, cos_p, sin_p,
                                              tiles_per_seq=tiles_per_seq, pos_tiles=tiles_per_seq)
    a_p = _attn_prompt(q_p, kb_p, vb_p, lamp, subg.reshape(HEAD_W, 1), batch=B, seq=S, lam_init=lam_init)
    pm_p = _pool_mix(u_p.reshape(B, S, d_pool), w_pool_b, pscale, tile=tm, cur_block0=0, pos0=0,
                     zero_first_halo=True).reshape(n_p, d_pool)

    xs_tok = x_sample.reshape(n_s, D)
    q_s, k_s, v_s, u_s, _, _ = _project(xs_tok, mods_s[0], mods_s[1], g1, w_in_b, cos_s, sin_s,
                                        tiles_per_seq=1, pos_tiles=1)
    a_s = _attn_sample(page_table, q_s, k_s, v_s, cache_k[l], cache_v[l], lamp, subg, t_new=T, lam_init=lam_init)
    u_ext = jnp.concatenate([jnp.zeros((DB, POOL_HALO - state_len, d_pool), F32),
                             state_pool[l].astype(F32), u_s.reshape(DB, T, d_pool)], axis=1)
    pm_s = _pool_mix(u_ext, w_pool_b, pscale, tile=T, cur_block0=POOL_HALO // T, pos0=past,
                     zero_first_halo=False).reshape(n_s, d_pool)

    outs = _mix_out(xp, a_p, pm_p, mods_p[2], mods_p[3], mods_p[4], g2, w_out_b, w_router_p, b_router_p,
                    n_total=n_all, tile0=0, tiles_per_seq=tiles_per_seq, n_experts=n_experts)
    x2, h2, pk4, gt4, tcnt = _mix_out(xs_tok, a_s, pm_s, mods_s[2], mods_s[3], mods_s[4], g2, w_out_b, w_router_p,
                                      b_router_p, n_total=n_all, tile0=n_p // tm, tiles_per_seq=1,
                                      n_experts=n_experts, prev=outs)

    d_ff = w_gate.shape[3]
    tf = min(FF_TILE, d_ff, D)
    tile_cnt = tcnt[:, 0, :n_experts].astype(I32)
    tabs, cnt_flat, base_flat, tail, has, r_max, n_steps = _expert_tables(
        tile_cnt, tile=tm, n_j=d_ff // tf, n_n=D // tf)
    xs_rows = _dispatch(h2, pk4, cnt_flat, base_flat, tail, has, r_max=r_max, n_experts=n_experts)
    ys_rows = _expert_ffn(tabs, xs_rows, w_gate[l], b_gate[l], w_up[l], b_up[l], w_down[l], b_down[l],
                          n_steps=n_steps)
    y_p, y_s = _combine(ys_rows, cnt_flat, base_flat, x2, pk4, gt4, mods_p[5], mods_s[5].reshape(n_s, D),
                        final_g.reshape(1, D), n_prompt=n_p, tiles_per_seq=tiles_per_seq, n_experts=n_experts)

    n_pages_p = S // page
    k_prompt = k_p.reshape(1, B, n_pages_p, page, n_heads, HEAD_W)
    v_prompt = v_p.reshape(1, B, n_pages_p, page, n_heads, HEAD_W)
    pool_prompt = u_p.reshape(B, S, d_pool)[:, S - state_len:][None]
    k_sample = k_s.reshape(1, DB, T, n_heads, HEAD_W)
    v_sample = v_s.reshape(1, DB, T, n_heads, HEAD_W)
    pool_sample = u_ext[:, -state_len:][None]
    return (y_p.reshape(B, S, D), y_s.reshape(DB, T, D), k_prompt, v_prompt, pool_prompt,
            k_sample, v_sample, pool_sample)
```

```python
import functools
import math

import jax
import jax.numpy as jnp
from jax import lax
from jax.experimental import pallas as pl
from jax.experimental.pallas import tpu as pltpu

F32 = jnp.float32
BF16 = jnp.bfloat16
I32 = jnp.int32

HEAD_DIM = 64
HEAD_W = 2 * HEAD_DIM
POOL_WINDOWS = (2, 4, 8, 16)
POOL_HALO = 16
TOP_K = 4
SWIGLU_LIMIT = 7.0
SWIGLU_ALPHA = 1.702
ROPE_THETA = 10000.0
NORM_EPS = 1e-5
N_ADA = 6
LANES = 128
SUBLANES = 8
V7X_VMEM_LIMIT = 58 * 1024 * 1024

TOK_TILE = 256
EXP_TILE = 256
EXP_GROUP = 6
FF_TILE = 512
Q_SCALE = (HEAD_DIM ** -0.5) * math.log2(math.e)
SCORE_BOUND = 60.0


def _cparams(sem, vmem=V7X_VMEM_LIMIT):
    return pltpu.CompilerParams(dimension_semantics=sem, vmem_limit_bytes=vmem)


def _cast_kernel(x_ref, o_ref):
    o_ref[...] = x_ref[...].astype(o_ref.dtype)


def _cast_bf16(w, rows):
    r, c = w.shape
    return pl.pallas_call(
        _cast_kernel,
        out_shape=jax.ShapeDtypeStruct((r, c), BF16),
        grid=(r // rows,),
        in_specs=[pl.BlockSpec((rows, c), lambda i: (i, 0))],
        out_specs=pl.BlockSpec((rows, c), lambda i: (i, 0)),
        compiler_params=_cparams(("arbitrary",)),
        name="cast_bf16",
    )(w)


def _ada_kernel(c_ref, w_ref, b_ref, o_ref):
    c = c_ref[...]
    s = (c * jax.nn.sigmoid(c)).astype(BF16)
    o_ref[...] = jnp.dot(s, w_ref[...].astype(BF16), preferred_element_type=F32) + b_ref[...]


def _adaln(c_all, w_ada, b_ada):
    rows, d = c_all.shape
    n = w_ada.shape[1]
    tn = min(1024, n)
    return pl.pallas_call(
        _ada_kernel,
        out_shape=jax.ShapeDtypeStruct((rows, n), F32),
        grid=(n // tn,),
        in_specs=[pl.BlockSpec((rows, d), lambda j: (0, 0)),
                  pl.BlockSpec((d, tn), lambda j: (0, j)),
                  pl.BlockSpec((1, tn), lambda j: (0, j))],
        out_specs=pl.BlockSpec((rows, tn), lambda j: (0, j)),
        compiler_params=_cparams(("arbitrary",)),
        name="adaln",
    )(c_all, w_ada, b_ada.reshape(1, n))


def _modulated_norm(x, g, shift, scale):
    ms = jnp.mean(x * x, axis=-1, keepdims=True)
    return (x * lax.rsqrt(ms + NORM_EPS) * g) * (1.0 + scale) + shift


def _proj_kernel(x_ref, shift_ref, scale_ref, g_ref, w_ref, cos_ref, sin_ref,
                 q_ref, k_ref, v_ref, u_ref, kb_ref, vb_ref, kn_ref, *, d_att):
    h = _modulated_norm(x_ref[...], g_ref[...], shift_ref[0], scale_ref[0]).astype(BF16)
    cos = cos_ref[...]
    sin = sin_ref[...]
    lane = lax.broadcasted_iota(I32, cos.shape, 1)
    first_half = (lane & (HEAD_DIM - 1)) < (HEAD_DIM // 2)
    head_lane = lax.broadcasted_iota(I32, (1, LANES), 1)

    def rope(z):
        rot = jnp.where(first_half, -pltpu.roll(z, LANES - HEAD_DIM // 2, 1), pltpu.roll(z, HEAD_DIM // 2, 1))
        return z * cos + rot * sin

    zq = jnp.dot(h, w_ref[:, 0:d_att], preferred_element_type=F32)
    zk = jnp.dot(h, w_ref[:, d_att:2 * d_att], preferred_element_type=F32)
    k_norm2 = jnp.zeros((1, LANES), F32)
    for hh in range(d_att // HEAD_W):
        sl = slice(hh * HEAD_W, (hh + 1) * HEAD_W)
        q_ref[:, sl] = rope(zq[:, sl]) * Q_SCALE
        kr = rope(zk[:, sl])
        k_ref[:, sl] = kr
        kb_ref[:, sl] = kr.astype(BF16)
        n2 = jnp.max(jnp.sum(kr * kr, axis=1, keepdims=True), axis=0, keepdims=True)
        k_norm2 = jnp.where(head_lane == hh, n2, k_norm2)
    kn_ref[0] = jnp.broadcast_to(k_norm2, kn_ref.shape[1:])
    zv = jnp.dot(h, w_ref[:, 2 * d_att:3 * d_att], preferred_element_type=F32)
    v_ref[...] = zv
    vb_ref[...] = zv.astype(BF16)
    u_ref[...] = jnp.dot(h, w_ref[:, 3 * d_att:], preferred_element_type=F32)


def _project(x, shift, scale, g, w_in_b, cos, sin, *, tiles_per_seq, pos_tiles):
    n, d = x.shape
    tm = min(TOK_TILE, n)
    d_in = w_in_b.shape[1]
    d_att = (d // 2)
    d_pool = d_in - 3 * d_att
    mod_rows = shift.shape[1]
    mod_spec = pl.BlockSpec((1, mod_rows, d), lambda i: (i // tiles_per_seq, 0, 0))
    tok = lambda c: pl.BlockSpec((tm, c), lambda i: (i, 0))
    return pl.pallas_call(
        functools.partial(_proj_kernel, d_att=d_att),
        out_shape=(jax.ShapeDtypeStruct((n, d_att), F32), jax.ShapeDtypeStruct((n, d_att), F32),
                   jax.ShapeDtypeStruct((n, d_att), F32), jax.ShapeDtypeStruct((n, d_pool), F32),
                   jax.ShapeDtypeStruct((n, d_att), BF16), jax.ShapeDtypeStruct((n, d_att), BF16),
                   jax.ShapeDtypeStruct((n // tm, SUBLANES, LANES), F32)),
        grid=(n // tm,),
        in_specs=[tok(d), mod_spec, mod_spec,
                  pl.BlockSpec((1, d), lambda i: (0, 0)),
                  pl.BlockSpec((d, d_in), lambda i: (0, 0)),
                  pl.BlockSpec((tm, LANES), lambda i: (i % pos_tiles, 0)),
                  pl.BlockSpec((tm, LANES), lambda i: (i % pos_tiles, 0))],
        out_specs=(tok(d_att), tok(d_att), tok(d_att), tok(d_pool), tok(d_att), tok(d_att),
                   pl.BlockSpec((1, SUBLANES, LANES), lambda i: (i, 0, 0))),
        compiler_params=_cparams(("arbitrary",)),
        name="in_proj",
    )(x, shift, scale, g, w_in_b, cos, sin)


def _lambda_value(lam_ref, lam_init):
    lp = lam_ref[...]
    a = jnp.sum(lp[0:1] * lp[1:2], axis=1, keepdims=True)
    b = jnp.sum(lp[2:3] * lp[3:4], axis=1, keepdims=True)
    return jnp.exp(a) - jnp.exp(b) + lam_init


def _stack_maps(q):
    lane = lax.broadcasted_iota(I32, q.shape, 1)
    q1 = jnp.where(lane < HEAD_DIM, q, 0.0)
    q2 = jnp.where(lane >= HEAD_DIM, q, 0.0)
    return jnp.concatenate([q1, q2], axis=0).astype(BF16)


_NT = (((1,), (1,)), ((), ()))


_TN = (((0,), (0,)), ((), ()))


def _attn_prompt_kernel(lam_ref, g_ref, kn_ref, q_ref, k_ref, k2_ref, v_ref, o_ref, m_ref, l_ref, acc_ref,
                        *, tq, tk, lam_init):
    qi = pl.program_id(2)
    heads = q_ref.shape[1] // HEAD_W
    qqs = [_stack_maps(q_ref[:, hh * HEAD_W:(hh + 1) * HEAD_W]) for hh in range(heads)]
    m_ref[...] = jnp.full(m_ref.shape, -jnp.inf, F32)
    l_ref[...] = jnp.zeros(l_ref.shape, F32)
    acc_ref[...] = jnp.zeros(acc_ref.shape, F32)

    lane = lax.broadcasted_iota(I32, (1, LANES), 1)
    kn = kn_ref[0, 0:1, :]
    worst = jnp.zeros((1, 1), F32)
    for hh in range(heads):
        qf = qqs[hh].astype(F32)
        qn2 = jnp.max(jnp.sum(qf * qf, axis=1, keepdims=True), axis=0, keepdims=True)
        kn2 = jnp.sum(jnp.where(lane == pl.program_id(1) * heads + hh, kn, 0.0), axis=1, keepdims=True)
        worst = jnp.maximum(worst, qn2 * kn2)
    bounded = worst[0, 0] <= SCORE_BOUND * SCORE_BOUND

    def run(fixed_reference):
        def block(j, masked):
            rows = pl.ds(pl.multiple_of(j * tk, tk), tk)
            for hh in range(heads):
                cols = slice(hh * HEAD_W, (hh + 1) * HEAD_W)

                def scores(kref):
                    st = lax.dot_general(kref[rows, cols], qqs[hh], _NT, preferred_element_type=F32)
                    if masked:
                        kpos = j * tk + lax.broadcasted_iota(I32, st.shape, 0)
                        c = lax.broadcasted_iota(I32, st.shape, 1)
                        qpos = qi * tq + jnp.where(c >= tq, c - tq, c)
                        st = jnp.where(kpos <= qpos, st, -jnp.inf)
                    return st

                if fixed_reference:
                    p = jnp.exp2(scores(k_ref))
                    l_ref[hh] = l_ref[hh] + jnp.sum(p, axis=0, keepdims=True)
                    acc_ref[hh] = acc_ref[hh] + lax.dot_general(v_ref[rows, cols], p.astype(BF16), _TN,
                                                                preferred_element_type=F32)
                else:
                    m_prev = m_ref[hh]
                    m_new = jnp.maximum(m_prev, jnp.max(scores(k_ref), axis=0, keepdims=True))
                    p = jnp.exp2(scores(k2_ref) - m_new)
                    alpha = jnp.exp2(m_prev - m_new)
                    l_ref[hh] = alpha * l_ref[hh] + jnp.sum(p, axis=0, keepdims=True)
                    pv = lax.dot_general(v_ref[rows, cols], p.astype(BF16), _TN, preferred_element_type=F32)
                    acc_ref[hh] = acc_ref[hh] * alpha + pv
                    m_ref[hh] = m_new

        n_full = (qi * tq) // tk

        def body(j, carry):
            block(j, False)
            return carry

        lax.fori_loop(0, n_full, body, 0)
        block(n_full, True)

    @pl.when(bounded)
    def _():
        run(True)

    @pl.when(jnp.logical_not(bounded))
    def _():
        run(False)

    lam = _lambda_value(lam_ref, lam_init)
    for hh in range(heads):
        l = l_ref[hh]
        acc = acc_ref[hh]
        o = acc[:, :tq] / l[:, :tq] - lam * (acc[:, tq:] / l[:, tq:])
        ms = jnp.mean(o * o, axis=0, keepdims=True)
        a = o * lax.rsqrt(ms + NORM_EPS) * g_ref[...] * (1.0 - lam_init)
        o_ref[:, hh * HEAD_W:(hh + 1) * HEAD_W] = a.T.astype(o_ref.dtype)


def _attn_prompt(q, kb, vb, lamp, g_col, k_norm2, *, batch, seq, lam_init):
    n, d_att = q.shape
    nh = d_att // HEAD_W
    hp = 4 if nh % 4 == 0 else 1
    tq = min(256, seq)
    tk = min(512, seq)
    nq = seq // tq
    return pl.pallas_call(
        functools.partial(_attn_prompt_kernel, tq=tq, tk=tk, lam_init=lam_init),
        out_shape=jax.ShapeDtypeStruct((n, d_att), BF16),
        grid=(batch, nh // hp, nq),
        in_specs=[pl.BlockSpec((8, LANES), lambda b, h, i: (0, 0)),
                  pl.BlockSpec((HEAD_W, 1), lambda b, h, i: (0, 0)),
                  pl.BlockSpec((1, SUBLANES, LANES), lambda b, h, i: (b, 0, 0)),
                  pl.BlockSpec((tq, hp * HEAD_W), lambda b, h, i: (b * nq + i, h)),
                  pl.BlockSpec((seq, hp * HEAD_W), lambda b, h, i: (b, h)),
                  pl.BlockSpec((seq, hp * HEAD_W), lambda b, h, i: (b, h)),
                  pl.BlockSpec((seq, hp * HEAD_W), lambda b, h, i: (b, h))],
        out_specs=pl.BlockSpec((tq, hp * HEAD_W), lambda b, h, i: (b * nq + i, h)),
        scratch_shapes=[pltpu.VMEM((hp, 1, 2 * tq), F32), pltpu.VMEM((hp, 1, 2 * tq), F32),
                        pltpu.VMEM((hp, HEAD_W, 2 * tq), F32)],
        compiler_params=_cparams(("arbitrary", "arbitrary", "arbitrary")),
        name="attn_prompt",
    )(lamp, g_col, k_norm2, q, kb, kb, vb)


def _attn_sample_kernel(pt_ref, lam_ref, g_ref, q_ref, kn_ref, vn_ref, *rest, n_heads, pages, t_new, lam_init):
    k_pages = rest[:pages]
    v_pages = rest[pages:2 * pages]
    o_ref = rest[2 * pages]
    m_ref, l_ref, acc_ref = rest[2 * pages + 1:]
    c = pl.program_id(1)
    rows_h = 2 * t_new
    assert n_heads == SUBLANES and n_heads * rows_h == LANES

    @pl.when(c == 0)
    def _():
        m_ref[...] = jnp.full(m_ref.shape, -jnp.inf, F32)
        l_ref[...] = jnp.zeros(l_ref.shape, F32)
        acc_ref[...] = jnp.zeros(acc_ref.shape, F32)

    q = q_ref[...]
    q_all = jnp.concatenate([_stack_maps(q[:, hh * HEAD_W:(hh + 1) * HEAD_W]) for hh in range(n_heads)], axis=0)
    sub = lax.broadcasted_iota(I32, (SUBLANES, LANES), 0)
    lane = lax.broadcasted_iota(I32, (SUBLANES, LANES), 1)
    own_head = (lane // rows_h) == sub

    def update(k_rows, v_rows, valid):
        n_pos = k_rows.shape[0] // n_heads
        r = lax.dot_general(k_rows.astype(BF16), q_all, _NT, preferred_element_type=F32)
        r = r.reshape(n_pos, n_heads, LANES)
        if valid is not None:
            r = jnp.where(valid, r, -jnp.inf)
        m_old = m_ref[...]
        m_new = jnp.maximum(m_old, jnp.max(r, axis=0))
        p = jnp.exp2(r - m_new[None])
        alpha = jnp.exp2(m_old - m_new)
        l_ref[...] = alpha * l_ref[...] + jnp.sum(p, axis=0)
        p_own = jnp.where(own_head[None], p, 0.0).reshape(n_pos * n_heads, LANES).astype(BF16)
        alpha_row = jnp.sum(jnp.where(own_head, alpha, 0.0), axis=0, keepdims=True)
        pv = lax.dot_general(v_rows.astype(BF16), p_own, _TN, preferred_element_type=F32)
        acc_ref[...] = acc_ref[...] * alpha_row + pv
        m_ref[...] = m_new

    for i in range(pages):
        update(k_pages[i][0], v_pages[i][0], None)

    @pl.when(c == pl.num_programs(1) - 1)
    def _():
        t_key = lax.broadcasted_iota(I32, (t_new, n_heads, LANES), 0)
        t_query = lax.broadcasted_iota(I32, (t_new, n_heads, LANES), 2) % t_new
        update(kn_ref[0], vn_ref[0], t_key <= t_query)
        l_row = jnp.sum(jnp.where(own_head, l_ref[...], 0.0), axis=0, keepdims=True)
        o_all = (acc_ref[...] / l_row).T
        lam = _lambda_value(lam_ref, lam_init)
        for hh in range(n_heads):
            rows = o_all[hh * rows_h:(hh + 1) * rows_h]
            o = rows[:t_new] - lam * rows[t_new:]
            ms = jnp.mean(o * o, axis=1, keepdims=True)
            o_ref[:, hh * HEAD_W:(hh + 1) * HEAD_W] = o * lax.rsqrt(ms + NORM_EPS) * g_ref[...] * (1.0 - lam_init)


def _attn_sample(page_table, q, k_new, v_new, cache_k, cache_v, lamp, g, *, t_new, lam_init):
    n, d_att = q.shape
    nh = d_att // HEAD_W
    db, n_pages = page_table.shape
    n_pool, page = cache_k.shape[0], cache_k.shape[1]
    pages = min(8, n_pages)
    ck = cache_k.reshape(n_pool, page * nh, HEAD_W)
    cv = cache_v.reshape(n_pool, page * nh, HEAD_W)

    def page_spec(i):
        return pl.BlockSpec((1, page * nh, HEAD_W), lambda b, c, pt: (pt[b, c * pages + i], 0, 0))

    tok = pl.BlockSpec((t_new, d_att), lambda b, c, pt: (b, 0))
    new_rows = pl.BlockSpec((1, t_new * nh, HEAD_W), lambda b, c, pt: (b, 0, 0))
    k_new = k_new.reshape(db, t_new, nh, HEAD_W).reshape(db, t_new * nh, HEAD_W)
    v_new = v_new.reshape(db, t_new, nh, HEAD_W).reshape(db, t_new * nh, HEAD_W)
    grid_spec = pltpu.PrefetchScalarGridSpec(
        num_scalar_prefetch=1,
        grid=(db, n_pages // pages),
        in_specs=[pl.BlockSpec((8, LANES), lambda b, c, pt: (0, 0)),
                  pl.BlockSpec((1, HEAD_W), lambda b, c, pt: (0, 0)),
                  tok, new_rows, new_rows]
                 + [page_spec(i) for i in range(pages)] + [page_spec(i) for i in range(pages)],
        out_specs=tok,
        scratch_shapes=[pltpu.VMEM((nh, LANES), F32), pltpu.VMEM((nh, LANES), F32),
                        pltpu.VMEM((HEAD_W, LANES), F32)],
    )
    return pl.pallas_call(
        functools.partial(_attn_sample_kernel, n_heads=nh, pages=pages, t_new=t_new, lam_init=lam_init),
        out_shape=jax.ShapeDtypeStruct((n, d_att), F32),
        grid_spec=grid_spec,
        compiler_params=_cparams(("arbitrary", "arbitrary")),
        name="attn_sample",
    )(page_table, lamp, g, q, k_new, v_new, *([ck] * pages), *([cv] * pages))


def _pool_kernel(halo_ref, cur_ref, w_ref, scale_ref, o_ref, ext_ref, *, pos0, tile_pos, zero_first_halo):
    i = pl.program_id(1)
    t = cur_ref.shape[1]
    halo = halo_ref[0]
    if zero_first_halo:
        halo = jnp.where(i == 0, 0.0, halo)
    cur = cur_ref[0]
    ext_ref[0:POOL_HALO, :] = halo
    ext_ref[POOL_HALO:POOL_HALO + t, :] = cur
    pos = pos0 + i * tile_pos + lax.broadcasted_iota(I32, (t, 1), 0)
    gw = cur.shape[1] // len(POOL_WINDOWS)
    for gi, w in enumerate(POOL_WINDOWS):
        cols = slice(gi * gw, (gi + 1) * gw)
        total = cur[:, cols]
        for j in range(1, w):
            total = total + ext_ref[POOL_HALO - j:POOL_HALO - j + t, cols]
        cnt = jnp.minimum(pos + 1, w).astype(F32)
        dlt = total / cnt - cur[:, cols]
        y = jnp.dot(dlt.astype(BF16), w_ref[gi], preferred_element_type=F32)
        o_ref[0, :, cols] = (y * scale_ref[:, cols]).astype(o_ref.dtype)


def _pool_mix(u3, w_pool_b, scale, *, tile, cur_block0, pos0, zero_first_halo):
    b, rows, c = u3.shape
    n_tiles = (rows - cur_block0 * tile) // tile
    ng, gw = w_pool_b.shape[0], w_pool_b.shape[1]
    assert (cur_block0 * tile) % POOL_HALO == 0 and (tile % POOL_HALO == 0 or n_tiles == 1)

    def halo_map(bi, i):
        return (bi, jnp.maximum(((cur_block0 + i) * tile) // POOL_HALO - 1, 0), 0)

    return pl.pallas_call(
        functools.partial(_pool_kernel, pos0=pos0, tile_pos=tile, zero_first_halo=zero_first_halo),
        out_shape=jax.ShapeDtypeStruct((b, n_tiles * tile, c), BF16),
        grid=(b, n_tiles),
        in_specs=[pl.BlockSpec((1, POOL_HALO, c), halo_map),
                  pl.BlockSpec((1, tile, c), lambda bi, i: (bi, cur_block0 + i, 0)),
                  pl.BlockSpec((ng, gw, gw), lambda bi, i: (0, 0, 0)),
                  pl.BlockSpec((1, c), lambda bi, i: (0, 0))],
        out_specs=pl.BlockSpec((1, tile, c), lambda bi, i: (bi, i, 0)),
        scratch_shapes=[pltpu.VMEM((POOL_HALO + tile, c), F32)],
        compiler_params=_cparams(("arbitrary", "arbitrary")),
        name="pool_mix",
    )(u3, u3, w_pool_b, scale)


def _split_bf16(x):
    hi = x.astype(BF16)
    lo = (x - hi.astype(F32)).astype(BF16)
    return hi, lo


def _round_up_f32(x, m):
    return jnp.floor((x + (m - 1.0)) * (1.0 / m)) * m


def _mix_out_kernel(*refs, n_experts, aliased):
    (x_ref, a_ref, pm_ref, gate_ref, shift_ref, scale_ref, g_ref, wo_ref, wr_ref, br_ref) = refs[:10]
    x2_ref, h2_ref, pk_ref, gt_ref, tc_ref = refs[10 + aliased:]
    d_att = a_ref.shape[1]
    mix = (jnp.dot(a_ref[...].astype(BF16), wo_ref[0:d_att, :], preferred_element_type=F32)
           + jnp.dot(pm_ref[...], wo_ref[d_att:, :], preferred_element_type=F32))
    x2 = x_ref[...] + gate_ref[0] * mix
    x2_ref[...] = x2
    h2 = _modulated_norm(x2, g_ref[...], shift_ref[0], scale_ref[0])
    h2_ref[...] = h2.astype(BF16)
    hh, hl = _split_bf16(h2)
    wh, wl = _split_bf16(wr_ref[...])
    logits = (jnp.dot(hh, wh, preferred_element_type=F32) + jnp.dot(hl, wh, preferred_element_type=F32)
              + jnp.dot(hh, wl, preferred_element_type=F32)) + br_ref[...]
    t = logits.shape[0]
    lane = lax.broadcasted_iota(I32, logits.shape, 1)
    lanef = lane.astype(F32)
    work = jnp.where(lane < n_experts, logits, -jnp.inf)
    vals, ids = [], []
    for _ in range(TOP_K):
        mx = jnp.max(work, axis=1, keepdims=True)
        ix = jnp.min(jnp.where(work == mx, lanef, float(LANES)), axis=1, keepdims=True)
        vals.append(mx)
        ids.append(ix)
        work = jnp.where(lanef == ix, -jnp.inf, work)
    es = [jnp.exp(v - vals[0]) for v in vals]
    den = es[0]
    for e in es[1:]:
        den = den + e
    sel = jnp.zeros(logits.shape, F32)
    for k in range(TOP_K):
        sel = jnp.where(lanef == ids[k], 1.0, sel)
    r = lax.broadcasted_iota(I32, (t, t), 0)
    c = lax.broadcasted_iota(I32, (t, t), 1)
    earlier = jnp.where(c < r, 1.0, 0.0).astype(BF16)
    local_rank = jnp.dot(earlier, sel.astype(BF16), preferred_element_type=F32)
    cnt = jnp.sum(sel, axis=0, keepdims=True)
    cnt8 = jnp.broadcast_to(_round_up_f32(cnt, float(SUBLANES)), (SUBLANES, LANES))
    er = lax.broadcasted_iota(I32, (LANES, LANES), 0)
    ec = lax.broadcasted_iota(I32, (LANES, LANES), 1)
    before = jnp.where(er < ec, 1.0, 0.0).astype(BF16)
    chunk_off = jnp.dot(cnt8.astype(BF16), before, preferred_element_type=F32)[0:1, :]
    pos = local_rank + chunk_off
    pk_out = jnp.zeros(logits.shape, F32)
    gt_out = jnp.zeros(logits.shape, F32)
    for k in range(TOP_K):
        pk = jnp.sum(jnp.where(lanef == ids[k], pos, 0.0), axis=1, keepdims=True)
        pk_out = jnp.where(lane == k, pk, pk_out)
        gt_out = jnp.where(lane == k, es[k] / den, gt_out)
    pk_ref[...] = pk_out
    gt_ref[...] = gt_out
    tc_ref[0] = jnp.broadcast_to(cnt, (SUBLANES, LANES))


def _mix_out(x, a, pm, gate, shift, scale, g, w_out_b, w_router_p, b_router_p, *, n_total, tile0,
             tiles_per_seq, n_experts, prev=None):
    n, d = x.shape
    tm = min(TOK_TILE, n)
    d_att = a.shape[1]
    mod_rows = gate.shape[1]
    mod_spec = pl.BlockSpec((1, mod_rows, d), lambda i: (i // tiles_per_seq, 0, 0))
    tok = lambda c: pl.BlockSpec((tm, c), lambda i: (i, 0))
    out_tok = lambda c: pl.BlockSpec((tm, c), lambda i: (tile0 + i, 0))
    out_shape = (jax.ShapeDtypeStruct((n_total, d), F32), jax.ShapeDtypeStruct((n_total, d), BF16),
                 jax.ShapeDtypeStruct((n_total, LANES), F32), jax.ShapeDtypeStruct((n_total, LANES), F32),
                 jax.ShapeDtypeStruct((n_total // tm, SUBLANES, LANES), F32))
    in_specs = [tok(d), tok(d_att), tok(pm.shape[1]), mod_spec, mod_spec, mod_spec,
                pl.BlockSpec((1, d), lambda i: (0, 0)),
                pl.BlockSpec(w_out_b.shape, lambda i: (0, 0)),
                pl.BlockSpec(w_router_p.shape, lambda i: (0, 0)),
                pl.BlockSpec((1, LANES), lambda i: (0, 0))]
    args = [x, a, pm, gate, shift, scale, g, w_out_b, w_router_p, b_router_p]
    aliases = {}
    n_alias = 0
    if prev is not None:
        n_alias = len(prev)
        in_specs += [pl.BlockSpec(memory_space=pl.ANY)] * n_alias
        aliases = {len(args) + k: k for k in range(n_alias)}
        args += list(prev)
    return pl.pallas_call(
        functools.partial(_mix_out_kernel, n_experts=n_experts, aliased=n_alias),
        out_shape=out_shape,
        grid=(n // tm,),
        in_specs=in_specs,
        out_specs=(out_tok(d), out_tok(d), out_tok(LANES), out_tok(LANES),
                   pl.BlockSpec((1, SUBLANES, LANES), lambda i: (tile0 + i, 0, 0))),
        input_output_aliases=aliases,
        compiler_params=_cparams(("arbitrary",)),
        name="mix_out_router",
    )(*args)


def _chunk_sizes(tile):
    sizes = []
    s = tile
    while s >= SUBLANES:
        sizes.append(s)
        s //= 2
    return sizes


def _for_each_chunk_piece(cnt_ref, base_ref, i, n_experts, tile, fn):
    def per_expert(e, off):
        l8 = (cnt_ref[i * n_experts + e] + (SUBLANES - 1)) & (-SUBLANES)
        dst0 = base_ref[i * n_experts + e]
        done = jnp.int32(0)
        for sz in _chunk_sizes(tile):
            take = l8 & sz

            @pl.when(take != 0)
            def _(done=done, sz=sz):
                fn(pl.multiple_of(off + done, SUBLANES), pl.multiple_of(dst0 + done, SUBLANES), sz)

            done = done + take
        return off + l8

    return lax.fori_loop(0, n_experts, per_expert, jnp.int32(0))


def _one_hot_rows(pk, rows, weights=None):
    t = pk.shape[0]
    col = lax.broadcasted_iota(I32, (t, rows), 1).astype(F32)
    out = jnp.zeros((t, rows), F32)
    for k in range(TOP_K):
        w = 1.0 if weights is None else weights[:, k:k + 1]
        out = jnp.where(col == pk[:, k:k + 1], w, out)
    return out.astype(BF16)


def _dispatch_kernel(cnt_ref, base_ref, tail_ref, has_ref, h_ref, pk_ref, xs_hbm, sorted_ref, zero_ref, sem,
                     zero_sem, *, n_experts, tile):
    i = pl.program_id(0)
    last = pl.num_programs(0) - 1

    def zero_copy(e):
        return pltpu.make_async_copy(
            zero_ref, xs_hbm.at[pl.ds(pl.multiple_of(tail_ref[e], EXP_TILE), EXP_TILE)], zero_sem)

    @pl.when(i == 0)
    def _():
        zero_ref[...] = jnp.zeros(zero_ref.shape, F32)
        for e in range(n_experts):
            @pl.when(has_ref[e] == 1)
            def _(e=e):
                zero_copy(e).start()
        for e in range(n_experts):
            @pl.when(has_ref[e] == 1)
            def _(e=e):
                zero_copy(e).wait()

    def copy(step, src, dst, sz):
        slot = step & 1
        return pltpu.make_async_copy(sorted_ref.at[slot, pl.ds(src, sz)], xs_hbm.at[pl.ds(dst, sz)], sem.at[slot])

    onehot = _one_hot_rows(pk_ref[...], sorted_ref.shape[1])
    sorted_ref[i & 1] = lax.dot_general(onehot, h_ref[...], _TN, preferred_element_type=F32)
    _for_each_chunk_piece(cnt_ref, base_ref, i, n_experts, tile, lambda s, d, sz: copy(i, s, d, sz).start())

    @pl.when(i > 0)
    def _():
        _for_each_chunk_piece(cnt_ref, base_ref, i - 1, n_experts, tile,
                              lambda s, d, sz: copy(i - 1, s, d, sz).wait())

    @pl.when(i == last)
    def _():
        _for_each_chunk_piece(cnt_ref, base_ref, i, n_experts, tile, lambda s, d, sz: copy(i, s, d, sz).wait())


def _sorted_rows(tile, n_experts):
    rows = tile * TOP_K + n_experts * (SUBLANES - 1)
    return -(-rows // LANES) * LANES


def _dispatch(h2, pk4, cnt_flat, base_flat, tail_rows, has_tile, *, r_max, n_experts):
    n, d = h2.shape
    tile = min(TOK_TILE, n)
    grid_spec = pltpu.PrefetchScalarGridSpec(
        num_scalar_prefetch=4,
        grid=(n // tile,),
        in_specs=[pl.BlockSpec((tile, d), lambda i, *_: (i, 0)),
                  pl.BlockSpec((tile, LANES), lambda i, *_: (i, 0))],
        out_specs=pl.BlockSpec(memory_space=pl.ANY),
        scratch_shapes=[pltpu.VMEM((2, _sorted_rows(tile, n_experts), d), F32),
                        pltpu.VMEM((EXP_TILE, d), F32), pltpu.SemaphoreType.DMA((2,)), pltpu.SemaphoreType.DMA],
    )
    return pl.pallas_call(
        functools.partial(_dispatch_kernel, n_experts=n_experts, tile=tile),
        out_shape=jax.ShapeDtypeStruct((r_max, d), F32),
        grid_spec=grid_spec,
        compiler_params=_cparams(("arbitrary",)),
        name="dispatch_rows",
    )(cnt_flat, base_flat, tail_rows, has_tile, h2, pk4)


def _expert_kernel(st_e, st_j, st_n, g_t0, g_gs, g_valid,
                   xs_hbm, wg_ref, wu_ref, wd_ref, bg_ref, bu_ref, bd_ref, ys_hbm,
                   xbuf, actbuf, wgb, wub, wdb, stage_in, stage_out, pending, sem_in, sem_out,
                   *, n_j, n_n, tm, tf):
    s = pl.program_id(0)
    steps = n_j + n_n
    q = s // steps
    ph = s - q * steps
    valid = g_valid[q] == 1
    gs = g_gs[q]
    row0 = g_t0[q] * tm

    @pl.when(s == 0)
    def _():
        pending[0] = 0
        pending[1] = 0

    @pl.when(jnp.logical_and(valid, ph == 0))
    def _():
        def in_copy(i):
            slot = i & 1
            return pltpu.make_async_copy(xs_hbm.at[pl.ds(pl.multiple_of(row0 + i * tm, tm), tm)],
                                         stage_in.at[slot], sem_in.at[slot])

        @pl.when(gs >= 1)
        def _():
            in_copy(0).start()

        def load(i, carry):
            @pl.when(i + 1 < gs)
            def _():
                in_copy(i + 1).start()

            in_copy(i).wait()
            xbuf[pl.ds(pl.multiple_of(i * tm, tm), tm), :] = stage_in[i & 1].astype(BF16)
            return carry

        lax.fori_loop(0, gs, load, 0)

    n_pairs = gs // 2
    has_tail = (gs & 1) == 1
    tail_row = pl.multiple_of(n_pairs * 2 * tm, tm)

    @pl.when(jnp.logical_and(valid, ph < n_j))
    def _():
        wgb[...] = wg_ref[0].astype(BF16)
        wub[...] = wu_ref[0].astype(BF16)

        def tile(r0, size):
            rows = pl.ds(r0, size)
            x = xbuf[rows, :]
            gate = jnp.dot(x, wgb[...], preferred_element_type=F32) + bg_ref[0]
            up = jnp.dot(x, wub[...], preferred_element_type=F32) + bu_ref[0]
            gate = jnp.minimum(gate, SWIGLU_LIMIT)
            up = jnp.clip(up, -SWIGLU_LIMIT, SWIGLU_LIMIT)
            act = (up + 1.0) * (gate * jax.nn.sigmoid(SWIGLU_ALPHA * gate))
            actbuf[ph, rows, :] = act.astype(BF16)

        def pair(i, carry):
            tile(pl.multiple_of(i * 2 * tm, 2 * tm), 2 * tm)
            return carry

        lax.fori_loop(0, n_pairs, pair, 0)

        @pl.when(has_tail)
        def _():
            tile(tail_row, tm)

    @pl.when(jnp.logical_and(valid, ph >= n_j))
    def _():
        wdb[...] = wd_ref[0].astype(BF16)
        col0 = pl.multiple_of((ph - n_j) * tf, tf)

        def out_copy(slot, r0, size):
            return pltpu.make_async_copy(
                stage_out.at[slot, pl.ds(0, size)],
                ys_hbm.at[pl.ds(pl.multiple_of(row0 + r0, tm), size), pl.ds(col0, tf)],
                sem_out.at[slot])

        def wait_slot(slot):
            for n_tiles in (1, 2):
                @pl.when(pending[slot] == n_tiles)
                def _(n_tiles=n_tiles):
                    out_copy(slot, 0, n_tiles * tm).wait()
                    pending[slot] = 0

        def tile(r0, size, slot):
            rows = pl.ds(r0, size)
            wait_slot(slot)
            y = bd_ref[0] + jnp.dot(actbuf[0, rows, :], wdb[0:tf, :], preferred_element_type=F32)
            for j in range(1, n_j):
                y = y + jnp.dot(actbuf[j, rows, :], wdb[j * tf:(j + 1) * tf, :], preferred_element_type=F32)
            stage_out[slot, 0:size] = y
            out_copy(slot, r0, size).start()
            pending[slot] = size // tm

        def pair(i, carry):
            tile(pl.multiple_of(i * 2 * tm, 2 * tm), 2 * tm, i & 1)
            return carry

        lax.fori_loop(0, n_pairs, pair, 0)

        @pl.when(has_tail)
        def _():
            tile(tail_row, tm, n_pairs & 1)

        @pl.when(s == pl.num_programs(0) - 1)
        def _():
            wait_slot(0)
            wait_slot(1)


def _expert_ffn(tables, xs, w_gate, b_gate, w_up, b_up, w_down, b_down, *, n_steps):
    st_e, st_j, st_n, g_t0, g_gs, g_valid = tables
    r_max, d = xs.shape
    n_exp, _, d_ff = w_gate.shape
    tf = min(FF_TILE, d_ff, d)
    n_j = d_ff // tf
    n_n = d // tf
    tm = EXP_TILE
    rows_g = EXP_GROUP * tm
    grid_spec = pltpu.PrefetchScalarGridSpec(
        num_scalar_prefetch=6,
        grid=(n_steps,),
        in_specs=[pl.BlockSpec(memory_space=pl.ANY),
                  pl.BlockSpec((1, d, tf), lambda s, e, j, n, *_: (e[s], 0, j[s])),
                  pl.BlockSpec((1, d, tf), lambda s, e, j, n, *_: (e[s], 0, j[s])),
                  pl.BlockSpec((1, d_ff, tf), lambda s, e, j, n, *_: (e[s], 0, n[s])),
                  pl.BlockSpec((1, 1, tf), lambda s, e, j, n, *_: (e[s], 0, j[s])),
                  pl.BlockSpec((1, 1, tf), lambda s, e, j, n, *_: (e[s], 0, j[s])),
                  pl.BlockSpec((1, 1, tf), lambda s, e, j, n, *_: (e[s], 0, n[s]))],
        out_specs=pl.BlockSpec(memory_space=pl.ANY),
        scratch_shapes=[pltpu.VMEM((rows_g, d), BF16),
                        pltpu.VMEM((n_j, rows_g, tf), BF16),
                        pltpu.VMEM((d, tf), BF16), pltpu.VMEM((d, tf), BF16), pltpu.VMEM((d_ff, tf), BF16),
                        pltpu.VMEM((2, tm, d), F32), pltpu.VMEM((2, 2 * tm, tf), F32), pltpu.SMEM((2,), I32),
                        pltpu.SemaphoreType.DMA((2,)), pltpu.SemaphoreType.DMA((2,))],
    )
    return pl.pallas_call(
        functools.partial(_expert_kernel, n_j=n_j, n_n=n_n, tm=tm, tf=tf),
        out_shape=jax.ShapeDtypeStruct((r_max, d), F32),
        grid_spec=grid_spec,
        compiler_params=_cparams(("arbitrary",)),
        name="expert_ffn",
    )(st_e, st_j, st_n, g_t0, g_gs, g_valid, xs, w_gate, w_up, w_down,
      b_gate.reshape(n_exp, 1, d_ff), b_up.reshape(n_exp, 1, d_ff), b_down.reshape(n_exp, 1, d))


def _combine_kernel(cnt_ref, base_ref, ys_hbm, x2_ref, pk_ref, gt_ref, gate_p_ref, gate_s_ref, g_ref,
                    yp_ref, ysm_ref, rows_ref, sem, *, n_experts, tile, n_prompt_tiles):
    i = pl.program_id(0)
    last = pl.num_programs(0) - 1

    def copy(step, dst, src, sz):
        slot = step & 1
        return pltpu.make_async_copy(ys_hbm.at[pl.ds(src, sz)], rows_ref.at[slot, pl.ds(dst, sz)], sem.at[slot])

    def fetch(step):
        _for_each_chunk_piece(cnt_ref, base_ref, step, n_experts, tile,
                              lambda s, d, sz: copy(step, s, d, sz).start())

    @pl.when(i == 0)
    def _():
        rows_ref[...] = jnp.zeros(rows_ref.shape, F32)
        fetch(i)

    @pl.when(i < last)
    def _():
        fetch(i + 1)

    _for_each_chunk_piece(cnt_ref, base_ref, i, n_experts, tile, lambda s, d, sz: copy(i, s, d, sz).wait())
    weights = _one_hot_rows(pk_ref[...], rows_ref.shape[1], gt_ref[...])
    y = jnp.dot(weights, rows_ref[i & 1].astype(BF16), preferred_element_type=F32)
    is_prompt = i < n_prompt_tiles
    gate = jnp.where(is_prompt, gate_p_ref[0], gate_s_ref[...])
    x3 = x2_ref[...] + gate * y
    ms = jnp.mean(x3 * x3, axis=-1, keepdims=True)
    out = x3 * lax.rsqrt(ms + NORM_EPS) * g_ref[...]

    @pl.when(is_prompt)
    def _():
        yp_ref[...] = out

    @pl.when(jnp.logical_not(is_prompt))
    def _():
        ysm_ref[...] = out


def _combine(ys, cnt_flat, base_flat, x2, pk4, gt, gate_p, gate_s, final_g, *, n_prompt, tiles_per_seq, n_experts):
    n, d = x2.shape
    tile = min(TOK_TILE, n_prompt)
    n_s = n - n_prompt
    assert n_s == tile and n_prompt % tile == 0
    npt = n_prompt // tile
    grid_spec = pltpu.PrefetchScalarGridSpec(
        num_scalar_prefetch=2,
        grid=(n // tile,),
        in_specs=[pl.BlockSpec(memory_space=pl.ANY),
                  pl.BlockSpec((tile, d), lambda i, *_: (i, 0)),
                  pl.BlockSpec((tile, LANES), lambda i, *_: (i, 0)),
                  pl.BlockSpec((tile, LANES), lambda i, *_: (i, 0)),
                  pl.BlockSpec((1, 1, d), lambda i, *_: (jnp.minimum(i, npt - 1) // tiles_per_seq, 0, 0)),
                  pl.BlockSpec((tile, d), lambda i, *_: (0, 0)),
                  pl.BlockSpec((1, d), lambda i, *_: (0, 0))],
        out_specs=(pl.BlockSpec((tile, d), lambda i, *_: (jnp.minimum(i, npt - 1), 0)),
                   pl.BlockSpec((tile, d), lambda i, *_: (0, 0))),
        scratch_shapes=[pltpu.VMEM((2, _sorted_rows(tile, n_experts), d), F32), pltpu.SemaphoreType.DMA((2,))],
    )
    return pl.pallas_call(
        functools.partial(_combine_kernel, n_experts=n_experts, tile=tile, n_prompt_tiles=npt),
        out_shape=(jax.ShapeDtypeStruct((n_prompt, d), F32), jax.ShapeDtypeStruct((n_s, d), F32)),
        grid_spec=grid_spec,
        compiler_params=_cparams(("arbitrary",)),
        name="combine_norm",
    )(cnt_flat, base_flat, ys, x2, pk4, gt, gate_p, gate_s, final_g)


def _expert_tables(tile_cnt, *, tile, n_j, n_n):
    n_tiles, n_experts = tile_cnt.shape
    tm, grp = EXP_TILE, EXP_GROUP
    cnt8 = (tile_cnt + (SUBLANES - 1)) // SUBLANES * SUBLANES
    rows_e = jnp.sum(cnt8, axis=0)
    max_rows = n_tiles * tile * TOP_K + n_experts * n_tiles * (SUBLANES - 1)
    t_max = -(-max_rows // tm) + n_experts
    ng_max = n_experts + t_max // grp
    ntile = (rows_e + tm - 1) // tm
    tile_start = jnp.cumsum(ntile) - ntile
    pstart = tile_start * tm
    base = pstart[None, :] + jnp.cumsum(cnt8, axis=0) - cnt8
    ng = (ntile + grp - 1) // grp
    cg = jnp.cumsum(ng)
    n_groups = cg[-1]
    q = jnp.arange(ng_max, dtype=I32)
    eq = jnp.minimum(jnp.sum((q[:, None] >= cg[None, :]).astype(I32), axis=1), n_experts - 1)
    onehot = eq[:, None] == jnp.arange(n_experts, dtype=I32)[None, :]

    def pick(v):
        return jnp.sum(jnp.where(onehot, v[None, :], 0), axis=1)

    lg = q - (pick(cg) - pick(ng))
    t0 = pick(tile_start) + lg * grp
    gs = jnp.clip(pick(ntile) - lg * grp, 0, grp)
    valid = q < n_groups
    is_last = q == jnp.maximum(n_groups - 1, 0)
    eq = jnp.where(valid, eq, jnp.sum(jnp.where(is_last, eq, 0)))
    t0 = jnp.where(valid, t0, jnp.sum(jnp.where(is_last, t0, 0)))
    gs = jnp.where(valid, gs, 0)
    steps = n_j + n_n
    ph = jnp.tile(jnp.arange(steps, dtype=I32), ng_max)
    vs = jnp.repeat(valid, steps)
    st_e = jnp.repeat(eq, steps)
    st_j = jnp.where(vs, jnp.minimum(ph, n_j - 1), n_j - 1)
    st_n = jnp.where(vs, jnp.maximum(ph - n_j, 0), n_n - 1)
    tail = jnp.maximum(pstart + (ntile - 1) * tm, 0)
    has = (ntile > 0).astype(I32)
    tabs = tuple(a.astype(I32) for a in (st_e, st_j, st_n, t0, gs, valid))
    return (tabs, tile_cnt.reshape(-1).astype(I32), base.reshape(-1).astype(I32), tail.astype(I32), has,
            t_max * tm, (n_groups * steps).astype(I32))


def _rope_tables(pos):
    inv = 1.0 / (ROPE_THETA ** (jnp.arange(0, HEAD_DIM, 2, dtype=F32) / HEAD_DIM))
    ang = pos.astype(F32)[:, None] * inv[None, :]
    reps = LANES // (HEAD_DIM // 2)
    return jnp.tile(jnp.cos(ang), (1, reps)), jnp.tile(jnp.sin(ang), (1, reps))


def kernel(x_prompt, x_sample, cache_k, cache_v, state_pool, page_table, c_prompt, c_sample, w_ada, b_ada, norm1_g, norm2_g, w_in, lam_q1, lam_k1, lam_q2, lam_k2, subln_g, w_pool, pool_scale, w_out, w_router, b_router, w_gate, b_gate, w_up, b_up, w_down, b_down, final_g):
    B, S, D = x_prompt.shape
    DB, T, _ = x_sample.shape
    depth = w_ada.shape[0]
    page = cache_k.shape[2]
    past = page_table.shape[1] * page
    d_att = D // 2
    n_heads = d_att // HEAD_W
    d_pool = w_in.shape[2] - 3 * d_att
    n_experts = w_router.shape[2]
    state_len = state_pool.shape[2]
    n_p, n_s = B * S, DB * T
    n_all = n_p + n_s
    tm = min(TOK_TILE, n_p)
    tiles_per_seq = S // tm
    assert depth == 1, "single-layer step"
    assert n_s == tm and S % tm == 0 and T % 8 == 0 and state_len < POOL_HALO <= tm

    cos_p, sin_p = _rope_tables(jnp.arange(S))
    cos_s, sin_s = _rope_tables(jnp.tile(past + jnp.arange(T), DB))

    l = 0
    lam_init = 0.8 - 0.6 * math.exp(-0.3 * l)
    lamp = jnp.zeros((8, LANES), F32)
    for r, vec in enumerate((lam_q1[l], lam_k1[l], lam_q2[l], lam_k2[l])):
        lamp = lamp.at[r, :HEAD_DIM].set(vec.astype(F32))
    subg = subln_g[l].reshape(1, HEAD_W)

    rows_c = -(-(B + DB) // 8) * 8
    c_all = jnp.zeros((rows_c, D), F32).at[:B].set(c_prompt).at[B:B + DB].set(c_sample)
    m_all = _adaln(c_all, w_ada[l], b_ada[l])
    mods_p = [m_all[:B, k * D:(k + 1) * D].reshape(B, 1, D) for k in range(N_ADA)]
    mods_s = [jnp.repeat(m_all[B:B + DB, k * D:(k + 1) * D], T, axis=0).reshape(1, n_s, D) for k in range(N_ADA)]

    w_in_b = _cast_bf16(w_in[l], 256)
    w_out_b = _cast_bf16(w_out[l], 256)
    ng, gw = w_pool.shape[1], w_pool.shape[2]
    w_pool_b = _cast_bf16(w_pool[l].reshape(ng * gw, gw), ng * gw).reshape(ng, gw, gw)
    g1 = norm1_g[l].reshape(1, D)
    g2 = norm2_g[l].reshape(1, D)
    pscale = pool_scale[l].reshape(1, d_pool)
    w_router_p = jnp.zeros((D, LANES), F32).at[:, :n_experts].set(w_router[l])
    b_router_p = jnp.zeros((1, LANES), F32).at[0, :n_experts].set(b_router[l].astype(F32))

    xp = x_prompt.reshape(n_p, D)
    q_p, k_p, v_p, u_p, kb_p, vb_p, kn_p = _project(xp, mods_p[0], mods_p[1], g1, w_in_b, cos_p, sin_p,
                                                    tiles_per_seq=tiles_per_seq, pos_tiles=tiles_per_seq)
    kn_seq = jnp.max(kn_p.reshape(B, tiles_per_seq, SUBLANES, LANES), axis=1)
    a_p = _attn_prompt(q_p, kb_p, vb_p, lamp, subg.reshape(HEAD_W, 1), kn_seq, batch=B, seq=S, lam_init=lam_init)
    pm_p = _pool_mix(u_p.reshape(B, S, d_pool), w_pool_b, pscale, tile=tm, cur_block0=0, pos0=0,
                     zero_first_halo=True).reshape(n_p, d_pool)

    xs_tok = x_sample.reshape(n_s, D)
    q_s, k_s, v_s, u_s, _, _, _ = _project(xs_tok, mods_s[0], mods_s[1], g1, w_in_b, cos_s, sin_s,
                                           tiles_per_seq=1, pos_tiles=1)
    a_s = _attn_sample(page_table, q_s, k_s, v_s, cache_k[l], cache_v[l], lamp, subg, t_new=T, lam_init=lam_init)
    u_ext = jnp.concatenate([jnp.zeros((DB, POOL_HALO - state_len, d_pool), F32),
                             state_pool[l].astype(F32), u_s.reshape(DB, T, d_pool)], axis=1)
    pm_s = _pool_mix(u_ext, w_pool_b, pscale, tile=T, cur_block0=POOL_HALO // T, pos0=past,
                     zero_first_halo=False).reshape(n_s, d_pool)

    outs = _mix_out(xp, a_p, pm_p, mods_p[2], mods_p[3], mods_p[4], g2, w_out_b, w_router_p, b_router_p,
                    n_total=n_all, tile0=0, tiles_per_seq=tiles_per_seq, n_experts=n_experts)
    x2, h2, pk4, gt4, tcnt = _mix_out(xs_tok, a_s, pm_s, mods_s[2], mods_s[3], mods_s[4], g2, w_out_b, w_router_p,
                                      b_router_p, n_total=n_all, tile0=n_p // tm, tiles_per_seq=1,
                                      n_experts=n_experts, prev=outs)

    d_ff = w_gate.shape[3]
    tf = min(FF_TILE, d_ff, D)
    tile_cnt = tcnt[:, 0, :n_experts].astype(I32)
    tabs, cnt_flat, base_flat, tail, has, r_max, n_steps = _expert_tables(
        tile_cnt, tile=tm, n_j=d_ff // tf, n_n=D // tf)
    xs_rows = _dispatch(h2, pk4, cnt_flat, base_flat, tail, has, r_max=r_max, n_experts=n_experts)
    ys_rows = _expert_ffn(tabs, xs_rows, w_gate[l], b_gate[l], w_up[l], b_up[l], w_down[l], b_down[l],
                          n_steps=n_steps)
    y_p, y_s = _combine(ys_rows, cnt_flat, base_flat, x2, pk4, gt4, mods_p[5], mods_s[5].reshape(n_s, D),
                        final_g.reshape(1, D), n_prompt=n_p, tiles_per_seq=tiles_per_seq, n_experts=n_experts)

    n_pages_p = S // page
    k_prompt = k_p.reshape(1, B, n_pages_p, page, n_heads, HEAD_W)
    v_prompt = v_p.reshape(1, B, n_pages_p, page, n_heads, HEAD_W)
    pool_prompt = u_p.reshape(B, S, d_pool)[:, S - state_len:][None]
    k_sample = k_s.reshape(1, DB, T, n_heads, HEAD_W)
    v_sample = v_s.reshape(1, DB, T, n_heads, HEAD_W)
    pool_sample = u_ext[:, -state_len:][None]
    return (y_p.reshape(B, S, D), y_s.reshape(DB, T, D), k_prompt, v_prompt, pool_prompt,
            k_sample, v_sample, pool_sample)
```

```python
import functools
import math

import jax
import jax.numpy as jnp
from jax import lax
from jax.experimental import pallas as pl
from jax.experimental.pallas import tpu as pltpu

F32 = jnp.float32
BF16 = jnp.bfloat16
I32 = jnp.int32

HEAD_DIM = 64
HEAD_W = 2 * HEAD_DIM
POOL_WINDOWS = (2, 4, 8, 16)
POOL_HALO = 16
TOP_K = 4
SWIGLU_LIMIT = 7.0
SWIGLU_ALPHA = 1.702
ROPE_THETA = 10000.0
NORM_EPS = 1e-5
N_ADA = 6
LANES = 128
SUBLANES = 8
V7X_VMEM_LIMIT = 58 * 1024 * 1024

TOK_TILE = 256
EXP_TILE = 256
EXP_GROUP = 6
FF_TILE = 512
Q_SCALE = (HEAD_DIM ** -0.5) * math.log2(math.e)
SCORE_BOUND = 60.0


def _cparams(sem, vmem=V7X_VMEM_LIMIT):
    return pltpu.CompilerParams(dimension_semantics=sem, vmem_limit_bytes=vmem)


def _cast_kernel(x_ref, o_ref):
    o_ref[...] = x_ref[...].astype(o_ref.dtype)


def _cast_bf16(w, rows):
    r, c = w.shape
    return pl.pallas_call(
        _cast_kernel,
        out_shape=jax.ShapeDtypeStruct((r, c), BF16),
        grid=(r // rows,),
        in_specs=[pl.BlockSpec((rows, c), lambda i: (i, 0))],
        out_specs=pl.BlockSpec((rows, c), lambda i: (i, 0)),
        compiler_params=_cparams(("arbitrary",)),
        name="cast_bf16",
    )(w)


def _ada_kernel(c_ref, w_ref, b_ref, o_ref):
    c = c_ref[...]
    s = (c * jax.nn.sigmoid(c)).astype(BF16)
    o_ref[...] = jnp.dot(s, w_ref[...].astype(BF16), preferred_element_type=F32) + b_ref[...]


def _adaln(c_all, w_ada, b_ada):
    rows, d = c_all.shape
    n = w_ada.shape[1]
    tn = min(1024, n)
    return pl.pallas_call(
        _ada_kernel,
        out_shape=jax.ShapeDtypeStruct((rows, n), F32),
        grid=(n // tn,),
        in_specs=[pl.BlockSpec((rows, d), lambda j: (0, 0)),
                  pl.BlockSpec((d, tn), lambda j: (0, j)),
                  pl.BlockSpec((1, tn), lambda j: (0, j))],
        out_specs=pl.BlockSpec((rows, tn), lambda j: (0, j)),
        compiler_params=_cparams(("arbitrary",)),
        name="adaln",
    )(c_all, w_ada, b_ada.reshape(1, n))


def _modulated_norm(x, g, shift, scale):
    ms = jnp.mean(x * x, axis=-1, keepdims=True)
    return (x * lax.rsqrt(ms + NORM_EPS) * g) * (1.0 + scale) + shift


def _proj_kernel(x_ref, shift_ref, scale_ref, g_ref, w_ref, cos_ref, sin_ref,
                 q_ref, k_ref, v_ref, u_ref, kb_ref, vb_ref, kn_ref, *, d_att):
    h = _modulated_norm(x_ref[...], g_ref[...], shift_ref[0], scale_ref[0]).astype(BF16)
    cos = cos_ref[...]
    sin = sin_ref[...]
    lane = lax.broadcasted_iota(I32, cos.shape, 1)
    first_half = (lane & (HEAD_DIM - 1)) < (HEAD_DIM // 2)
    head_lane = lax.broadcasted_iota(I32, (1, LANES), 1)

    def rope(z):
        rot = jnp.where(first_half, -pltpu.roll(z, LANES - HEAD_DIM // 2, 1), pltpu.roll(z, HEAD_DIM // 2, 1))
        return z * cos + rot * sin

    zq = jnp.dot(h, w_ref[:, 0:d_att], preferred_element_type=F32)
    zk = jnp.dot(h, w_ref[:, d_att:2 * d_att], preferred_element_type=F32)
    k_norm2 = jnp.zeros((1, LANES), F32)
    for hh in range(d_att // HEAD_W):
        sl = slice(hh * HEAD_W, (hh + 1) * HEAD_W)
        q_ref[:, sl] = rope(zq[:, sl]) * Q_SCALE
        kr = rope(zk[:, sl])
        k_ref[:, sl] = kr
        kb_ref[:, sl] = kr.astype(BF16)
        n2 = jnp.max(jnp.sum(kr * kr, axis=1, keepdims=True), axis=0, keepdims=True)
        k_norm2 = jnp.where(head_lane == hh, n2, k_norm2)
    kn_ref[0] = jnp.broadcast_to(k_norm2, kn_ref.shape[1:])
    zv = jnp.dot(h, w_ref[:, 2 * d_att:3 * d_att], preferred_element_type=F32)
    v_ref[...] = zv
    vb_ref[...] = zv.astype(BF16)
    u_ref[...] = jnp.dot(h, w_ref[:, 3 * d_att:], preferred_element_type=F32)


def _project(x, shift, scale, g, w_in_b, cos, sin, *, tiles_per_seq, pos_tiles):
    n, d = x.shape
    tm = min(TOK_TILE, n)
    d_in = w_in_b.shape[1]
    d_att = (d // 2)
    d_pool = d_in - 3 * d_att
    mod_rows = shift.shape[1]
    mod_spec = pl.BlockSpec((1, mod_rows, d), lambda i: (i // tiles_per_seq, 0, 0))
    tok = lambda c: pl.BlockSpec((tm, c), lambda i: (i, 0))
    return pl.pallas_call(
        functools.partial(_proj_kernel, d_att=d_att),
        out_shape=(jax.ShapeDtypeStruct((n, d_att), F32), jax.ShapeDtypeStruct((n, d_att), F32),
                   jax.ShapeDtypeStruct((n, d_att), F32), jax.ShapeDtypeStruct((n, d_pool), F32),
                   jax.ShapeDtypeStruct((n, d_att), BF16), jax.ShapeDtypeStruct((n, d_att), BF16),
                   jax.ShapeDtypeStruct((n // tm, SUBLANES, LANES), F32)),
        grid=(n // tm,),
        in_specs=[tok(d), mod_spec, mod_spec,
                  pl.BlockSpec((1, d), lambda i: (0, 0)),
                  pl.BlockSpec((d, d_in), lambda i: (0, 0)),
                  pl.BlockSpec((tm, LANES), lambda i: (i % pos_tiles, 0)),
                  pl.BlockSpec((tm, LANES), lambda i: (i % pos_tiles, 0))],
        out_specs=(tok(d_att), tok(d_att), tok(d_att), tok(d_pool), tok(d_att), tok(d_att),
                   pl.BlockSpec((1, SUBLANES, LANES), lambda i: (i, 0, 0))),
        compiler_params=_cparams(("arbitrary",)),
        name="in_proj",
    )(x, shift, scale, g, w_in_b, cos, sin)


def _lambda_value(lam_ref, lam_init):
    lp = lam_ref[...]
    a = jnp.sum(lp[0:1] * lp[1:2], axis=1, keepdims=True)
    b = jnp.sum(lp[2:3] * lp[3:4], axis=1, keepdims=True)
    return jnp.exp(a) - jnp.exp(b) + lam_init


def _stack_maps(q):
    lane = lax.broadcasted_iota(I32, q.shape, 1)
    q1 = jnp.where(lane < HEAD_DIM, q, 0.0)
    q2 = jnp.where(lane >= HEAD_DIM, q, 0.0)
    return jnp.concatenate([q1, q2], axis=0).astype(BF16)


_NT = (((1,), (1,)), ((), ()))


_TN = (((0,), (0,)), ((), ()))


def _attn_prompt_kernel(lam_ref, g_ref, kn_ref, q_ref, k_ref, k2_ref, v_ref, o_ref, m_ref, l_ref, acc_ref,
                        *, tq, tk, lam_init):
    qi = pl.program_id(2)
    heads = q_ref.shape[1] // HEAD_W
    qqs = [_stack_maps(q_ref[:, hh * HEAD_W:(hh + 1) * HEAD_W]) for hh in range(heads)]
    m_ref[...] = jnp.full(m_ref.shape, -jnp.inf, F32)
    l_ref[...] = jnp.zeros(l_ref.shape, F32)
    acc_ref[...] = jnp.zeros(acc_ref.shape, F32)

    lane = lax.broadcasted_iota(I32, (1, LANES), 1)
    kn = kn_ref[0, 0:1, :]
    worst = jnp.zeros((1, 1), F32)
    for hh in range(heads):
        qf = qqs[hh].astype(F32)
        qn2 = jnp.max(jnp.sum(qf * qf, axis=1, keepdims=True), axis=0, keepdims=True)
        kn2 = jnp.sum(jnp.where(lane == pl.program_id(1) * heads + hh, kn, 0.0), axis=1, keepdims=True)
        worst = jnp.maximum(worst, qn2 * kn2)
    bounded = worst[0, 0] <= SCORE_BOUND * SCORE_BOUND

    def run(fixed_reference):
        def block(j, masked):
            rows = pl.ds(pl.multiple_of(j * tk, tk), tk)
            for hh in range(heads):
                cols = slice(hh * HEAD_W, (hh + 1) * HEAD_W)

                def scores(kref):
                    st = lax.dot_general(kref[rows, cols], qqs[hh], _NT, preferred_element_type=F32)
                    if masked:
                        kpos = j * tk + lax.broadcasted_iota(I32, st.shape, 0)
                        c = lax.broadcasted_iota(I32, st.shape, 1)
                        qpos = qi * tq + jnp.where(c >= tq, c - tq, c)
                        st = jnp.where(kpos <= qpos, st, -jnp.inf)
                    return st

                if fixed_reference:
                    p = jnp.exp2(scores(k_ref))
                    l_ref[hh] = l_ref[hh] + jnp.sum(p, axis=0, keepdims=True)
                    acc_ref[hh] = acc_ref[hh] + lax.dot_general(v_ref[rows, cols], p.astype(BF16), _TN,
                                                                preferred_element_type=F32)
                else:
                    m_prev = m_ref[hh]
                    m_new = jnp.maximum(m_prev, jnp.max(scores(k_ref), axis=0, keepdims=True))
                    p = jnp.exp2(scores(k2_ref) - m_new)
                    alpha = jnp.exp2(m_prev - m_new)
                    l_ref[hh] = alpha * l_ref[hh] + jnp.sum(p, axis=0, keepdims=True)
                    pv = lax.dot_general(v_ref[rows, cols], p.astype(BF16), _TN, preferred_element_type=F32)
                    acc_ref[hh] = acc_ref[hh] * alpha + pv
                    m_ref[hh] = m_new

        n_full = (qi * tq) // tk

        def body(j, carry):
            block(j, False)
            return carry

        lax.fori_loop(0, n_full, body, 0)
        block(n_full, True)

    @pl.when(bounded)
    def _():
        run(True)

    @pl.when(jnp.logical_not(bounded))
    def _():
        run(False)

    lam = _lambda_value(lam_ref, lam_init)
    for hh in range(heads):
        l = l_ref[hh]
        acc = acc_ref[hh]
        o = acc[:, :tq] / l[:, :tq] - lam * (acc[:, tq:] / l[:, tq:])
        ms = jnp.mean(o * o, axis=0, keepdims=True)
        a = o * lax.rsqrt(ms + NORM_EPS) * g_ref[...] * (1.0 - lam_init)
        o_ref[:, hh * HEAD_W:(hh + 1) * HEAD_W] = a.T.astype(o_ref.dtype)


def _attn_prompt(q, kb, vb, lamp, g_col, k_norm2, *, batch, seq, lam_init):
    n, d_att = q.shape
    nh = d_att // HEAD_W
    hp = 4 if nh % 4 == 0 else 1
    tq = min(256, seq)
    tk = min(512, seq)
    nq = seq // tq
    return pl.pallas_call(
        functools.partial(_attn_prompt_kernel, tq=tq, tk=tk, lam_init=lam_init),
        out_shape=jax.ShapeDtypeStruct((n, d_att), BF16),
        grid=(batch, nh // hp, nq),
        in_specs=[pl.BlockSpec((8, LANES), lambda b, h, i: (0, 0)),
                  pl.BlockSpec((HEAD_W, 1), lambda b, h, i: (0, 0)),
                  pl.BlockSpec((1, SUBLANES, LANES), lambda b, h, i: (b, 0, 0)),
                  pl.BlockSpec((tq, hp * HEAD_W), lambda b, h, i: (b * nq + i, h)),
                  pl.BlockSpec((seq, hp * HEAD_W), lambda b, h, i: (b, h)),
                  pl.BlockSpec((seq, hp * HEAD_W), lambda b, h, i: (b, h)),
                  pl.BlockSpec((seq, hp * HEAD_W), lambda b, h, i: (b, h))],
        out_specs=pl.BlockSpec((tq, hp * HEAD_W), lambda b, h, i: (b * nq + i, h)),
        scratch_shapes=[pltpu.VMEM((hp, 1, 2 * tq), F32), pltpu.VMEM((hp, 1, 2 * tq), F32),
                        pltpu.VMEM((hp, HEAD_W, 2 * tq), F32)],
        compiler_params=_cparams(("arbitrary", "arbitrary", "arbitrary")),
        name="attn_prompt",
    )(lamp, g_col, k_norm2, q, kb, kb, vb)


def _attn_sample_kernel(pt_ref, lam_ref, g_ref, q_ref, kn_ref, vn_ref, *rest, n_heads, pages, t_new, lam_init):
    k_pages = rest[:pages]
    v_pages = rest[pages:2 * pages]
    o_ref = rest[2 * pages]
    m_ref, l_ref, acc_ref = rest[2 * pages + 1:]
    c = pl.program_id(1)
    rows_h = 2 * t_new
    assert n_heads == SUBLANES and n_heads * rows_h == LANES

    @pl.when(c == 0)
    def _():
        m_ref[...] = jnp.full(m_ref.shape, -jnp.inf, F32)
        l_ref[...] = jnp.zeros(l_ref.shape, F32)
        acc_ref[...] = jnp.zeros(acc_ref.shape, F32)

    q = q_ref[...]
    q_all = jnp.concatenate([_stack_maps(q[:, hh * HEAD_W:(hh + 1) * HEAD_W]) for hh in range(n_heads)], axis=0)
    sub = lax.broadcasted_iota(I32, (SUBLANES, LANES), 0)
    lane = lax.broadcasted_iota(I32, (SUBLANES, LANES), 1)
    own_head = (lane // rows_h) == sub

    def update(k_rows, v_rows, valid):
        n_pos = k_rows.shape[0] // n_heads
        r = lax.dot_general(k_rows.astype(BF16), q_all, _NT, preferred_element_type=F32)
        r = r.reshape(n_pos, n_heads, LANES)
        if valid is not None:
            r = jnp.where(valid, r, -jnp.inf)
        m_old = m_ref[...]
        m_new = jnp.maximum(m_old, jnp.max(r, axis=0))
        p = jnp.exp2(r - m_new[None])
        alpha = jnp.exp2(m_old - m_new)
        l_ref[...] = alpha * l_ref[...] + jnp.sum(p, axis=0)
        p_own = jnp.where(own_head[None], p, 0.0).reshape(n_pos * n_heads, LANES).astype(BF16)
        alpha_row = jnp.sum(jnp.where(own_head, alpha, 0.0), axis=0, keepdims=True)
        pv = lax.dot_general(v_rows.astype(BF16), p_own, _TN, preferred_element_type=F32)
        acc_ref[...] = acc_ref[...] * alpha_row + pv
        m_ref[...] = m_new

    for i in range(pages):
        update(k_pages[i][0], v_pages[i][0], None)

    @pl.when(c == pl.num_programs(1) - 1)
    def _():
        t_key = lax.broadcasted_iota(I32, (t_new, n_heads, LANES), 0)
        t_query = lax.broadcasted_iota(I32, (t_new, n_heads, LANES), 2) % t_new
        update(kn_ref[0], vn_ref[0], t_key <= t_query)
        l_row = jnp.sum(jnp.where(own_head, l_ref[...], 0.0), axis=0, keepdims=True)
        o_all = (acc_ref[...] / l_row).T
        lam = _lambda_value(lam_ref, lam_init)
        for hh in range(n_heads):
            rows = o_all[hh * rows_h:(hh + 1) * rows_h]
            o = rows[:t_new] - lam * rows[t_new:]
            ms = jnp.mean(o * o, axis=1, keepdims=True)
            o_ref[:, hh * HEAD_W:(hh + 1) * HEAD_W] = o * lax.rsqrt(ms + NORM_EPS) * g_ref[...] * (1.0 - lam_init)


def _attn_sample(page_table, q, k_new, v_new, cache_k, cache_v, lamp, g, *, t_new, lam_init):
    n, d_att = q.shape
    nh = d_att // HEAD_W
    db, n_pages = page_table.shape
    n_pool, page = cache_k.shape[0], cache_k.shape[1]
    pages = min(8, n_pages)
    ck = cache_k.reshape(n_pool, page * nh, HEAD_W)
    cv = cache_v.reshape(n_pool, page * nh, HEAD_W)

    def page_spec(i):
        return pl.BlockSpec((1, page * nh, HEAD_W), lambda b, c, pt: (pt[b, c * pages + i], 0, 0))

    tok = pl.BlockSpec((t_new, d_att), lambda b, c, pt: (b, 0))
    new_rows = pl.BlockSpec((1, t_new * nh, HEAD_W), lambda b, c, pt: (b, 0, 0))
    k_new = k_new.reshape(db, t_new, nh, HEAD_W).reshape(db, t_new * nh, HEAD_W)
    v_new = v_new.reshape(db, t_new, nh, HEAD_W).reshape(db, t_new * nh, HEAD_W)
    grid_spec = pltpu.PrefetchScalarGridSpec(
        num_scalar_prefetch=1,
        grid=(db, n_pages // pages),
        in_specs=[pl.BlockSpec((8, LANES), lambda b, c, pt: (0, 0)),
                  pl.BlockSpec((1, HEAD_W), lambda b, c, pt: (0, 0)),
                  tok, new_rows, new_rows]
                 + [page_spec(i) for i in range(pages)] + [page_spec(i) for i in range(pages)],
        out_specs=tok,
        scratch_shapes=[pltpu.VMEM((nh, LANES), F32), pltpu.VMEM((nh, LANES), F32),
                        pltpu.VMEM((HEAD_W, LANES), F32)],
    )
    return pl.pallas_call(
        functools.partial(_attn_sample_kernel, n_heads=nh, pages=pages, t_new=t_new, lam_init=lam_init),
        out_shape=jax.ShapeDtypeStruct((n, d_att), F32),
        grid_spec=grid_spec,
        compiler_params=_cparams(("arbitrary", "arbitrary")),
        name="attn_sample",
    )(page_table, lamp, g, q, k_new, v_new, *([ck] * pages), *([cv] * pages))


def _pool_kernel(halo_ref, cur_ref, w_ref, scale_ref, o_ref, ext_ref, *, pos0, tile_pos, zero_first_halo):
    i = pl.program_id(1)
    t = cur_ref.shape[1]
    halo = halo_ref[0]
    if zero_first_halo:
        halo = jnp.where(i == 0, 0.0, halo)
    cur = cur_ref[0]
    ext_ref[0:POOL_HALO, :] = halo
    ext_ref[POOL_HALO:POOL_HALO + t, :] = cur
    pos = pos0 + i * tile_pos + lax.broadcasted_iota(I32, (t, 1), 0)
    gw = cur.shape[1] // len(POOL_WINDOWS)
    for gi, w in enumerate(POOL_WINDOWS):
        cols = slice(gi * gw, (gi + 1) * gw)
        total = cur[:, cols]
        for j in range(1, w):
            total = total + ext_ref[POOL_HALO - j:POOL_HALO - j + t, cols]
        cnt = jnp.minimum(pos + 1, w).astype(F32)
        dlt = total / cnt - cur[:, cols]
        y = jnp.dot(dlt.astype(BF16), w_ref[gi], preferred_element_type=F32)
        o_ref[0, :, cols] = (y * scale_ref[:, cols]).astype(o_ref.dtype)


def _pool_mix(u3, w_pool_b, scale, *, tile, cur_block0, pos0, zero_first_halo):
    b, rows, c = u3.shape
    n_tiles = (rows - cur_block0 * tile) // tile
    ng, gw = w_pool_b.shape[0], w_pool_b.shape[1]
    assert (cur_block0 * tile) % POOL_HALO == 0 and (tile % POOL_HALO == 0 or n_tiles == 1)

    def halo_map(bi, i):
        return (bi, jnp.maximum(((cur_block0 + i) * tile) // POOL_HALO - 1, 0), 0)

    return pl.pallas_call(
        functools.partial(_pool_kernel, pos0=pos0, tile_pos=tile, zero_first_halo=zero_first_halo),
        out_shape=jax.ShapeDtypeStruct((b, n_tiles * tile, c), BF16),
        grid=(b, n_tiles),
        in_specs=[pl.BlockSpec((1, POOL_HALO, c), halo_map),
                  pl.BlockSpec((1, tile, c), lambda bi, i: (bi, cur_block0 + i, 0)),
                  pl.BlockSpec((ng, gw, gw), lambda bi, i: (0, 0, 0)),
                  pl.BlockSpec((1, c), lambda bi, i: (0, 0))],
        out_specs=pl.BlockSpec((1, tile, c), lambda bi, i: (bi, i, 0)),
        scratch_shapes=[pltpu.VMEM((POOL_HALO + tile, c), F32)],
        compiler_params=_cparams(("arbitrary", "arbitrary")),
        name="pool_mix",
    )(u3, u3, w_pool_b, scale)


def _split_bf16(x):
    hi = x.astype(BF16)
    lo = (x - hi.astype(F32)).astype(BF16)
    return hi, lo


def _round_up_f32(x, m):
    return jnp.floor((x + (m - 1.0)) * (1.0 / m)) * m


def _mix_out_kernel(*refs, n_experts, aliased):
    (x_ref, a_ref, pm_ref, gate_ref, shift_ref, scale_ref, g_ref, wo_ref, wr_ref, br_ref) = refs[:10]
    x2_ref, h2_ref, pk_ref, gt_ref, tc_ref = refs[10 + aliased:]
    d_att = a_ref.shape[1]
    mix = (jnp.dot(a_ref[...].astype(BF16), wo_ref[0:d_att, :], preferred_element_type=F32)
           + jnp.dot(pm_ref[...], wo_ref[d_att:, :], preferred_element_type=F32))
    x2 = x_ref[...] + gate_ref[0] * mix
    x2_ref[...] = x2
    h2 = _modulated_norm(x2, g_ref[...], shift_ref[0], scale_ref[0])
    h2_ref[...] = h2.astype(BF16)
    hh, hl = _split_bf16(h2)
    wh, wl = _split_bf16(wr_ref[...])
    logits = (jnp.dot(hh, wh, preferred_element_type=F32) + jnp.dot(hl, wh, preferred_element_type=F32)
              + jnp.dot(hh, wl, preferred_element_type=F32)) + br_ref[...]
    t = logits.shape[0]
    lane = lax.broadcasted_iota(I32, logits.shape, 1)
    lanef = lane.astype(F32)
    work = jnp.where(lane < n_experts, logits, -jnp.inf)
    vals, ids = [], []
    for _ in range(TOP_K):
        mx = jnp.max(work, axis=1, keepdims=True)
        ix = jnp.min(jnp.where(work == mx, lanef, float(LANES)), axis=1, keepdims=True)
        vals.append(mx)
        ids.append(ix)
        work = jnp.where(lanef == ix, -jnp.inf, work)
    es = [jnp.exp(v - vals[0]) for v in vals]
    den = es[0]
    for e in es[1:]:
        den = den + e
    sel = jnp.zeros(logits.shape, F32)
    for k in range(TOP_K):
        sel = jnp.where(lanef == ids[k], 1.0, sel)
    r = lax.broadcasted_iota(I32, (t, t), 0)
    c = lax.broadcasted_iota(I32, (t, t), 1)
    earlier = jnp.where(c < r, 1.0, 0.0).astype(BF16)
    local_rank = jnp.dot(earlier, sel.astype(BF16), preferred_element_type=F32)
    cnt = jnp.sum(sel, axis=0, keepdims=True)
    cnt8 = jnp.broadcast_to(_round_up_f32(cnt, float(SUBLANES)), (SUBLANES, LANES))
    er = lax.broadcasted_iota(I32, (LANES, LANES), 0)
    ec = lax.broadcasted_iota(I32, (LANES, LANES), 1)
    before = jnp.where(er < ec, 1.0, 0.0).astype(BF16)
    chunk_off = jnp.dot(cnt8.astype(BF16), before, preferred_element_type=F32)[0:1, :]
    pos = local_rank + chunk_off
    pk_out = jnp.zeros(logits.shape, F32)
    gt_out = jnp.zeros(logits.shape, F32)
    for k in range(TOP_K):
        pk = jnp.sum(jnp.where(lanef == ids[k], pos, 0.0), axis=1, keepdims=True)
        pk_out = jnp.where(lane == k, pk, pk_out)
        gt_out = jnp.where(lane == k, es[k] / den, gt_out)
    pk_ref[...] = pk_out
    gt_ref[...] = gt_out
    tc_ref[0] = jnp.broadcast_to(cnt, (SUBLANES, LANES))


def _mix_out(x, a, pm, gate, shift, scale, g, w_out_b, w_router_p, b_router_p, *, n_total, tile0,
             tiles_per_seq, n_experts, prev=None):
    n, d = x.shape
    tm = min(TOK_TILE, n)
    d_att = a.shape[1]
    mod_rows = gate.shape[1]
    mod_spec = pl.BlockSpec((1, mod_rows, d), lambda i: (i // tiles_per_seq, 0, 0))
    tok = lambda c: pl.BlockSpec((tm, c), lambda i: (i, 0))
    out_tok = lambda c: pl.BlockSpec((tm, c), lambda i: (tile0 + i, 0))
    out_shape = (jax.ShapeDtypeStruct((n_total, d), F32), jax.ShapeDtypeStruct((n_total, d), BF16),
                 jax.ShapeDtypeStruct((n_total, LANES), F32), jax.ShapeDtypeStruct((n_total, LANES), F32),
                 jax.ShapeDtypeStruct((n_total // tm, SUBLANES, LANES), F32))
    in_specs = [tok(d), tok(d_att), tok(pm.shape[1]), mod_spec, mod_spec, mod_spec,
                pl.BlockSpec((1, d), lambda i: (0, 0)),
                pl.BlockSpec(w_out_b.shape, lambda i: (0, 0)),
                pl.BlockSpec(w_router_p.shape, lambda i: (0, 0)),
                pl.BlockSpec((1, LANES), lambda i: (0, 0))]
    args = [x, a, pm, gate, shift, scale, g, w_out_b, w_router_p, b_router_p]
    aliases = {}
    n_alias = 0
    if prev is not None:
        n_alias = len(prev)
        in_specs += [pl.BlockSpec(memory_space=pl.ANY)] * n_alias
        aliases = {len(args) + k: k for k in range(n_alias)}
        args += list(prev)
    return pl.pallas_call(
        functools.partial(_mix_out_kernel, n_experts=n_experts, aliased=n_alias),
        out_shape=out_shape,
        grid=(n // tm,),
        in_specs=in_specs,
        out_specs=(out_tok(d), out_tok(d), out_tok(LANES), out_tok(LANES),
                   pl.BlockSpec((1, SUBLANES, LANES), lambda i: (tile0 + i, 0, 0))),
        input_output_aliases=aliases,
        compiler_params=_cparams(("arbitrary",)),
        name="mix_out_router",
    )(*args)


def _chunk_sizes(tile):
    sizes = []
    s = tile
    while s >= SUBLANES:
        sizes.append(s)
        s //= 2
    return sizes


def _for_each_chunk_piece(cnt_ref, base_ref, i, n_experts, tile, fn):
    def per_expert(e, off):
        l8 = (cnt_ref[i * n_experts + e] + (SUBLANES - 1)) & (-SUBLANES)
        dst0 = base_ref[i * n_experts + e]
        done = jnp.int32(0)
        for sz in _chunk_sizes(tile):
            take = l8 & sz

            @pl.when(take != 0)
            def _(done=done, sz=sz):
                fn(pl.multiple_of(off + done, SUBLANES), pl.multiple_of(dst0 + done, SUBLANES), sz)

            done = done + take
        return off + l8

    return lax.fori_loop(0, n_experts, per_expert, jnp.int32(0))


def _one_hot_rows(pk, rows, weights=None):
    t = pk.shape[0]
    col = lax.broadcasted_iota(I32, (t, rows), 1).astype(F32)
    out = jnp.zeros((t, rows), F32)
    for k in range(TOP_K):
        w = 1.0 if weights is None else weights[:, k:k + 1]
        out = jnp.where(col == pk[:, k:k + 1], w, out)
    return out.astype(BF16)


def _dispatch_kernel(cnt_ref, base_ref, tail_ref, has_ref, h_ref, pk_ref, xs_hbm, sorted_ref, zero_ref, sem,
                     zero_sem, *, n_experts, tile):
    i = pl.program_id(0)
    last = pl.num_programs(0) - 1

    def zero_copy(e):
        return pltpu.make_async_copy(
            zero_ref, xs_hbm.at[pl.ds(pl.multiple_of(tail_ref[e], EXP_TILE), EXP_TILE)], zero_sem)

    @pl.when(i == 0)
    def _():
        zero_ref[...] = jnp.zeros(zero_ref.shape, F32)
        for e in range(n_experts):
            @pl.when(has_ref[e] == 1)
            def _(e=e):
                zero_copy(e).start()
        for e in range(n_experts):
            @pl.when(has_ref[e] == 1)
            def _(e=e):
                zero_copy(e).wait()

    def copy(step, src, dst, sz):
        slot = step & 1
        return pltpu.make_async_copy(sorted_ref.at[slot, pl.ds(src, sz)], xs_hbm.at[pl.ds(dst, sz)], sem.at[slot])

    onehot = _one_hot_rows(pk_ref[...], sorted_ref.shape[1])
    sorted_ref[i & 1] = lax.dot_general(onehot, h_ref[...], _TN, preferred_element_type=F32)
    _for_each_chunk_piece(cnt_ref, base_ref, i, n_experts, tile, lambda s, d, sz: copy(i, s, d, sz).start())

    @pl.when(i > 0)
    def _():
        _for_each_chunk_piece(cnt_ref, base_ref, i - 1, n_experts, tile,
                              lambda s, d, sz: copy(i - 1, s, d, sz).wait())

    @pl.when(i == last)
    def _():
        _for_each_chunk_piece(cnt_ref, base_ref, i, n_experts, tile, lambda s, d, sz: copy(i, s, d, sz).wait())


def _sorted_rows(tile, n_experts):
    rows = tile * TOP_K + n_experts * (SUBLANES - 1)
    return -(-rows // LANES) * LANES


def _dispatch(h2, pk4, cnt_flat, base_flat, tail_rows, has_tile, *, r_max, n_experts):
    n, d = h2.shape
    tile = min(TOK_TILE, n)
    grid_spec = pltpu.PrefetchScalarGridSpec(
        num_scalar_prefetch=4,
        grid=(n // tile,),
        in_specs=[pl.BlockSpec((tile, d), lambda i, *_: (i, 0)),
                  pl.BlockSpec((tile, LANES), lambda i, *_: (i, 0))],
        out_specs=pl.BlockSpec(memory_space=pl.ANY),
        scratch_shapes=[pltpu.VMEM((2, _sorted_rows(tile, n_experts), d), F32),
                        pltpu.VMEM((EXP_TILE, d), F32), pltpu.SemaphoreType.DMA((2,)), pltpu.SemaphoreType.DMA],
    )
    return pl.pallas_call(
        functools.partial(_dispatch_kernel, n_experts=n_experts, tile=tile),
        out_shape=jax.ShapeDtypeStruct((r_max, d), F32),
        grid_spec=grid_spec,
        compiler_params=_cparams(("arbitrary",)),
        name="dispatch_rows",
    )(cnt_flat, base_flat, tail_rows, has_tile, h2, pk4)


def _expert_kernel(st_e, st_j, st_n, g_t0, g_gs, g_valid,
                   xs_hbm, wg_ref, wu_ref, wd_ref, bg_ref, bu_ref, bd_ref, ys_hbm,
                   xbuf, actbuf, wgb, wub, wdb, stage_in, stage_out, pending, sem_in, sem_out,
                   *, n_j, n_n, tm, tf):
    s = pl.program_id(0)
    steps = n_j + n_n
    q = s // steps
    ph = s - q * steps
    valid = g_valid[q] == 1
    gs = g_gs[q]
    row0 = g_t0[q] * tm

    @pl.when(s == 0)
    def _():
        pending[0] = 0
        pending[1] = 0

    def in_copy(first_row, i):
        slot = i & 1
        return pltpu.make_async_copy(xs_hbm.at[pl.ds(pl.multiple_of(first_row + i * tm, tm), tm)],
                                     stage_in.at[slot], sem_in.at[slot])

    def stage_to_xbuf(i):
        xbuf[pl.ds(pl.multiple_of(i * tm, tm), tm), :] = stage_in[i & 1].astype(BF16)

    @pl.when(s == 0)
    def _():
        in_copy(row0, 0).start()

        def load(i, carry):
            @pl.when(i + 1 < gs)
            def _():
                in_copy(row0, i + 1).start()

            in_copy(row0, i).wait()
            stage_to_xbuf(i)
            return carry

        lax.fori_loop(0, gs, load, 0)

    q_next = jnp.minimum(q + 1, g_gs.shape[0] - 1)
    gs_next = jnp.where(q + 1 < g_gs.shape[0], g_gs[q_next], 0)
    row0_next = g_t0[q_next] * tm
    fetch_tiles = [2 * (ph - n_j), 2 * (ph - n_j) + 1]

    n_pairs = gs // 2
    has_tail = (gs & 1) == 1
    tail_row = pl.multiple_of(n_pairs * 2 * tm, tm)

    @pl.when(jnp.logical_and(valid, ph < n_j))
    def _():
        wgb[...] = wg_ref[0].astype(BF16)
        wub[...] = wu_ref[0].astype(BF16)

        def tile(r0, size):
            rows = pl.ds(r0, size)
            x = xbuf[rows, :]
            gate = jnp.dot(x, wgb[...], preferred_element_type=F32) + bg_ref[0]
            up = jnp.dot(x, wub[...], preferred_element_type=F32) + bu_ref[0]
            gate = jnp.minimum(gate, SWIGLU_LIMIT)
            up = jnp.clip(up, -SWIGLU_LIMIT, SWIGLU_LIMIT)
            act = (up + 1.0) * (gate * jax.nn.sigmoid(SWIGLU_ALPHA * gate))
            actbuf[ph, rows, :] = act.astype(BF16)

        def pair(i, carry):
            tile(pl.multiple_of(i * 2 * tm, 2 * tm), 2 * tm)
            return carry

        lax.fori_loop(0, n_pairs, pair, 0)

        @pl.when(has_tail)
        def _():
            tile(tail_row, tm)

    @pl.when(jnp.logical_and(valid, ph >= n_j))
    def _():
        for t in fetch_tiles:
            @pl.when(t < gs_next)
            def _(t=t):
                in_copy(row0_next, t).start()

        wdb[...] = wd_ref[0].astype(BF16)
        col0 = pl.multiple_of((ph - n_j) * tf, tf)

        def out_copy(slot, r0, size):
            return pltpu.make_async_copy(
                stage_out.at[slot, pl.ds(0, size)],
                ys_hbm.at[pl.ds(pl.multiple_of(row0 + r0, tm), size), pl.ds(col0, tf)],
                sem_out.at[slot])

        def wait_slot(slot):
            for n_tiles in (1, 2):
                @pl.when(pending[slot] == n_tiles)
                def _(n_tiles=n_tiles):
                    out_copy(slot, 0, n_tiles * tm).wait()
                    pending[slot] = 0

        def tile(r0, size, slot):
            rows = pl.ds(r0, size)
            wait_slot(slot)
            y = bd_ref[0] + jnp.dot(actbuf[0, rows, :], wdb[0:tf, :], preferred_element_type=F32)
            for j in range(1, n_j):
                y = y + jnp.dot(actbuf[j, rows, :], wdb[j * tf:(j + 1) * tf, :], preferred_element_type=F32)
            stage_out[slot, 0:size] = y
            out_copy(slot, r0, size).start()
            pending[slot] = size // tm

        def pair(i, carry):
            tile(pl.multiple_of(i * 2 * tm, 2 * tm), 2 * tm, i & 1)
            return carry

        lax.fori_loop(0, n_pairs, pair, 0)

        @pl.when(has_tail)
        def _():
            tile(tail_row, tm, n_pairs & 1)

        for t in fetch_tiles:
            @pl.when(t < gs_next)
            def _(t=t):
                in_copy(row0_next, t).wait()
                stage_to_xbuf(t)

        @pl.when(s == pl.num_programs(0) - 1)
        def _():
            wait_slot(0)
            wait_slot(1)


def _expert_ffn(tables, xs, w_gate, b_gate, w_up, b_up, w_down, b_down, *, n_steps):
    st_e, st_j, st_n, g_t0, g_gs, g_valid = tables
    r_max, d = xs.shape
    n_exp, _, d_ff = w_gate.shape
    tf = min(FF_TILE, d_ff, d)
    n_j = d_ff // tf
    n_n = d // tf
    tm = EXP_TILE
    rows_g = EXP_GROUP * tm
    assert 2 * n_n >= EXP_GROUP, "phase-2 steps fetch two row tiles of the next group each"
    grid_spec = pltpu.PrefetchScalarGridSpec(
        num_scalar_prefetch=6,
        grid=(n_steps,),
        in_specs=[pl.BlockSpec(memory_space=pl.ANY),
                  pl.BlockSpec((1, d, tf), lambda s, e, j, n, *_: (e[s], 0, j[s])),
                  pl.BlockSpec((1, d, tf), lambda s, e, j, n, *_: (e[s], 0, j[s])),
                  pl.BlockSpec((1, d_ff, tf), lambda s, e, j, n, *_: (e[s], 0, n[s])),
                  pl.BlockSpec((1, 1, tf), lambda s, e, j, n, *_: (e[s], 0, j[s])),
                  pl.BlockSpec((1, 1, tf), lambda s, e, j, n, *_: (e[s], 0, j[s])),
                  pl.BlockSpec((1, 1, tf), lambda s, e, j, n, *_: (e[s], 0, n[s]))],
        out_specs=pl.BlockSpec(memory_space=pl.ANY),
        scratch_shapes=[pltpu.VMEM((rows_g, d), BF16),
                        pltpu.VMEM((n_j, rows_g, tf), BF16),
                        pltpu.VMEM((d, tf), BF16), pltpu.VMEM((d, tf), BF16), pltpu.VMEM((d_ff, tf), BF16),
                        pltpu.VMEM((2, tm, d), F32), pltpu.VMEM((2, 2 * tm, tf), F32), pltpu.SMEM((2,), I32),
                        pltpu.SemaphoreType.DMA((2,)), pltpu.SemaphoreType.DMA((2,))],
    )
    return pl.pallas_call(
        functools.partial(_expert_kernel, n_j=n_j, n_n=n_n, tm=tm, tf=tf),
        out_shape=jax.ShapeDtypeStruct((r_max, d), F32),
        grid_spec=grid_spec,
        compiler_params=_cparams(("arbitrary",)),
        name="expert_ffn",
    )(st_e, st_j, st_n, g_t0, g_gs, g_valid, xs, w_gate, w_up, w_down,
      b_gate.reshape(n_exp, 1, d_ff), b_up.reshape(n_exp, 1, d_ff), b_down.reshape(n_exp, 1, d))


def _combine_kernel(cnt_ref, base_ref, ys_hbm, x2_ref, pk_ref, gt_ref, gate_p_ref, gate_s_ref, g_ref,
                    yp_ref, ysm_ref, rows_ref, sem, *, n_experts, tile, n_prompt_tiles):
    i = pl.program_id(0)
    last = pl.num_programs(0) - 1

    def copy(step, dst, src, sz):
        slot = step & 1
        return pltpu.make_async_copy(ys_hbm.at[pl.ds(src, sz)], rows_ref.at[slot, pl.ds(dst, sz)], sem.at[slot])

    def fetch(step):
        _for_each_chunk_piece(cnt_ref, base_ref, step, n_experts, tile,
                              lambda s, d, sz: copy(step, s, d, sz).start())

    @pl.when(i == 0)
    def _():
        rows_ref[...] = jnp.zeros(rows_ref.shape, F32)
        fetch(i)

    @pl.when(i < last)
    def _():
        fetch(i + 1)

    _for_each_chunk_piece(cnt_ref, base_ref, i, n_experts, tile, lambda s, d, sz: copy(i, s, d, sz).wait())
    weights = _one_hot_rows(pk_ref[...], rows_ref.shape[1], gt_ref[...])
    y = jnp.dot(weights, rows_ref[i & 1].astype(BF16), preferred_element_type=F32)
    is_prompt = i < n_prompt_tiles
    gate = jnp.where(is_prompt, gate_p_ref[0], gate_s_ref[...])
    x3 = x2_ref[...] + gate * y
    ms = jnp.mean(x3 * x3, axis=-1, keepdims=True)
    out = x3 * lax.rsqrt(ms + NORM_EPS) * g_ref[...]

    @pl.when(is_prompt)
    def _():
        yp_ref[...] = out

    @pl.when(jnp.logical_not(is_prompt))
    def _():
        ysm_ref[...] = out


def _combine(ys, cnt_flat, base_flat, x2, pk4, gt, gate_p, gate_s, final_g, *, n_prompt, tiles_per_seq, n_experts):
    n, d = x2.shape
    tile = min(TOK_TILE, n_prompt)
    n_s = n - n_prompt
    assert n_s == tile and n_prompt % tile == 0
    npt = n_prompt // tile
    grid_spec = pltpu.PrefetchScalarGridSpec(
        num_scalar_prefetch=2,
        grid=(n // tile,),
        in_specs=[pl.BlockSpec(memory_space=pl.ANY),
                  pl.BlockSpec((tile, d), lambda i, *_: (i, 0)),
                  pl.BlockSpec((tile, LANES), lambda i, *_: (i, 0)),
                  pl.BlockSpec((tile, LANES), lambda i, *_: (i, 0)),
                  pl.BlockSpec((1, 1, d), lambda i, *_: (jnp.minimum(i, npt - 1) // tiles_per_seq, 0, 0)),
                  pl.BlockSpec((tile, d), lambda i, *_: (0, 0)),
                  pl.BlockSpec((1, d), lambda i, *_: (0, 0))],
        out_specs=(pl.BlockSpec((tile, d), lambda i, *_: (jnp.minimum(i, npt - 1), 0)),
                   pl.BlockSpec((tile, d), lambda i, *_: (0, 0))),
        scratch_shapes=[pltpu.VMEM((2, _sorted_rows(tile, n_experts), d), F32), pltpu.SemaphoreType.DMA((2,))],
    )
    return pl.pallas_call(
        functools.partial(_combine_kernel, n_experts=n_experts, tile=tile, n_prompt_tiles=npt),
        out_shape=(jax.ShapeDtypeStruct((n_prompt, d), F32), jax.ShapeDtypeStruct((n_s, d), F32)),
        grid_spec=grid_spec,
        compiler_params=_cparams(("arbitrary",)),
        name="combine_norm",
    )(cnt_flat, base_flat, ys, x2, pk4, gt, gate_p, gate_s, final_g)


def _expert_tables(tile_cnt, *, tile, n_j, n_n):
    n_tiles, n_experts = tile_cnt.shape
    tm, grp = EXP_TILE, EXP_GROUP
    cnt8 = (tile_cnt + (SUBLANES - 1)) // SUBLANES * SUBLANES
    rows_e = jnp.sum(cnt8, axis=0)
    max_rows = n_tiles * tile * TOP_K + n_experts * n_tiles * (SUBLANES - 1)
    t_max = -(-max_rows // tm) + n_experts
    ng_max = n_experts + t_max // grp
    ntile = (rows_e + tm - 1) // tm
    tile_start = jnp.cumsum(ntile) - ntile
    pstart = tile_start * tm
    base = pstart[None, :] + jnp.cumsum(cnt8, axis=0) - cnt8
    ng = (ntile + grp - 1) // grp
    cg = jnp.cumsum(ng)
    n_groups = cg[-1]
    q = jnp.arange(ng_max, dtype=I32)
    eq = jnp.minimum(jnp.sum((q[:, None] >= cg[None, :]).astype(I32), axis=1), n_experts - 1)
    onehot = eq[:, None] == jnp.arange(n_experts, dtype=I32)[None, :]

    def pick(v):
        return jnp.sum(jnp.where(onehot, v[None, :], 0), axis=1)

    lg = q - (pick(cg) - pick(ng))
    t0 = pick(tile_start) + lg * grp
    gs = jnp.clip(pick(ntile) - lg * grp, 0, grp)
    valid = q < n_groups
    is_last = q == jnp.maximum(n_groups - 1, 0)
    eq = jnp.where(valid, eq, jnp.sum(jnp.where(is_last, eq, 0)))
    t0 = jnp.where(valid, t0, jnp.sum(jnp.where(is_last, t0, 0)))
    gs = jnp.where(valid, gs, 0)
    steps = n_j + n_n
    ph = jnp.tile(jnp.arange(steps, dtype=I32), ng_max)
    vs = jnp.repeat(valid, steps)
    st_e = jnp.repeat(eq, steps)
    st_j = jnp.where(vs, jnp.minimum(ph, n_j - 1), n_j - 1)
    st_n = jnp.where(vs, jnp.maximum(ph - n_j, 0), n_n - 1)
    tail = jnp.maximum(pstart + (ntile - 1) * tm, 0)
    has = (ntile > 0).astype(I32)
    tabs = tuple(a.astype(I32) for a in (st_e, st_j, st_n, t0, gs, valid))
    return (tabs, tile_cnt.reshape(-1).astype(I32), base.reshape(-1).astype(I32), tail.astype(I32), has,
            t_max * tm, (n_groups * steps).astype(I32))


def _rope_tables(pos):
    inv = 1.0 / (ROPE_THETA ** (jnp.arange(0, HEAD_DIM, 2, dtype=F32) / HEAD_DIM))
    ang = pos.astype(F32)[:, None] * inv[None, :]
    reps = LANES // (HEAD_DIM // 2)
    return jnp.tile(jnp.cos(ang), (1, reps)), jnp.tile(jnp.sin(ang), (1, reps))


def kernel(x_prompt, x_sample, cache_k, cache_v, state_pool, page_table, c_prompt, c_sample, w_ada, b_ada, norm1_g, norm2_g, w_in, lam_q1, lam_k1, lam_q2, lam_k2, subln_g, w_pool, pool_scale, w_out, w_router, b_router, w_gate, b_gate, w_up, b_up, w_down, b_down, final_g):
    B, S, D = x_prompt.shape
    DB, T, _ = x_sample.shape
    depth = w_ada.shape[0]
    page = cache_k.shape[2]
    past = page_table.shape[1] * page
    d_att = D // 2
    n_heads = d_att // HEAD_W
    d_pool = w_in.shape[2] - 3 * d_att
    n_experts = w_router.shape[2]
    state_len = state_pool.shape[2]
    n_p, n_s = B * S, DB * T
    n_all = n_p + n_s
    tm = min(TOK_TILE, n_p)
    tiles_per_seq = S // tm
    assert depth == 1, "single-layer step"
    assert n_s == tm and S % tm == 0 and T % 8 == 0 and state_len < POOL_HALO <= tm

    cos_p, sin_p = _rope_tables(jnp.arange(S))
    cos_s, sin_s = _rope_tables(jnp.tile(past + jnp.arange(T), DB))

    l = 0
    lam_init = 0.8 - 0.6 * math.exp(-0.3 * l)
    lamp = jnp.zeros((8, LANES), F32)
    for r, vec in enumerate((lam_q1[l], lam_k1[l], lam_q2[l], lam_k2[l])):
        lamp = lamp.at[r, :HEAD_DIM].set(vec.astype(F32))
    subg = subln_g[l].reshape(1, HEAD_W)

    rows_c = -(-(B + DB) // 8) * 8
    c_all = jnp.zeros((rows_c, D), F32).at[:B].set(c_prompt).at[B:B + DB].set(c_sample)
    m_all = _adaln(c_all, w_ada[l], b_ada[l])
    mods_p = [m_all[:B, k * D:(k + 1) * D].reshape(B, 1, D) for k in range(N_ADA)]
    mods_s = [jnp.repeat(m_all[B:B + DB, k * D:(k + 1) * D], T, axis=0).reshape(1, n_s, D) for k in range(N_ADA)]

    w_in_b = _cast_bf16(w_in[l], 256)
    w_out_b = _cast_bf16(w_out[l], 256)
    ng, gw = w_pool.shape[1], w_pool.shape[2]
    w_pool_b = _cast_bf16(w_pool[l].reshape(ng * gw, gw), ng * gw).reshape(ng, gw, gw)
    g1 = norm1_g[l].reshape(1, D)
    g2 = norm2_g[l].reshape(1, D)
    pscale = pool_scale[l].reshape(1, d_pool)
    w_router_p = jnp.zeros((D, LANES), F32).at[:, :n_experts].set(w_router[l])
    b_router_p = jnp.zeros((1, LANES), F32).at[0, :n_experts].set(b_router[l].astype(F32))

    xp = x_prompt.reshape(n_p, D)
    q_p, k_p, v_p, u_p, kb_p, vb_p, kn_p = _project(xp, mods_p[0], mods_p[1], g1, w_in_b, cos_p, sin_p,
                                                    tiles_per_seq=tiles_per_seq, pos_tiles=tiles_per_seq)
    kn_seq = jnp.max(kn_p.reshape(B, tiles_per_seq, SUBLANES, LANES), axis=1)
    a_p = _attn_prompt(q_p, kb_p, vb_p, lamp, subg.reshape(HEAD_W, 1), kn_seq, batch=B, seq=S, lam_init=lam_init)
    pm_p = _pool_mix(u_p.reshape(B, S, d_pool), w_pool_b, pscale, tile=tm, cur_block0=0, pos0=0,
                     zero_first_halo=True).reshape(n_p, d_pool)

    xs_tok = x_sample.reshape(n_s, D)
    q_s, k_s, v_s, u_s, _, _, _ = _project(xs_tok, mods_s[0], mods_s[1], g1, w_in_b, cos_s, sin_s,
                                           tiles_per_seq=1, pos_tiles=1)
    a_s = _attn_sample(page_table, q_s, k_s, v_s, cache_k[l], cache_v[l], lamp, subg, t_new=T, lam_init=lam_init)
    u_ext = jnp.concatenate([jnp.zeros((DB, POOL_HALO - state_len, d_pool), F32),
                             state_pool[l].astype(F32), u_s.reshape(DB, T, d_pool)], axis=1)
    pm_s = _pool_mix(u_ext, w_pool_b, pscale, tile=T, cur_block0=POOL_HALO // T, pos0=past,
                     zero_first_halo=False).reshape(n_s, d_pool)

    outs = _mix_out(xp, a_p, pm_p, mods_p[2], mods_p[3], mods_p[4], g2, w_out_b, w_router_p, b_router_p,
                    n_total=n_all, tile0=0, tiles_per_seq=tiles_per_seq, n_experts=n_experts)
    x2, h2, pk4, gt4, tcnt = _mix_out(xs_tok, a_s, pm_s, mods_s[2], mods_s[3], mods_s[4], g2, w_out_b, w_router_p,
                                      b_router_p, n_total=n_all, tile0=n_p // tm, tiles_per_seq=1,
                                      n_experts=n_experts, prev=outs)

    d_ff = w_gate.shape[3]
    tf = min(FF_TILE, d_ff, D)
    tile_cnt = tcnt[:, 0, :n_experts].astype(I32)
    tabs, cnt_flat, base_flat, tail, has, r_max, n_steps = _expert_tables(
        tile_cnt, tile=tm, n_j=d_ff // tf, n_n=D // tf)
    xs_rows = _dispatch(h2, pk4, cnt_flat, base_flat, tail, has, r_max=r_max, n_experts=n_experts)
    ys_rows = _expert_ffn(tabs, xs_rows, w_gate[l], b_gate[l], w_up[l], b_up[l], w_down[l], b_down[l],
                          n_steps=n_steps)
    y_p, y_s = _combine(ys_rows, cnt_flat, base_flat, x2, pk4, gt4, mods_p[5], mods_s[5].reshape(n_s, D),
                        final_g.reshape(1, D), n_prompt=n_p, tiles_per_seq=tiles_per_seq, n_experts=n_experts)

    n_pages_p = S // page
    k_prompt = k_p.reshape(1, B, n_pages_p, page, n_heads, HEAD_W)
    v_prompt = v_p.reshape(1, B, n_pages_p, page, n_heads, HEAD_W)
    pool_prompt = u_p.reshape(B, S, d_pool)[:, S - state_len:][None]
    k_sample = k_s.reshape(1, DB, T, n_heads, HEAD_W)
    v_sample = v_s.reshape(1, DB, T, n_heads, HEAD_W)
    pool_sample = u_ext[:, -state_len:][None]
    return (y_p.reshape(B, S, D), y_s.reshape(DB, T, D), k_prompt, v_prompt, pool_prompt,
            k_sample, v_sample, pool_sample)
```

```python
import functools
import math

import jax
import jax.numpy as jnp
from jax import lax
from jax.experimental import pallas as pl
from jax.experimental.pallas import tpu as pltpu

F32 = jnp.float32
BF16 = jnp.bfloat16
I32 = jnp.int32
U32 = jnp.uint32

HEAD_DIM = 64
HEAD_W = 2 * HEAD_DIM
POOL_WINDOWS = (2, 4, 8, 16)
POOL_HALO = 16
TOP_K = 4
SWIGLU_LIMIT = 7.0
SWIGLU_ALPHA = 1.702
ROPE_THETA = 10000.0
NORM_EPS = 1e-5
N_ADA = 6
LANES = 128
SUBLANES = 8
V7X_VMEM_LIMIT = 58 * 1024 * 1024

TOK_TILE = 256
EXP_TILE = 256
EXP_GROUP = 6
FF_TILE = 512
Q_SCALE = (HEAD_DIM ** -0.5) * math.log2(math.e)
SCORE_BOUND = 60.0


def _cparams(sem, vmem=V7X_VMEM_LIMIT):
    return pltpu.CompilerParams(dimension_semantics=sem, vmem_limit_bytes=vmem)


def _cast_kernel(x_ref, o_ref):
    o_ref[...] = x_ref[...].astype(o_ref.dtype)


def _cast_bf16(w, rows):
    r, c = w.shape
    return pl.pallas_call(
        _cast_kernel,
        out_shape=jax.ShapeDtypeStruct((r, c), BF16),
        grid=(r // rows,),
        in_specs=[pl.BlockSpec((rows, c), lambda i: (i, 0))],
        out_specs=pl.BlockSpec((rows, c), lambda i: (i, 0)),
        compiler_params=_cparams(("arbitrary",)),
        name="cast_bf16",
    )(w)


def _ada_kernel(c_ref, w_ref, b_ref, o_ref):
    c = c_ref[...]
    s = (c * jax.nn.sigmoid(c)).astype(BF16)
    o_ref[...] = jnp.dot(s, w_ref[...].astype(BF16), preferred_element_type=F32) + b_ref[...]


def _adaln(c_all, w_ada, b_ada):
    rows, d = c_all.shape
    n = w_ada.shape[1]
    tn = min(1024, n)
    return pl.pallas_call(
        _ada_kernel,
        out_shape=jax.ShapeDtypeStruct((rows, n), F32),
        grid=(n // tn,),
        in_specs=[pl.BlockSpec((rows, d), lambda j: (0, 0)),
                  pl.BlockSpec((d, tn), lambda j: (0, j)),
                  pl.BlockSpec((1, tn), lambda j: (0, j))],
        out_specs=pl.BlockSpec((rows, tn), lambda j: (0, j)),
        compiler_params=_cparams(("arbitrary",)),
        name="adaln",
    )(c_all, w_ada, b_ada.reshape(1, n))


def _modulated_norm(x, g, shift, scale):
    ms = jnp.mean(x * x, axis=-1, keepdims=True)
    return (x * lax.rsqrt(ms + NORM_EPS) * g) * (1.0 + scale) + shift


def _proj_kernel(x_ref, shift_ref, scale_ref, g_ref, w_ref, cos_ref, sin_ref,
                 q_ref, k_ref, v_ref, u_ref, kb_ref, vb_ref, kn_ref, *, d_att):
    h = _modulated_norm(x_ref[...], g_ref[...], shift_ref[0], scale_ref[0]).astype(BF16)
    cos = cos_ref[...]
    sin = sin_ref[...]
    lane = lax.broadcasted_iota(I32, cos.shape, 1)
    first_half = (lane & (HEAD_DIM - 1)) < (HEAD_DIM // 2)
    head_lane = lax.broadcasted_iota(I32, (1, LANES), 1)

    def rope(z):
        rot = jnp.where(first_half, -pltpu.roll(z, LANES - HEAD_DIM // 2, 1), pltpu.roll(z, HEAD_DIM // 2, 1))
        return z * cos + rot * sin

    zq = jnp.dot(h, w_ref[:, 0:d_att], preferred_element_type=F32)
    zk = jnp.dot(h, w_ref[:, d_att:2 * d_att], preferred_element_type=F32)
    k_norm2 = jnp.zeros((1, LANES), F32)
    for hh in range(d_att // HEAD_W):
        sl = slice(hh * HEAD_W, (hh + 1) * HEAD_W)
        q_ref[:, sl] = rope(zq[:, sl]) * Q_SCALE
        kr = rope(zk[:, sl])
        k_ref[:, sl] = kr
        kb_ref[:, sl] = kr.astype(BF16)
        n2 = jnp.max(jnp.sum(kr * kr, axis=1, keepdims=True), axis=0, keepdims=True)
        k_norm2 = jnp.where(head_lane == hh, n2, k_norm2)
    kn_ref[0] = jnp.broadcast_to(k_norm2, kn_ref.shape[1:])
    zv = jnp.dot(h, w_ref[:, 2 * d_att:3 * d_att], preferred_element_type=F32)
    v_ref[...] = zv
    vb_ref[...] = zv.astype(BF16)
    u_ref[...] = jnp.dot(h, w_ref[:, 3 * d_att:], preferred_element_type=F32)


def _project(x, shift, scale, g, w_in_b, cos, sin, *, tiles_per_seq, pos_tiles):
    n, d = x.shape
    tm = min(TOK_TILE, n)
    d_in = w_in_b.shape[1]
    d_att = (d // 2)
    d_pool = d_in - 3 * d_att
    mod_rows = shift.shape[1]
    mod_spec = pl.BlockSpec((1, mod_rows, d), lambda i: (i // tiles_per_seq, 0, 0))
    tok = lambda c: pl.BlockSpec((tm, c), lambda i: (i, 0))
    return pl.pallas_call(
        functools.partial(_proj_kernel, d_att=d_att),
        out_shape=(jax.ShapeDtypeStruct((n, d_att), F32), jax.ShapeDtypeStruct((n, d_att), F32),
                   jax.ShapeDtypeStruct((n, d_att), F32), jax.ShapeDtypeStruct((n, d_pool), F32),
                   jax.ShapeDtypeStruct((n, d_att), BF16), jax.ShapeDtypeStruct((n, d_att), BF16),
                   jax.ShapeDtypeStruct((n // tm, SUBLANES, LANES), F32)),
        grid=(n // tm,),
        in_specs=[tok(d), mod_spec, mod_spec,
                  pl.BlockSpec((1, d), lambda i: (0, 0)),
                  pl.BlockSpec((d, d_in), lambda i: (0, 0)),
                  pl.BlockSpec((tm, LANES), lambda i: (i % pos_tiles, 0)),
                  pl.BlockSpec((tm, LANES), lambda i: (i % pos_tiles, 0))],
        out_specs=(tok(d_att), tok(d_att), tok(d_att), tok(d_pool), tok(d_att), tok(d_att),
                   pl.BlockSpec((1, SUBLANES, LANES), lambda i: (i, 0, 0))),
        compiler_params=_cparams(("arbitrary",)),
        name="in_proj",
    )(x, shift, scale, g, w_in_b, cos, sin)


def _lambda_value(lam_ref, lam_init):
    lp = lam_ref[...]
    a = jnp.sum(lp[0:1] * lp[1:2], axis=1, keepdims=True)
    b = jnp.sum(lp[2:3] * lp[3:4], axis=1, keepdims=True)
    return jnp.exp(a) - jnp.exp(b) + lam_init


def _stack_maps(q):
    lane = lax.broadcasted_iota(I32, q.shape, 1)
    q1 = jnp.where(lane < HEAD_DIM, q, 0.0)
    q2 = jnp.where(lane >= HEAD_DIM, q, 0.0)
    return jnp.concatenate([q1, q2], axis=0).astype(BF16)


_NT = (((1,), (1,)), ((), ()))


_TN = (((0,), (0,)), ((), ()))


def _attn_prompt_kernel(lam_ref, g_ref, kn_ref, q_ref, k_ref, k2_ref, v_ref, o_ref, m_ref, l_ref, acc_ref,
                        *, tq, tk, lam_init):
    qi = pl.program_id(2)
    heads = q_ref.shape[1] // HEAD_W
    qqs = [_stack_maps(q_ref[:, hh * HEAD_W:(hh + 1) * HEAD_W]) for hh in range(heads)]
    m_ref[...] = jnp.full(m_ref.shape, -jnp.inf, F32)
    l_ref[...] = jnp.zeros(l_ref.shape, F32)
    acc_ref[...] = jnp.zeros(acc_ref.shape, F32)

    lane = lax.broadcasted_iota(I32, (1, LANES), 1)
    kn = kn_ref[0, 0:1, :]
    worst = jnp.zeros((1, 1), F32)
    for hh in range(heads):
        qf = qqs[hh].astype(F32)
        qn2 = jnp.max(jnp.sum(qf * qf, axis=1, keepdims=True), axis=0, keepdims=True)
        kn2 = jnp.sum(jnp.where(lane == pl.program_id(1) * heads + hh, kn, 0.0), axis=1, keepdims=True)
        worst = jnp.maximum(worst, qn2 * kn2)
    bounded = worst[0, 0] <= SCORE_BOUND * SCORE_BOUND

    def run(fixed_reference):
        def block(j, masked):
            rows = pl.ds(pl.multiple_of(j * tk, tk), tk)
            for hh in range(heads):
                cols = slice(hh * HEAD_W, (hh + 1) * HEAD_W)

                def scores(kref):
                    st = lax.dot_general(kref[rows, cols], qqs[hh], _NT, preferred_element_type=F32)
                    if masked:
                        kpos = j * tk + lax.broadcasted_iota(I32, st.shape, 0)
                        c = lax.broadcasted_iota(I32, st.shape, 1)
                        qpos = qi * tq + jnp.where(c >= tq, c - tq, c)
                        st = jnp.where(kpos <= qpos, st, -jnp.inf)
                    return st

                if fixed_reference:
                    p = jnp.exp2(scores(k_ref))
                    l_ref[hh] = l_ref[hh] + jnp.sum(p, axis=0, keepdims=True)
                    acc_ref[hh] = acc_ref[hh] + lax.dot_general(v_ref[rows, cols], p.astype(BF16), _TN,
                                                                preferred_element_type=F32)
                else:
                    m_prev = m_ref[hh]
                    m_new = jnp.maximum(m_prev, jnp.max(scores(k_ref), axis=0, keepdims=True))
                    p = jnp.exp2(scores(k2_ref) - m_new)
                    alpha = jnp.exp2(m_prev - m_new)
                    l_ref[hh] = alpha * l_ref[hh] + jnp.sum(p, axis=0, keepdims=True)
                    pv = lax.dot_general(v_ref[rows, cols], p.astype(BF16), _TN, preferred_element_type=F32)
                    acc_ref[hh] = acc_ref[hh] * alpha + pv
                    m_ref[hh] = m_new

        n_full = (qi * tq) // tk

        def body(j, carry):
            block(j, False)
            return carry

        lax.fori_loop(0, n_full, body, 0)
        block(n_full, True)

    @pl.when(bounded)
    def _():
        run(True)

    @pl.when(jnp.logical_not(bounded))
    def _():
        run(False)

    lam = _lambda_value(lam_ref, lam_init)
    for hh in range(heads):
        l = l_ref[hh]
        acc = acc_ref[hh]
        o = acc[:, :tq] / l[:, :tq] - lam * (acc[:, tq:] / l[:, tq:])
        ms = jnp.mean(o * o, axis=0, keepdims=True)
        a = o * lax.rsqrt(ms + NORM_EPS) * g_ref[...] * (1.0 - lam_init)
        o_ref[:, hh * HEAD_W:(hh + 1) * HEAD_W] = a.T.astype(o_ref.dtype)


def _attn_prompt(q, kb, vb, lamp, g_col, k_norm2, *, batch, seq, lam_init):
    n, d_att = q.shape
    nh = d_att // HEAD_W
    hp = 4 if nh % 4 == 0 else 1
    tq = min(256, seq)
    tk = min(512, seq)
    nq = seq // tq
    return pl.pallas_call(
        functools.partial(_attn_prompt_kernel, tq=tq, tk=tk, lam_init=lam_init),
        out_shape=jax.ShapeDtypeStruct((n, d_att), BF16),
        grid=(batch, nh // hp, nq),
        in_specs=[pl.BlockSpec((8, LANES), lambda b, h, i: (0, 0)),
                  pl.BlockSpec((HEAD_W, 1), lambda b, h, i: (0, 0)),
                  pl.BlockSpec((1, SUBLANES, LANES), lambda b, h, i: (b, 0, 0)),
                  pl.BlockSpec((tq, hp * HEAD_W), lambda b, h, i: (b * nq + i, h)),
                  pl.BlockSpec((seq, hp * HEAD_W), lambda b, h, i: (b, h)),
                  pl.BlockSpec((seq, hp * HEAD_W), lambda b, h, i: (b, h)),
                  pl.BlockSpec((seq, hp * HEAD_W), lambda b, h, i: (b, h))],
        out_specs=pl.BlockSpec((tq, hp * HEAD_W), lambda b, h, i: (b * nq + i, h)),
        scratch_shapes=[pltpu.VMEM((hp, 1, 2 * tq), F32), pltpu.VMEM((hp, 1, 2 * tq), F32),
                        pltpu.VMEM((hp, HEAD_W, 2 * tq), F32)],
        compiler_params=_cparams(("arbitrary", "arbitrary", "arbitrary")),
        name="attn_prompt",
    )(lamp, g_col, k_norm2, q, kb, kb, vb)


def _attn_sample_kernel(pt_ref, lam_ref, g_ref, q_ref, kn_ref, vn_ref, *rest, n_heads, pages, t_new, lam_init):
    k_pages = rest[:pages]
    v_pages = rest[pages:2 * pages]
    o_ref = rest[2 * pages]
    m_ref, l_ref, acc_ref = rest[2 * pages + 1:]
    c = pl.program_id(1)
    rows_h = 2 * t_new
    assert n_heads == SUBLANES and n_heads * rows_h == LANES

    @pl.when(c == 0)
    def _():
        m_ref[...] = jnp.full(m_ref.shape, -jnp.inf, F32)
        l_ref[...] = jnp.zeros(l_ref.shape, F32)
        acc_ref[...] = jnp.zeros(acc_ref.shape, F32)

    q = q_ref[...]
    q_all = jnp.concatenate([_stack_maps(q[:, hh * HEAD_W:(hh + 1) * HEAD_W]) for hh in range(n_heads)], axis=0)
    sub = lax.broadcasted_iota(I32, (SUBLANES, LANES), 0)
    lane = lax.broadcasted_iota(I32, (SUBLANES, LANES), 1)
    own_head = (lane // rows_h) == sub

    def update(k_rows, v_rows, valid):
        n_pos = k_rows.shape[0] // n_heads
        r = lax.dot_general(k_rows.astype(BF16), q_all, _NT, preferred_element_type=F32)
        r = r.reshape(n_pos, n_heads, LANES)
        if valid is not None:
            r = jnp.where(valid, r, -jnp.inf)
        m_old = m_ref[...]
        m_new = jnp.maximum(m_old, jnp.max(r, axis=0))
        p = jnp.exp2(r - m_new[None])
        alpha = jnp.exp2(m_old - m_new)
        l_ref[...] = alpha * l_ref[...] + jnp.sum(p, axis=0)
        p_own = jnp.where(own_head[None], p, 0.0).reshape(n_pos * n_heads, LANES).astype(BF16)
        alpha_row = jnp.sum(jnp.where(own_head, alpha, 0.0), axis=0, keepdims=True)
        pv = lax.dot_general(v_rows.astype(BF16), p_own, _TN, preferred_element_type=F32)
        acc_ref[...] = acc_ref[...] * alpha_row + pv
        m_ref[...] = m_new

    for i in range(pages):
        update(k_pages[i][0], v_pages[i][0], None)

    @pl.when(c == pl.num_programs(1) - 1)
    def _():
        t_key = lax.broadcasted_iota(I32, (t_new, n_heads, LANES), 0)
        t_query = lax.broadcasted_iota(I32, (t_new, n_heads, LANES), 2) % t_new
        update(kn_ref[0], vn_ref[0], t_key <= t_query)
        l_row = jnp.sum(jnp.where(own_head, l_ref[...], 0.0), axis=0, keepdims=True)
        o_all = (acc_ref[...] / l_row).T
        lam = _lambda_value(lam_ref, lam_init)
        for hh in range(n_heads):
            rows = o_all[hh * rows_h:(hh + 1) * rows_h]
            o = rows[:t_new] - lam * rows[t_new:]
            ms = jnp.mean(o * o, axis=1, keepdims=True)
            o_ref[:, hh * HEAD_W:(hh + 1) * HEAD_W] = o * lax.rsqrt(ms + NORM_EPS) * g_ref[...] * (1.0 - lam_init)


def _attn_sample(page_table, q, k_new, v_new, cache_k, cache_v, lamp, g, *, t_new, lam_init):
    n, d_att = q.shape
    nh = d_att // HEAD_W
    db, n_pages = page_table.shape
    n_pool, page = cache_k.shape[0], cache_k.shape[1]
    pages = min(16, n_pages)
    ck = cache_k.reshape(n_pool, page * nh, HEAD_W)
    cv = cache_v.reshape(n_pool, page * nh, HEAD_W)

    def page_spec(i):
        return pl.BlockSpec((1, page * nh, HEAD_W), lambda b, c, pt: (pt[b, c * pages + i], 0, 0))

    tok = pl.BlockSpec((t_new, d_att), lambda b, c, pt: (b, 0))
    new_rows = pl.BlockSpec((1, t_new * nh, HEAD_W), lambda b, c, pt: (b, 0, 0))
    k_new = k_new.reshape(db, t_new, nh, HEAD_W).reshape(db, t_new * nh, HEAD_W)
    v_new = v_new.reshape(db, t_new, nh, HEAD_W).reshape(db, t_new * nh, HEAD_W)
    grid_spec = pltpu.PrefetchScalarGridSpec(
        num_scalar_prefetch=1,
        grid=(db, n_pages // pages),
        in_specs=[pl.BlockSpec((8, LANES), lambda b, c, pt: (0, 0)),
                  pl.BlockSpec((1, HEAD_W), lambda b, c, pt: (0, 0)),
                  tok, new_rows, new_rows]
                 + [page_spec(i) for i in range(pages)] + [page_spec(i) for i in range(pages)],
        out_specs=tok,
        scratch_shapes=[pltpu.VMEM((nh, LANES), F32), pltpu.VMEM((nh, LANES), F32),
                        pltpu.VMEM((HEAD_W, LANES), F32)],
    )
    return pl.pallas_call(
        functools.partial(_attn_sample_kernel, n_heads=nh, pages=pages, t_new=t_new, lam_init=lam_init),
        out_shape=jax.ShapeDtypeStruct((n, d_att), F32),
        grid_spec=grid_spec,
        compiler_params=_cparams(("arbitrary", "arbitrary")),
        name="attn_sample",
    )(page_table, lamp, g, q, k_new, v_new, *([ck] * pages), *([cv] * pages))


def _pool_kernel(halo_ref, cur_ref, w_ref, scale_ref, o_ref, ext_ref, *, pos0, tile_pos, zero_first_halo):
    i = pl.program_id(1)
    t = cur_ref.shape[1]
    halo = halo_ref[0]
    if zero_first_halo:
        halo = jnp.where(i == 0, 0.0, halo)
    cur = cur_ref[0]
    ext_ref[0:POOL_HALO, :] = halo
    ext_ref[POOL_HALO:POOL_HALO + t, :] = cur
    pos = pos0 + i * tile_pos + lax.broadcasted_iota(I32, (t, 1), 0)
    gw = cur.shape[1] // len(POOL_WINDOWS)
    for gi, w in enumerate(POOL_WINDOWS):
        cols = slice(gi * gw, (gi + 1) * gw)
        total = cur[:, cols]
        for j in range(1, w):
            total = total + ext_ref[POOL_HALO - j:POOL_HALO - j + t, cols]
        cnt = jnp.minimum(pos + 1, w).astype(F32)
        dlt = total / cnt - cur[:, cols]
        y = jnp.dot(dlt.astype(BF16), w_ref[gi], preferred_element_type=F32)
        o_ref[0, :, cols] = (y * scale_ref[:, cols]).astype(o_ref.dtype)


def _pool_mix(u3, w_pool_b, scale, *, tile, cur_block0, pos0, zero_first_halo):
    b, rows, c = u3.shape
    n_tiles = (rows - cur_block0 * tile) // tile
    ng, gw = w_pool_b.shape[0], w_pool_b.shape[1]
    assert (cur_block0 * tile) % POOL_HALO == 0 and (tile % POOL_HALO == 0 or n_tiles == 1)

    def halo_map(bi, i):
        return (bi, jnp.maximum(((cur_block0 + i) * tile) // POOL_HALO - 1, 0), 0)

    return pl.pallas_call(
        functools.partial(_pool_kernel, pos0=pos0, tile_pos=tile, zero_first_halo=zero_first_halo),
        out_shape=jax.ShapeDtypeStruct((b, n_tiles * tile, c), BF16),
        grid=(b, n_tiles),
        in_specs=[pl.BlockSpec((1, POOL_HALO, c), halo_map),
                  pl.BlockSpec((1, tile, c), lambda bi, i: (bi, cur_block0 + i, 0)),
                  pl.BlockSpec((ng, gw, gw), lambda bi, i: (0, 0, 0)),
                  pl.BlockSpec((1, c), lambda bi, i: (0, 0))],
        out_specs=pl.BlockSpec((1, tile, c), lambda bi, i: (bi, i, 0)),
        scratch_shapes=[pltpu.VMEM((POOL_HALO + tile, c), F32)],
        compiler_params=_cparams(("arbitrary", "arbitrary")),
        name="pool_mix",
    )(u3, u3, w_pool_b, scale)


def _split_bf16(x):
    hi = x.astype(BF16)
    lo = (x - hi.astype(F32)).astype(BF16)
    return hi, lo


def _round_up_f32(x, m):
    return jnp.floor((x + (m - 1.0)) * (1.0 / m)) * m


def _mix_out_kernel(*refs, n_experts, aliased):
    (x_ref, a_ref, pm_ref, gate_ref, shift_ref, scale_ref, g_ref, wo_ref, wr_ref, br_ref) = refs[:10]
    x2_ref, h2_ref, pk_ref, gt_ref, tc_ref = refs[10 + aliased:]
    d_att = a_ref.shape[1]
    mix = (jnp.dot(a_ref[...].astype(BF16), wo_ref[0:d_att, :], preferred_element_type=F32)
           + jnp.dot(pm_ref[...], wo_ref[d_att:, :], preferred_element_type=F32))
    x2 = x_ref[...] + gate_ref[0] * mix
    x2_ref[...] = x2
    h2 = _modulated_norm(x2, g_ref[...], shift_ref[0], scale_ref[0])
    h2_ref[...] = h2.astype(BF16)
    hh, hl = _split_bf16(h2)
    wh, wl = _split_bf16(wr_ref[...])
    logits = (jnp.dot(hh, wh, preferred_element_type=F32) + jnp.dot(hl, wh, preferred_element_type=F32)
              + jnp.dot(hh, wl, preferred_element_type=F32)) + br_ref[...]
    t = logits.shape[0]
    lane = lax.broadcasted_iota(I32, logits.shape, 1)
    lanef = lane.astype(F32)
    work = jnp.where(lane < n_experts, logits, -jnp.inf)
    vals, ids = [], []
    for _ in range(TOP_K):
        mx = jnp.max(work, axis=1, keepdims=True)
        ix = jnp.min(jnp.where(work == mx, lanef, float(LANES)), axis=1, keepdims=True)
        vals.append(mx)
        ids.append(ix)
        work = jnp.where(lanef == ix, -jnp.inf, work)
    es = [jnp.exp(v - vals[0]) for v in vals]
    den = es[0]
    for e in es[1:]:
        den = den + e
    sel = jnp.zeros(logits.shape, F32)
    for k in range(TOP_K):
        sel = jnp.where(lanef == ids[k], 1.0, sel)
    r = lax.broadcasted_iota(I32, (t, t), 0)
    c = lax.broadcasted_iota(I32, (t, t), 1)
    earlier = jnp.where(c < r, 1.0, 0.0).astype(BF16)
    local_rank = jnp.dot(earlier, sel.astype(BF16), preferred_element_type=F32)
    cnt = jnp.sum(sel, axis=0, keepdims=True)
    cnt8 = jnp.broadcast_to(_round_up_f32(cnt, float(SUBLANES)), (SUBLANES, LANES))
    er = lax.broadcasted_iota(I32, (LANES, LANES), 0)
    ec = lax.broadcasted_iota(I32, (LANES, LANES), 1)
    before = jnp.where(er < ec, 1.0, 0.0).astype(BF16)
    chunk_off = jnp.dot(cnt8.astype(BF16), before, preferred_element_type=F32)[0:1, :]
    pos = local_rank + chunk_off
    pk_out = jnp.zeros(logits.shape, F32)
    gt_out = jnp.zeros(logits.shape, F32)
    for k in range(TOP_K):
        pk = jnp.sum(jnp.where(lanef == ids[k], pos, 0.0), axis=1, keepdims=True)
        pk_out = jnp.where(lane == k, pk, pk_out)
        gt_out = jnp.where(lane == k, es[k] / den, gt_out)
    pk_ref[...] = pk_out
    gt_ref[...] = gt_out
    tc_ref[0] = jnp.broadcast_to(cnt, (SUBLANES, LANES))


def _mix_out(x, a, pm, gate, shift, scale, g, w_out_b, w_router_p, b_router_p, *, n_total, tile0,
             tiles_per_seq, n_experts, prev=None):
    n, d = x.shape
    tm = min(TOK_TILE, n)
    d_att = a.shape[1]
    mod_rows = gate.shape[1]
    mod_spec = pl.BlockSpec((1, mod_rows, d), lambda i: (i // tiles_per_seq, 0, 0))
    tok = lambda c: pl.BlockSpec((tm, c), lambda i: (i, 0))
    out_tok = lambda c: pl.BlockSpec((tm, c), lambda i: (tile0 + i, 0))
    out_shape = (jax.ShapeDtypeStruct((n_total, d), F32), jax.ShapeDtypeStruct((n_total, d), BF16),
                 jax.ShapeDtypeStruct((n_total, LANES), F32), jax.ShapeDtypeStruct((n_total, LANES), F32),
                 jax.ShapeDtypeStruct((n_total // tm, SUBLANES, LANES), F32))
    in_specs = [tok(d), tok(d_att), tok(pm.shape[1]), mod_spec, mod_spec, mod_spec,
                pl.BlockSpec((1, d), lambda i: (0, 0)),
                pl.BlockSpec(w_out_b.shape, lambda i: (0, 0)),
                pl.BlockSpec(w_router_p.shape, lambda i: (0, 0)),
                pl.BlockSpec((1, LANES), lambda i: (0, 0))]
    args = [x, a, pm, gate, shift, scale, g, w_out_b, w_router_p, b_router_p]
    aliases = {}
    n_alias = 0
    if prev is not None:
        n_alias = len(prev)
        in_specs += [pl.BlockSpec(memory_space=pl.ANY)] * n_alias
        aliases = {len(args) + k: k for k in range(n_alias)}
        args += list(prev)
    return pl.pallas_call(
        functools.partial(_mix_out_kernel, n_experts=n_experts, aliased=n_alias),
        out_shape=out_shape,
        grid=(n // tm,),
        in_specs=in_specs,
        out_specs=(out_tok(d), out_tok(d), out_tok(LANES), out_tok(LANES),
                   pl.BlockSpec((1, SUBLANES, LANES), lambda i: (tile0 + i, 0, 0))),
        input_output_aliases=aliases,
        compiler_params=_cparams(("arbitrary",)),
        name="mix_out_router",
    )(*args)


def _chunk_sizes(tile):
    sizes = []
    s = tile
    while s >= SUBLANES:
        sizes.append(s)
        s //= 2
    return sizes


def _for_each_chunk_piece(cnt_ref, base_ref, i, n_experts, tile, fn):
    def per_expert(e, off):
        l8 = (cnt_ref[i * n_experts + e] + (SUBLANES - 1)) & (-SUBLANES)
        dst0 = base_ref[i * n_experts + e]
        done = jnp.int32(0)
        for sz in _chunk_sizes(tile):
            take = l8 & sz

            @pl.when(take != 0)
            def _(done=done, sz=sz):
                fn(pl.multiple_of(off + done, SUBLANES), pl.multiple_of(dst0 + done, SUBLANES), sz)

            done = done + take
        return off + l8

    return lax.fori_loop(0, n_experts, per_expert, jnp.int32(0))


def _pack_halves(x):
    c = x.shape[1] // 2
    lo = lax.bitcast_convert_type(x[:, :c], U32) >> 16
    hi = lax.bitcast_convert_type(x[:, c:], U32) & jnp.uint32(0xFFFF0000)
    return lo | hi


def _unpack_halves(w):
    lo = lax.bitcast_convert_type(w << 16, F32).astype(BF16)
    hi = lax.bitcast_convert_type(w & jnp.uint32(0xFFFF0000), F32).astype(BF16)
    return lo, hi


def _one_hot_rows(pk, rows, weights=None):
    t = pk.shape[0]
    col = lax.broadcasted_iota(I32, (t, rows), 1).astype(F32)
    out = jnp.zeros((t, rows), F32)
    for k in range(TOP_K):
        w = 1.0 if weights is None else weights[:, k:k + 1]
        out = jnp.where(col == pk[:, k:k + 1], w, out)
    return out.astype(BF16)


def _dispatch_kernel(cnt_ref, base_ref, tail_ref, has_ref, h_ref, pk_ref, xs_hbm, sorted_ref, zero_ref, sem,
                     zero_sem, *, n_experts, tile):
    i = pl.program_id(0)
    last = pl.num_programs(0) - 1

    def zero_copy(e):
        return pltpu.make_async_copy(
            zero_ref, xs_hbm.at[pl.ds(pl.multiple_of(tail_ref[e], EXP_TILE), EXP_TILE)], zero_sem)

    @pl.when(i == 0)
    def _():
        zero_ref[...] = jnp.zeros(zero_ref.shape, U32)
        for e in range(n_experts):
            @pl.when(has_ref[e] == 1)
            def _(e=e):
                zero_copy(e).start()
        for e in range(n_experts):
            @pl.when(has_ref[e] == 1)
            def _(e=e):
                zero_copy(e).wait()

    def copy(step, src, dst, sz):
        slot = step & 1
        return pltpu.make_async_copy(sorted_ref.at[slot, pl.ds(src, sz)], xs_hbm.at[pl.ds(dst, sz)], sem.at[slot])

    onehot = _one_hot_rows(pk_ref[...], sorted_ref.shape[1])
    sorted_ref[i & 1] = _pack_halves(lax.dot_general(onehot, h_ref[...], _TN, preferred_element_type=F32))
    _for_each_chunk_piece(cnt_ref, base_ref, i, n_experts, tile, lambda s, d, sz: copy(i, s, d, sz).start())

    @pl.when(i > 0)
    def _():
        _for_each_chunk_piece(cnt_ref, base_ref, i - 1, n_experts, tile,
                              lambda s, d, sz: copy(i - 1, s, d, sz).wait())

    @pl.when(i == last)
    def _():
        _for_each_chunk_piece(cnt_ref, base_ref, i, n_experts, tile, lambda s, d, sz: copy(i, s, d, sz).wait())


def _sorted_rows(tile, n_experts):
    rows = tile * TOP_K + n_experts * (SUBLANES - 1)
    return -(-rows // LANES) * LANES


def _dispatch(h2, pk4, cnt_flat, base_flat, tail_rows, has_tile, *, r_max, n_experts):
    n, d = h2.shape
    tile = min(TOK_TILE, n)
    grid_spec = pltpu.PrefetchScalarGridSpec(
        num_scalar_prefetch=4,
        grid=(n // tile,),
        in_specs=[pl.BlockSpec((tile, d), lambda i, *_: (i, 0)),
                  pl.BlockSpec((tile, LANES), lambda i, *_: (i, 0))],
        out_specs=pl.BlockSpec(memory_space=pl.ANY),
        scratch_shapes=[pltpu.VMEM((2, _sorted_rows(tile, n_experts), d // 2), U32),
                        pltpu.VMEM((EXP_TILE, d // 2), U32), pltpu.SemaphoreType.DMA((2,)),
                        pltpu.SemaphoreType.DMA],
    )
    return pl.pallas_call(
        functools.partial(_dispatch_kernel, n_experts=n_experts, tile=tile),
        out_shape=jax.ShapeDtypeStruct((r_max, d // 2), U32),
        grid_spec=grid_spec,
        compiler_params=_cparams(("arbitrary",)),
        name="dispatch_rows",
    )(cnt_flat, base_flat, tail_rows, has_tile, h2, pk4)


def _expert_kernel(st_e, st_j, st_n, g_t0, g_gs, g_valid,
                   xs_hbm, wg_ref, wu_ref, wd_ref, bg_ref, bu_ref, bd_ref, ys_hbm,
                   xbuf, actbuf, wgb, wub, wdb, stage_in, stage_out, pending, sem_in, sem_out,
                   *, n_j, n_n, tm, tf):
    s = pl.program_id(0)
    steps = n_j + n_n
    q = s // steps
    ph = s - q * steps
    valid = g_valid[q] == 1
    gs = g_gs[q]
    row0 = g_t0[q] * tm

    @pl.when(s == 0)
    def _():
        pending[0] = 0
        pending[1] = 0

    def in_copy(first_row, i):
        slot = i & 1
        return pltpu.make_async_copy(xs_hbm.at[pl.ds(pl.multiple_of(first_row + i * tm, tm), tm)],
                                     stage_in.at[slot], sem_in.at[slot])

    def stage_to_xbuf(i):
        rows = pl.ds(pl.multiple_of(i * tm, tm), tm)
        lo, hi = _unpack_halves(stage_in[i & 1])
        half = lo.shape[1]
        xbuf[rows, 0:half] = lo
        xbuf[rows, half:2 * half] = hi

    @pl.when(s == 0)
    def _():
        in_copy(row0, 0).start()

        def load(i, carry):
            @pl.when(i + 1 < gs)
            def _():
                in_copy(row0, i + 1).start()

            in_copy(row0, i).wait()
            stage_to_xbuf(i)
            return carry

        lax.fori_loop(0, gs, load, 0)

    q_next = jnp.minimum(q + 1, g_gs.shape[0] - 1)
    gs_next = jnp.where(q + 1 < g_gs.shape[0], g_gs[q_next], 0)
    row0_next = g_t0[q_next] * tm
    fetch_tiles = [2 * (ph - n_j), 2 * (ph - n_j) + 1]

    n_pairs = gs // 2
    has_tail = (gs & 1) == 1
    tail_row = pl.multiple_of(n_pairs * 2 * tm, tm)

    @pl.when(jnp.logical_and(valid, ph < n_j))
    def _():
        wgb[...] = wg_ref[0].astype(BF16)
        wub[...] = wu_ref[0].astype(BF16)

        def tile(r0, size):
            rows = pl.ds(r0, size)
            x = xbuf[rows, :]
            gate = jnp.dot(x, wgb[...], preferred_element_type=F32) + bg_ref[0]
            up = jnp.dot(x, wub[...], preferred_element_type=F32) + bu_ref[0]
            gate = jnp.minimum(gate, SWIGLU_LIMIT)
            up = jnp.clip(up, -SWIGLU_LIMIT, SWIGLU_LIMIT)
            act = (up + 1.0) * (gate * jax.nn.sigmoid(SWIGLU_ALPHA * gate))
            actbuf[ph, rows, :] = act.astype(BF16)

        def pair(i, carry):
            tile(pl.multiple_of(i * 2 * tm, 2 * tm), 2 * tm)
            return carry

        lax.fori_loop(0, n_pairs, pair, 0)

        @pl.when(has_tail)
        def _():
            tile(tail_row, tm)

    @pl.when(jnp.logical_and(valid, ph >= n_j))
    def _():
        for t in fetch_tiles:
            @pl.when(t < gs_next)
            def _(t=t):
                in_copy(row0_next, t).start()

        wdb[...] = wd_ref[0].astype(BF16)
        col0 = pl.multiple_of((ph - n_j) * (tf // 2), tf // 2)

        def out_copy(slot, r0, size):
            return pltpu.make_async_copy(
                stage_out.at[slot, pl.ds(0, size)],
                ys_hbm.at[pl.ds(pl.multiple_of(row0 + r0, tm), size), pl.ds(col0, tf // 2)],
                sem_out.at[slot])

        def wait_slot(slot):
            for n_tiles in (1, 2):
                @pl.when(pending[slot] == n_tiles)
                def _(n_tiles=n_tiles):
                    out_copy(slot, 0, n_tiles * tm).wait()
                    pending[slot] = 0

        def tile(r0, size, slot):
            rows = pl.ds(r0, size)
            wait_slot(slot)
            y = bd_ref[0] + jnp.dot(actbuf[0, rows, :], wdb[0:tf, :], preferred_element_type=F32)
            for j in range(1, n_j):
                y = y + jnp.dot(actbuf[j, rows, :], wdb[j * tf:(j + 1) * tf, :], preferred_element_type=F32)
            stage_out[slot, 0:size] = _pack_halves(y.astype(BF16).astype(F32))
            out_copy(slot, r0, size).start()
            pending[slot] = size // tm

        def pair(i, carry):
            tile(pl.multiple_of(i * 2 * tm, 2 * tm), 2 * tm, i & 1)
            return carry

        lax.fori_loop(0, n_pairs, pair, 0)

        @pl.when(has_tail)
        def _():
            tile(tail_row, tm, n_pairs & 1)

        for t in fetch_tiles:
            @pl.when(t < gs_next)
            def _(t=t):
                in_copy(row0_next, t).wait()
                stage_to_xbuf(t)

        @pl.when(s == pl.num_programs(0) - 1)
        def _():
            wait_slot(0)
            wait_slot(1)


def _expert_ffn(tables, xs, w_gate, b_gate, w_up, b_up, w_down, b_down, *, n_steps):
    st_e, st_j, st_n, g_t0, g_gs, g_valid = tables
    r_max = xs.shape[0]
    n_exp, d, d_ff = w_gate.shape
    tf = min(FF_TILE, d_ff, d)
    n_j = d_ff // tf
    n_n = d // tf
    tm = EXP_TILE
    rows_g = EXP_GROUP * tm
    assert 2 * n_n >= EXP_GROUP, "phase-2 steps fetch two row tiles of the next group each"
    grid_spec = pltpu.PrefetchScalarGridSpec(
        num_scalar_prefetch=6,
        grid=(n_steps,),
        in_specs=[pl.BlockSpec(memory_space=pl.ANY),
                  pl.BlockSpec((1, d, tf), lambda s, e, j, n, *_: (e[s], 0, j[s])),
                  pl.BlockSpec((1, d, tf), lambda s, e, j, n, *_: (e[s], 0, j[s])),
                  pl.BlockSpec((1, d_ff, tf), lambda s, e, j, n, *_: (e[s], 0, n[s])),
                  pl.BlockSpec((1, 1, tf), lambda s, e, j, n, *_: (e[s], 0, j[s])),
                  pl.BlockSpec((1, 1, tf), lambda s, e, j, n, *_: (e[s], 0, j[s])),
                  pl.BlockSpec((1, 1, tf), lambda s, e, j, n, *_: (e[s], 0, n[s]))],
        out_specs=pl.BlockSpec(memory_space=pl.ANY),
        scratch_shapes=[pltpu.VMEM((rows_g, d), BF16),
                        pltpu.VMEM((n_j, rows_g, tf), BF16),
                        pltpu.VMEM((d, tf), BF16), pltpu.VMEM((d, tf), BF16), pltpu.VMEM((d_ff, tf), BF16),
                        pltpu.VMEM((2, tm, d // 2), U32), pltpu.VMEM((2, 2 * tm, tf // 2), U32),
                        pltpu.SMEM((2,), I32),
                        pltpu.SemaphoreType.DMA((2,)), pltpu.SemaphoreType.DMA((2,))],
    )
    return pl.pallas_call(
        functools.partial(_expert_kernel, n_j=n_j, n_n=n_n, tm=tm, tf=tf),
        out_shape=jax.ShapeDtypeStruct((r_max, d // 2), U32),
        grid_spec=grid_spec,
        compiler_params=_cparams(("arbitrary",)),
        name="expert_ffn",
    )(st_e, st_j, st_n, g_t0, g_gs, g_valid, xs, w_gate, w_up, w_down,
      b_gate.reshape(n_exp, 1, d_ff), b_up.reshape(n_exp, 1, d_ff), b_down.reshape(n_exp, 1, d))


def _combine_kernel(cnt_ref, base_ref, ys_hbm, x2_ref, pk_ref, gt_ref, gate_p_ref, gate_s_ref, g_ref,
                    yp_ref, ysm_ref, rows_ref, sem, *, n_experts, tile, n_prompt_tiles, out_tile):
    i = pl.program_id(0)
    last = pl.num_programs(0) - 1

    def copy(step, dst, src, sz):
        slot = step & 1
        return pltpu.make_async_copy(ys_hbm.at[pl.ds(src, sz)], rows_ref.at[slot, pl.ds(dst, sz)], sem.at[slot])

    def fetch(step):
        _for_each_chunk_piece(cnt_ref, base_ref, step, n_experts, tile,
                              lambda s, d, sz: copy(step, s, d, sz).start())

    @pl.when(i == 0)
    def _():
        rows_ref[...] = jnp.zeros(rows_ref.shape, U32)
        fetch(i)

    @pl.when(i < last)
    def _():
        fetch(i + 1)

    _for_each_chunk_piece(cnt_ref, base_ref, i, n_experts, tile, lambda s, d, sz: copy(i, s, d, sz).wait())
    weights = _one_hot_rows(pk_ref[...], rows_ref.shape[1], gt_ref[...])
    lo, hi = _unpack_halves(rows_ref[i & 1])
    y_lo = jnp.dot(weights, lo, preferred_element_type=F32)
    y_hi = jnp.dot(weights, hi, preferred_element_type=F32)
    half = out_tile // 2
    pieces = []
    for n in range(x2_ref.shape[1] // out_tile):
        pieces += [y_lo[:, n * half:(n + 1) * half], y_hi[:, n * half:(n + 1) * half]]
    y = jnp.concatenate(pieces, axis=1)
    is_prompt = i < n_prompt_tiles
    gate = jnp.where(is_prompt, gate_p_ref[0], gate_s_ref[...])
    x3 = x2_ref[...] + gate * y
    ms = jnp.mean(x3 * x3, axis=-1, keepdims=True)
    out = x3 * lax.rsqrt(ms + NORM_EPS) * g_ref[...]

    @pl.when(is_prompt)
    def _():
        yp_ref[...] = out

    @pl.when(jnp.logical_not(is_prompt))
    def _():
        ysm_ref[...] = out


def _combine(ys, cnt_flat, base_flat, x2, pk4, gt, gate_p, gate_s, final_g, *, n_prompt, tiles_per_seq, n_experts,
             out_tile):
    n, d = x2.shape
    tile = min(TOK_TILE, n_prompt)
    n_s = n - n_prompt
    assert n_s == tile and n_prompt % tile == 0
    npt = n_prompt // tile
    grid_spec = pltpu.PrefetchScalarGridSpec(
        num_scalar_prefetch=2,
        grid=(n // tile,),
        in_specs=[pl.BlockSpec(memory_space=pl.ANY),
                  pl.BlockSpec((tile, d), lambda i, *_: (i, 0)),
                  pl.BlockSpec((tile, LANES), lambda i, *_: (i, 0)),
                  pl.BlockSpec((tile, LANES), lambda i, *_: (i, 0)),
                  pl.BlockSpec((1, 1, d), lambda i, *_: (jnp.minimum(i, npt - 1) // tiles_per_seq, 0, 0)),
                  pl.BlockSpec((tile, d), lambda i, *_: (0, 0)),
                  pl.BlockSpec((1, d), lambda i, *_: (0, 0))],
        out_specs=(pl.BlockSpec((tile, d), lambda i, *_: (jnp.minimum(i, npt - 1), 0)),
                   pl.BlockSpec((tile, d), lambda i, *_: (0, 0))),
        scratch_shapes=[pltpu.VMEM((2, _sorted_rows(tile, n_experts), d // 2), U32),
                        pltpu.SemaphoreType.DMA((2,))],
    )
    return pl.pallas_call(
        functools.partial(_combine_kernel, n_experts=n_experts, tile=tile, n_prompt_tiles=npt, out_tile=out_tile),
        out_shape=(jax.ShapeDtypeStruct((n_prompt, d), F32), jax.ShapeDtypeStruct((n_s, d), F32)),
        grid_spec=grid_spec,
        compiler_params=_cparams(("arbitrary",)),
        name="combine_norm",
    )(cnt_flat, base_flat, ys, x2, pk4, gt, gate_p, gate_s, final_g)


def _expert_tables(tile_cnt, *, tile, n_j, n_n):
    n_tiles, n_experts = tile_cnt.shape
    tm, grp = EXP_TILE, EXP_GROUP
    cnt8 = (tile_cnt + (SUBLANES - 1)) // SUBLANES * SUBLANES
    rows_e = jnp.sum(cnt8, axis=0)
    max_rows = n_tiles * tile * TOP_K + n_experts * n_tiles * (SUBLANES - 1)
    t_max = -(-max_rows // tm) + n_experts
    ng_max = n_experts + t_max // grp
    ntile = (rows_e + tm - 1) // tm
    tile_start = jnp.cumsum(ntile) - ntile
    pstart = tile_start * tm
    base = pstart[None, :] + jnp.cumsum(cnt8, axis=0) - cnt8
    ng = (ntile + grp - 1) // grp
    cg = jnp.cumsum(ng)
    n_groups = cg[-1]
    q = jnp.arange(ng_max, dtype=I32)
    eq = jnp.minimum(jnp.sum((q[:, None] >= cg[None, :]).astype(I32), axis=1), n_experts - 1)
    onehot = eq[:, None] == jnp.arange(n_experts, dtype=I32)[None, :]

    def pick(v):
        return jnp.sum(jnp.where(onehot, v[None, :], 0), axis=1)

    lg = q - (pick(cg) - pick(ng))
    t0 = pick(tile_start) + lg * grp
    gs = jnp.clip(pick(ntile) - lg * grp, 0, grp)
    valid = q < n_groups
    is_last = q == jnp.maximum(n_groups - 1, 0)
    eq = jnp.where(valid, eq, jnp.sum(jnp.where(is_last, eq, 0)))
    t0 = jnp.where(valid, t0, jnp.sum(jnp.where(is_last, t0, 0)))
    gs = jnp.where(valid, gs, 0)
    steps = n_j + n_n
    ph = jnp.tile(jnp.arange(steps, dtype=I32), ng_max)
    vs = jnp.repeat(valid, steps)
    st_e = jnp.repeat(eq, steps)
    st_j = jnp.where(vs, jnp.minimum(ph, n_j - 1), n_j - 1)
    st_n = jnp.where(vs, jnp.maximum(ph - n_j, 0), n_n - 1)
    tail = jnp.maximum(pstart + (ntile - 1) * tm, 0)
    has = (ntile > 0).astype(I32)
    tabs = tuple(a.astype(I32) for a in (st_e, st_j, st_n, t0, gs, valid))
    return (tabs, tile_cnt.reshape(-1).astype(I32), base.reshape(-1).astype(I32), tail.astype(I32), has,
            t_max * tm, (n_groups * steps).astype(I32))


def _rope_tables(pos):
    inv = 1.0 / (ROPE_THETA ** (jnp.arange(0, HEAD_DIM, 2, dtype=F32) / HEAD_DIM))
    ang = pos.astype(F32)[:, None] * inv[None, :]
    reps = LANES // (HEAD_DIM // 2)
    return jnp.tile(jnp.cos(ang), (1, reps)), jnp.tile(jnp.sin(ang), (1, reps))


def kernel(x_prompt, x_sample, cache_k, cache_v, state_pool, page_table, c_prompt, c_sample, w_ada, b_ada, norm1_g, norm2_g, w_in, lam_q1, lam_k1, lam_q2, lam_k2, subln_g, w_pool, pool_scale, w_out, w_router, b_router, w_gate, b_gate, w_up, b_up, w_down, b_down, final_g):
    B, S, D = x_prompt.shape
    DB, T, _ = x_sample.shape
    depth = w_ada.shape[0]
    page = cache_k.shape[2]
    past = page_table.shape[1] * page
    d_att = D // 2
    n_heads = d_att // HEAD_W
    d_pool = w_in.shape[2] - 3 * d_att
    n_experts = w_router.shape[2]
    state_len = state_pool.shape[2]
    n_p, n_s = B * S, DB * T
    n_all = n_p + n_s
    tm = min(TOK_TILE, n_p)
    tiles_per_seq = S // tm
    assert depth == 1, "single-layer step"
    assert n_s == tm and S % tm == 0 and T % 8 == 0 and state_len < POOL_HALO <= tm

    cos_p, sin_p = _rope_tables(jnp.arange(S))
    cos_s, sin_s = _rope_tables(jnp.tile(past + jnp.arange(T), DB))

    l = 0
    lam_init = 0.8 - 0.6 * math.exp(-0.3 * l)
    lamp = jnp.zeros((8, LANES), F32)
    for r, vec in enumerate((lam_q1[l], lam_k1[l], lam_q2[l], lam_k2[l])):
        lamp = lamp.at[r, :HEAD_DIM].set(vec.astype(F32))
    subg = subln_g[l].reshape(1, HEAD_W)

    rows_c = -(-(B + DB) // 8) * 8
    c_all = jnp.zeros((rows_c, D), F32).at[:B].set(c_prompt).at[B:B + DB].set(c_sample)
    m_all = _adaln(c_all, w_ada[l], b_ada[l])
    mods_p = [m_all[:B, k * D:(k + 1) * D].reshape(B, 1, D) for k in range(N_ADA)]
    mods_s = [jnp.repeat(m_all[B:B + DB, k * D:(k + 1) * D], T, axis=0).reshape(1, n_s, D) for k in range(N_ADA)]

    w_in_b = _cast_bf16(w_in[l], 256)
    w_out_b = _cast_bf16(w_out[l], 256)
    ng, gw = w_pool.shape[1], w_pool.shape[2]
    w_pool_b = _cast_bf16(w_pool[l].reshape(ng * gw, gw), ng * gw).reshape(ng, gw, gw)
    g1 = norm1_g[l].reshape(1, D)
    g2 = norm2_g[l].reshape(1, D)
    pscale = pool_scale[l].reshape(1, d_pool)
    w_router_p = jnp.zeros((D, LANES), F32).at[:, :n_experts].set(w_router[l])
    b_router_p = jnp.zeros((1, LANES), F32).at[0, :n_experts].set(b_router[l].astype(F32))

    xp = x_prompt.reshape(n_p, D)
    q_p, k_p, v_p, u_p, kb_p, vb_p, kn_p = _project(xp, mods_p[0], mods_p[1], g1, w_in_b, cos_p, sin_p,
                                                    tiles_per_seq=tiles_per_seq, pos_tiles=tiles_per_seq)
    kn_seq = jnp.max(kn_p.reshape(B, tiles_per_seq, SUBLANES, LANES), axis=1)
    a_p = _attn_prompt(q_p, kb_p, vb_p, lamp, subg.reshape(HEAD_W, 1), kn_seq, batch=B, seq=S, lam_init=lam_init)
    pm_p = _pool_mix(u_p.reshape(B, S, d_pool), w_pool_b, pscale, tile=tm, cur_block0=0, pos0=0,
                     zero_first_halo=True).reshape(n_p, d_pool)

    xs_tok = x_sample.reshape(n_s, D)
    q_s, k_s, v_s, u_s, _, _, _ = _project(xs_tok, mods_s[0], mods_s[1], g1, w_in_b, cos_s, sin_s,
                                           tiles_per_seq=1, pos_tiles=1)
    a_s = _attn_sample(page_table, q_s, k_s, v_s, cache_k[l], cache_v[l], lamp, subg, t_new=T, lam_init=lam_init)
    u_ext = jnp.concatenate([jnp.zeros((DB, POOL_HALO - state_len, d_pool), F32),
                             state_pool[l].astype(F32), u_s.reshape(DB, T, d_pool)], axis=1)
    pm_s = _pool_mix(u_ext, w_pool_b, pscale, tile=T, cur_block0=POOL_HALO // T, pos0=past,
                     zero_first_halo=False).reshape(n_s, d_pool)

    outs = _mix_out(xp, a_p, pm_p, mods_p[2], mods_p[3], mods_p[4], g2, w_out_b, w_router_p, b_router_p,
                    n_total=n_all, tile0=0, tiles_per_seq=tiles_per_seq, n_experts=n_experts)
    x2, h2, pk4, gt4, tcnt = _mix_out(xs_tok, a_s, pm_s, mods_s[2], mods_s[3], mods_s[4], g2, w_out_b, w_router_p,
                                      b_router_p, n_total=n_all, tile0=n_p // tm, tiles_per_seq=1,
                                      n_experts=n_experts, prev=outs)

    d_ff = w_gate.shape[3]
    tf = min(FF_TILE, d_ff, D)
    tile_cnt = tcnt[:, 0, :n_experts].astype(I32)
    tabs, cnt_flat, base_flat, tail, has, r_max, n_steps = _expert_tables(
        tile_cnt, tile=tm, n_j=d_ff // tf, n_n=D // tf)
    xs_rows = _dispatch(h2, pk4, cnt_flat, base_flat, tail, has, r_max=r_max, n_experts=n_experts)
    ys_rows = _expert_ffn(tabs, xs_rows, w_gate[l], b_gate[l], w_up[l], b_up[l], w_down[l], b_down[l],
                          n_steps=n_steps)
    y_p, y_s = _combine(ys_rows, cnt_flat, base_flat, x2, pk4, gt4, mods_p[5], mods_s[5].reshape(n_s, D),
                        final_g.reshape(1, D), n_prompt=n_p, tiles_per_seq=tiles_per_seq, n_experts=n_experts,
                        out_tile=tf)

    n_pages_p = S // page
    k_prompt = k_p.reshape(1, B, n_pages_p, page, n_heads, HEAD_W)
    v_prompt = v_p.reshape(1, B, n_pages_p, page, n_heads, HEAD_W)
    pool_prompt = u_p.reshape(B, S, d_pool)[:, S - state_len:][None]
    k_sample = k_s.reshape(1, DB, T, n_heads, HEAD_W)
    v_sample = v_s.reshape(1, DB, T, n_heads, HEAD_W)
    pool_sample = u_ext[:, -state_len:][None]
    return (y_p.reshape(B, S, D), y_s.reshape(DB, T, D), k_prompt, v_prompt, pool_prompt,
            k_sample, v_sample, pool_sample)
```

```python
import functools
import math

import jax
import jax.numpy as jnp
from jax import lax
from jax.experimental import pallas as pl
from jax.experimental.pallas import tpu as pltpu

F32 = jnp.float32
BF16 = jnp.bfloat16
I32 = jnp.int32
U32 = jnp.uint32

HEAD_DIM = 64
HEAD_W = 2 * HEAD_DIM
POOL_WINDOWS = (2, 4, 8, 16)
POOL_HALO = 16
TOP_K = 4
SWIGLU_LIMIT = 7.0
SWIGLU_ALPHA = 1.702
ROPE_THETA = 10000.0
NORM_EPS = 1e-5
N_ADA = 6
LANES = 128
SUBLANES = 8
V7X_VMEM_LIMIT = 58 * 1024 * 1024

TOK_TILE = 256
EXP_TILE = 256
EXP_GROUP = 6
FF_TILE = 512
Q_SCALE = (HEAD_DIM ** -0.5) * math.log2(math.e)
SCORE_BOUND = 60.0


def _cparams(sem, vmem=V7X_VMEM_LIMIT):
    return pltpu.CompilerParams(dimension_semantics=sem, vmem_limit_bytes=vmem)


def _cast_kernel(x_ref, o_ref):
    o_ref[...] = x_ref[...].astype(o_ref.dtype)


def _cast_bf16(w, rows):
    r, c = w.shape
    return pl.pallas_call(
        _cast_kernel,
        out_shape=jax.ShapeDtypeStruct((r, c), BF16),
        grid=(r // rows,),
        in_specs=[pl.BlockSpec((rows, c), lambda i: (i, 0))],
        out_specs=pl.BlockSpec((rows, c), lambda i: (i, 0)),
        compiler_params=_cparams(("arbitrary",)),
        name="cast_bf16",
    )(w)


def _ada_kernel(c_ref, w_ref, b_ref, o_ref):
    c = c_ref[...]
    s = (c * jax.nn.sigmoid(c)).astype(BF16)
    o_ref[...] = jnp.dot(s, w_ref[...].astype(BF16), preferred_element_type=F32) + b_ref[...]


def _adaln(c_all, w_ada, b_ada):
    rows, d = c_all.shape
    n = w_ada.shape[1]
    tn = min(1024, n)
    return pl.pallas_call(
        _ada_kernel,
        out_shape=jax.ShapeDtypeStruct((rows, n), F32),
        grid=(n // tn,),
        in_specs=[pl.BlockSpec((rows, d), lambda j: (0, 0)),
                  pl.BlockSpec((d, tn), lambda j: (0, j)),
                  pl.BlockSpec((1, tn), lambda j: (0, j))],
        out_specs=pl.BlockSpec((rows, tn), lambda j: (0, j)),
        compiler_params=_cparams(("arbitrary",)),
        name="adaln",
    )(c_all, w_ada, b_ada.reshape(1, n))


def _modulated_norm(x, g, shift, scale):
    ms = jnp.mean(x * x, axis=-1, keepdims=True)
    return (x * lax.rsqrt(ms + NORM_EPS) * g) * (1.0 + scale) + shift


def _proj_kernel(x_ref, shift_ref, scale_ref, g_ref, w_ref, cos_ref, sin_ref,
                 q_ref, k_ref, v_ref, u_ref, kb_ref, vb_ref, kn_ref, *, d_att):
    h = _modulated_norm(x_ref[...], g_ref[...], shift_ref[0], scale_ref[0]).astype(BF16)
    cos = cos_ref[...]
    sin = sin_ref[...]
    lane = lax.broadcasted_iota(I32, cos.shape, 1)
    first_half = (lane & (HEAD_DIM - 1)) < (HEAD_DIM // 2)
    head_lane = lax.broadcasted_iota(I32, (1, LANES), 1)

    def rope(z):
        rot = jnp.where(first_half, -pltpu.roll(z, LANES - HEAD_DIM // 2, 1), pltpu.roll(z, HEAD_DIM // 2, 1))
        return z * cos + rot * sin

    zq = jnp.dot(h, w_ref[:, 0:d_att], preferred_element_type=F32)
    zk = jnp.dot(h, w_ref[:, d_att:2 * d_att], preferred_element_type=F32)
    k_norm2 = jnp.zeros((1, LANES), F32)
    for hh in range(d_att // HEAD_W):
        sl = slice(hh * HEAD_W, (hh + 1) * HEAD_W)
        q_ref[:, sl] = rope(zq[:, sl]) * Q_SCALE
        kr = rope(zk[:, sl])
        k_ref[:, sl] = kr
        kb_ref[:, sl] = kr.astype(BF16)
        n2 = jnp.max(jnp.sum(kr * kr, axis=1, keepdims=True), axis=0, keepdims=True)
        k_norm2 = jnp.where(head_lane == hh, n2, k_norm2)
    kn_ref[0] = jnp.broadcast_to(k_norm2, kn_ref.shape[1:])
    zv = jnp.dot(h, w_ref[:, 2 * d_att:3 * d_att], preferred_element_type=F32)
    v_ref[...] = zv
    vb_ref[...] = zv.astype(BF16)
    u_ref[...] = jnp.dot(h, w_ref[:, 3 * d_att:], preferred_element_type=F32)


def _project(x, shift, scale, g, w_in_b, cos, sin, *, tiles_per_seq, pos_tiles):
    n, d = x.shape
    tm = min(TOK_TILE, n)
    d_in = w_in_b.shape[1]
    d_att = (d // 2)
    d_pool = d_in - 3 * d_att
    mod_rows = shift.shape[1]
    mod_spec = pl.BlockSpec((1, mod_rows, d), lambda i: (i // tiles_per_seq, 0, 0))
    tok = lambda c: pl.BlockSpec((tm, c), lambda i: (i, 0))
    return pl.pallas_call(
        functools.partial(_proj_kernel, d_att=d_att),
        out_shape=(jax.ShapeDtypeStruct((n, d_att), F32), jax.ShapeDtypeStruct((n, d_att), F32),
                   jax.ShapeDtypeStruct((n, d_att), F32), jax.ShapeDtypeStruct((n, d_pool), F32),
                   jax.ShapeDtypeStruct((n, d_att), BF16), jax.ShapeDtypeStruct((n, d_att), BF16),
                   jax.ShapeDtypeStruct((n // tm, SUBLANES, LANES), F32)),
        grid=(n // tm,),
        in_specs=[tok(d), mod_spec, mod_spec,
                  pl.BlockSpec((1, d), lambda i: (0, 0)),
                  pl.BlockSpec((d, d_in), lambda i: (0, 0)),
                  pl.BlockSpec((tm, LANES), lambda i: (i % pos_tiles, 0)),
                  pl.BlockSpec((tm, LANES), lambda i: (i % pos_tiles, 0))],
        out_specs=(tok(d_att), tok(d_att), tok(d_att), tok(d_pool), tok(d_att), tok(d_att),
                   pl.BlockSpec((1, SUBLANES, LANES), lambda i: (i, 0, 0))),
        compiler_params=_cparams(("arbitrary",)),
        name="in_proj",
    )(x, shift, scale, g, w_in_b, cos, sin)


def _lambda_value(lam_ref, lam_init):
    lp = lam_ref[...]
    a = jnp.sum(lp[0:1] * lp[1:2], axis=1, keepdims=True)
    b = jnp.sum(lp[2:3] * lp[3:4], axis=1, keepdims=True)
    return jnp.exp(a) - jnp.exp(b) + lam_init


def _stack_maps(q):
    lane = lax.broadcasted_iota(I32, q.shape, 1)
    q1 = jnp.where(lane < HEAD_DIM, q, 0.0)
    q2 = jnp.where(lane >= HEAD_DIM, q, 0.0)
    return jnp.concatenate([q1, q2], axis=0).astype(BF16)


_NT = (((1,), (1,)), ((), ()))


_TN = (((0,), (0,)), ((), ()))


def _attn_prompt_kernel(lam_ref, g_ref, kn_ref, q_ref, k_ref, k2_ref, v_ref, o_ref, m_ref, l_ref, acc_ref,
                        *, tq, tk, lam_init):
    qi = pl.program_id(2)
    heads = q_ref.shape[1] // HEAD_W
    qqs = [_stack_maps(q_ref[:, hh * HEAD_W:(hh + 1) * HEAD_W]) for hh in range(heads)]
    m_ref[...] = jnp.full(m_ref.shape, -jnp.inf, F32)
    l_ref[...] = jnp.zeros(l_ref.shape, F32)
    acc_ref[...] = jnp.zeros(acc_ref.shape, F32)

    lane = lax.broadcasted_iota(I32, (1, LANES), 1)
    kn = kn_ref[0, 0:1, :]
    worst = jnp.zeros((1, 1), F32)
    for hh in range(heads):
        qf = qqs[hh].astype(F32)
        qn2 = jnp.max(jnp.sum(qf * qf, axis=1, keepdims=True), axis=0, keepdims=True)
        kn2 = jnp.sum(jnp.where(lane == pl.program_id(1) * heads + hh, kn, 0.0), axis=1, keepdims=True)
        worst = jnp.maximum(worst, qn2 * kn2)
    bounded = worst[0, 0] <= SCORE_BOUND * SCORE_BOUND

    def run(fixed_reference):
        def block(start, size, masked):
            rows = pl.ds(start, size)
            for hh in range(heads):
                cols = slice(hh * HEAD_W, (hh + 1) * HEAD_W)

                def scores(kref):
                    st = lax.dot_general(kref[rows, cols], qqs[hh], _NT, preferred_element_type=F32)
                    if masked:
                        kpos = start + lax.broadcasted_iota(I32, st.shape, 0)
                        c = lax.broadcasted_iota(I32, st.shape, 1)
                        qpos = qi * tq + jnp.where(c >= tq, c - tq, c)
                        st = jnp.where(kpos <= qpos, st, -jnp.inf)
                    return st

                if fixed_reference:
                    p = jnp.exp2(scores(k_ref))
                    l_ref[hh] = l_ref[hh] + jnp.sum(p, axis=0, keepdims=True)
                    acc_ref[hh] = acc_ref[hh] + lax.dot_general(v_ref[rows, cols], p.astype(BF16), _TN,
                                                                preferred_element_type=F32)
                else:
                    m_prev = m_ref[hh]
                    m_new = jnp.maximum(m_prev, jnp.max(scores(k_ref), axis=0, keepdims=True))
                    p = jnp.exp2(scores(k2_ref) - m_new)
                    alpha = jnp.exp2(m_prev - m_new)
                    l_ref[hh] = alpha * l_ref[hh] + jnp.sum(p, axis=0, keepdims=True)
                    pv = lax.dot_general(v_ref[rows, cols], p.astype(BF16), _TN, preferred_element_type=F32)
                    acc_ref[hh] = acc_ref[hh] * alpha + pv
                    m_ref[hh] = m_new

        n_full = (qi * tq) // tk
        tail_start = pl.multiple_of(n_full * tk, tk)

        def body(j, carry):
            block(pl.multiple_of(j * tk, tk), tk, False)
            return carry

        lax.fori_loop(0, n_full, body, 0)
        if tk == tq:
            block(tail_start, tq, True)
        else:
            whole = qi * tq == n_full * tk

            @pl.when(whole)
            def _():
                block(tail_start, tq, True)

            @pl.when(jnp.logical_not(whole))
            def _():
                block(tail_start, tk, True)

    @pl.when(bounded)
    def _():
        run(True)

    @pl.when(jnp.logical_not(bounded))
    def _():
        run(False)

    lam = _lambda_value(lam_ref, lam_init)
    for hh in range(heads):
        l = l_ref[hh]
        acc = acc_ref[hh]
        o = acc[:, :tq] / l[:, :tq] - lam * (acc[:, tq:] / l[:, tq:])
        ms = jnp.mean(o * o, axis=0, keepdims=True)
        a = o * lax.rsqrt(ms + NORM_EPS) * g_ref[...] * (1.0 - lam_init)
        o_ref[:, hh * HEAD_W:(hh + 1) * HEAD_W] = a.T.astype(o_ref.dtype)


def _attn_prompt(q, kb, vb, lamp, g_col, k_norm2, *, batch, seq, lam_init):
    n, d_att = q.shape
    nh = d_att // HEAD_W
    hp = 4 if nh % 4 == 0 else 1
    tq = min(256, seq)
    tk = min(512, seq)
    nq = seq // tq
    return pl.pallas_call(
        functools.partial(_attn_prompt_kernel, tq=tq, tk=tk, lam_init=lam_init),
        out_shape=jax.ShapeDtypeStruct((n, d_att), BF16),
        grid=(batch, nh // hp, nq),
        in_specs=[pl.BlockSpec((8, LANES), lambda b, h, i: (0, 0)),
                  pl.BlockSpec((HEAD_W, 1), lambda b, h, i: (0, 0)),
                  pl.BlockSpec((1, SUBLANES, LANES), lambda b, h, i: (b, 0, 0)),
                  pl.BlockSpec((tq, hp * HEAD_W), lambda b, h, i: (b * nq + i, h)),
                  pl.BlockSpec((seq, hp * HEAD_W), lambda b, h, i: (b, h)),
                  pl.BlockSpec((seq, hp * HEAD_W), lambda b, h, i: (b, h)),
                  pl.BlockSpec((seq, hp * HEAD_W), lambda b, h, i: (b, h))],
        out_specs=pl.BlockSpec((tq, hp * HEAD_W), lambda b, h, i: (b * nq + i, h)),
        scratch_shapes=[pltpu.VMEM((hp, 1, 2 * tq), F32), pltpu.VMEM((hp, 1, 2 * tq), F32),
                        pltpu.VMEM((hp, HEAD_W, 2 * tq), F32)],
        compiler_params=_cparams(("arbitrary", "arbitrary", "arbitrary")),
        name="attn_prompt",
    )(lamp, g_col, k_norm2, q, kb, kb, vb)


def _attn_sample_kernel(pt_ref, lam_ref, g_ref, q_ref, kn_ref, vn_ref, *rest, n_heads, pages, t_new, lam_init):
    k_pages = rest[:pages]
    v_pages = rest[pages:2 * pages]
    o_ref = rest[2 * pages]
    m_ref, l_ref, acc_ref = rest[2 * pages + 1:]
    c = pl.program_id(1)
    rows_h = 2 * t_new
    assert n_heads == SUBLANES and n_heads * rows_h == LANES

    @pl.when(c == 0)
    def _():
        m_ref[...] = jnp.full(m_ref.shape, -jnp.inf, F32)
        l_ref[...] = jnp.zeros(l_ref.shape, F32)
        acc_ref[...] = jnp.zeros(acc_ref.shape, F32)

    q = q_ref[...]
    q_all = jnp.concatenate([_stack_maps(q[:, hh * HEAD_W:(hh + 1) * HEAD_W]) for hh in range(n_heads)], axis=0)
    sub = lax.broadcasted_iota(I32, (SUBLANES, LANES), 0)
    lane = lax.broadcasted_iota(I32, (SUBLANES, LANES), 1)
    own_head = (lane // rows_h) == sub

    def update(k_rows, v_rows, valid):
        n_pos = k_rows.shape[0] // n_heads
        r = lax.dot_general(k_rows.astype(BF16), q_all, _NT, preferred_element_type=F32)
        r = r.reshape(n_pos, n_heads, LANES)
        if valid is not None:
            r = jnp.where(valid, r, -jnp.inf)
        m_old = m_ref[...]
        m_new = jnp.maximum(m_old, jnp.max(r, axis=0))
        p = jnp.exp2(r - m_new[None])
        alpha = jnp.exp2(m_old - m_new)
        l_ref[...] = alpha * l_ref[...] + jnp.sum(p, axis=0)
        p_own = jnp.where(own_head[None], p, 0.0).reshape(n_pos * n_heads, LANES).astype(BF16)
        alpha_row = jnp.sum(jnp.where(own_head, alpha, 0.0), axis=0, keepdims=True)
        pv = lax.dot_general(v_rows.astype(BF16), p_own, _TN, preferred_element_type=F32)
        acc_ref[...] = acc_ref[...] * alpha_row + pv
        m_ref[...] = m_new

    for i in range(pages):
        update(k_pages[i][0], v_pages[i][0], None)

    @pl.when(c == pl.num_programs(1) - 1)
    def _():
        t_key = lax.broadcasted_iota(I32, (t_new, n_heads, LANES), 0)
        t_query = lax.broadcasted_iota(I32, (t_new, n_heads, LANES), 2) % t_new
        update(kn_ref[0], vn_ref[0], t_key <= t_query)
        l_row = jnp.sum(jnp.where(own_head, l_ref[...], 0.0), axis=0, keepdims=True)
        o_all = (acc_ref[...] / l_row).T
        lam = _lambda_value(lam_ref, lam_init)
        for hh in range(n_heads):
            rows = o_all[hh * rows_h:(hh + 1) * rows_h]
            o = rows[:t_new] - lam * rows[t_new:]
            ms = jnp.mean(o * o, axis=1, keepdims=True)
            o_ref[:, hh * HEAD_W:(hh + 1) * HEAD_W] = o * lax.rsqrt(ms + NORM_EPS) * g_ref[...] * (1.0 - lam_init)


def _attn_sample(page_table, q, k_new, v_new, cache_k, cache_v, lamp, g, *, t_new, lam_init):
    n, d_att = q.shape
    nh = d_att // HEAD_W
    db, n_pages = page_table.shape
    n_pool, page = cache_k.shape[0], cache_k.shape[1]
    pages = min(16, n_pages)
    ck = cache_k.reshape(n_pool, page * nh, HEAD_W)
    cv = cache_v.reshape(n_pool, page * nh, HEAD_W)

    def page_spec(i):
        return pl.BlockSpec((1, page * nh, HEAD_W), lambda b, c, pt: (pt[b, c * pages + i], 0, 0))

    tok = pl.BlockSpec((t_new, d_att), lambda b, c, pt: (b, 0))
    new_rows = pl.BlockSpec((1, t_new * nh, HEAD_W), lambda b, c, pt: (b, 0, 0))
    k_new = k_new.reshape(db, t_new, nh, HEAD_W).reshape(db, t_new * nh, HEAD_W)
    v_new = v_new.reshape(db, t_new, nh, HEAD_W).reshape(db, t_new * nh, HEAD_W)
    grid_spec = pltpu.PrefetchScalarGridSpec(
        num_scalar_prefetch=1,
        grid=(db, n_pages // pages),
        in_specs=[pl.BlockSpec((8, LANES), lambda b, c, pt: (0, 0)),
                  pl.BlockSpec((1, HEAD_W), lambda b, c, pt: (0, 0)),
                  tok, new_rows, new_rows]
                 + [page_spec(i) for i in range(pages)] + [page_spec(i) for i in range(pages)],
        out_specs=tok,
        scratch_shapes=[pltpu.VMEM((nh, LANES), F32), pltpu.VMEM((nh, LANES), F32),
                        pltpu.VMEM((HEAD_W, LANES), F32)],
    )
    return pl.pallas_call(
        functools.partial(_attn_sample_kernel, n_heads=nh, pages=pages, t_new=t_new, lam_init=lam_init),
        out_shape=jax.ShapeDtypeStruct((n, d_att), F32),
        grid_spec=grid_spec,
        compiler_params=_cparams(("arbitrary", "arbitrary")),
        name="attn_sample",
    )(page_table, lamp, g, q, k_new, v_new, *([ck] * pages), *([cv] * pages))


def _pool_kernel(halo_ref, cur_ref, w_ref, scale_ref, o_ref, ext_ref, *, pos0, tile_pos, zero_first_halo):
    i = pl.program_id(1)
    t = cur_ref.shape[1]
    halo = halo_ref[0]
    if zero_first_halo:
        halo = jnp.where(i == 0, 0.0, halo)
    cur = cur_ref[0]
    ext_ref[0:POOL_HALO, :] = halo
    ext_ref[POOL_HALO:POOL_HALO + t, :] = cur
    pos = pos0 + i * tile_pos + lax.broadcasted_iota(I32, (t, 1), 0)
    gw = cur.shape[1] // len(POOL_WINDOWS)
    for gi, w in enumerate(POOL_WINDOWS):
        cols = slice(gi * gw, (gi + 1) * gw)
        total = cur[:, cols]
        for j in range(1, w):
            total = total + ext_ref[POOL_HALO - j:POOL_HALO - j + t, cols]
        cnt = jnp.minimum(pos + 1, w).astype(F32)
        dlt = total / cnt - cur[:, cols]
        y = jnp.dot(dlt.astype(BF16), w_ref[gi], preferred_element_type=F32)
        o_ref[0, :, cols] = (y * scale_ref[:, cols]).astype(o_ref.dtype)


def _pool_mix(u3, w_pool_b, scale, *, tile, cur_block0, pos0, zero_first_halo):
    b, rows, c = u3.shape
    n_tiles = (rows - cur_block0 * tile) // tile
    ng, gw = w_pool_b.shape[0], w_pool_b.shape[1]
    assert (cur_block0 * tile) % POOL_HALO == 0 and (tile % POOL_HALO == 0 or n_tiles == 1)

    def halo_map(bi, i):
        return (bi, jnp.maximum(((cur_block0 + i) * tile) // POOL_HALO - 1, 0), 0)

    return pl.pallas_call(
        functools.partial(_pool_kernel, pos0=pos0, tile_pos=tile, zero_first_halo=zero_first_halo),
        out_shape=jax.ShapeDtypeStruct((b, n_tiles * tile, c), BF16),
        grid=(b, n_tiles),
        in_specs=[pl.BlockSpec((1, POOL_HALO, c), halo_map),
                  pl.BlockSpec((1, tile, c), lambda bi, i: (bi, cur_block0 + i, 0)),
                  pl.BlockSpec((ng, gw, gw), lambda bi, i: (0, 0, 0)),
                  pl.BlockSpec((1, c), lambda bi, i: (0, 0))],
        out_specs=pl.BlockSpec((1, tile, c), lambda bi, i: (bi, i, 0)),
        scratch_shapes=[pltpu.VMEM((POOL_HALO + tile, c), F32)],
        compiler_params=_cparams(("arbitrary", "arbitrary")),
        name="pool_mix",
    )(u3, u3, w_pool_b, scale)


def _split_bf16(x):
    hi = x.astype(BF16)
    lo = (x - hi.astype(F32)).astype(BF16)
    return hi, lo


def _round_up_f32(x, m):
    return jnp.floor((x + (m - 1.0)) * (1.0 / m)) * m


def _mix_out_kernel(*refs, n_experts, aliased):
    (x_ref, a_ref, pm_ref, gate_ref, shift_ref, scale_ref, g_ref, wo_ref, wr_ref, br_ref) = refs[:10]
    x2_ref, h2_ref, pk_ref, gt_ref, tc_ref = refs[10 + aliased:]
    d_att = a_ref.shape[1]
    mix = (jnp.dot(a_ref[...].astype(BF16), wo_ref[0:d_att, :], preferred_element_type=F32)
           + jnp.dot(pm_ref[...], wo_ref[d_att:, :], preferred_element_type=F32))
    x2 = x_ref[...] + gate_ref[0] * mix
    x2_ref[...] = x2
    h2 = _modulated_norm(x2, g_ref[...], shift_ref[0], scale_ref[0])
    h2_ref[...] = h2.astype(BF16)
    hh, hl = _split_bf16(h2)
    wh, wl = _split_bf16(wr_ref[...])
    logits = (jnp.dot(hh, wh, preferred_element_type=F32) + jnp.dot(hl, wh, preferred_element_type=F32)
              + jnp.dot(hh, wl, preferred_element_type=F32)) + br_ref[...]
    t = logits.shape[0]
    lane = lax.broadcasted_iota(I32, logits.shape, 1)
    lanef = lane.astype(F32)
    work = jnp.where(lane < n_experts, logits, -jnp.inf)
    vals, ids = [], []
    for _ in range(TOP_K):
        mx = jnp.max(work, axis=1, keepdims=True)
        ix = jnp.min(jnp.where(work == mx, lanef, float(LANES)), axis=1, keepdims=True)
        vals.append(mx)
        ids.append(ix)
        work = jnp.where(lanef == ix, -jnp.inf, work)
    es = [jnp.exp(v - vals[0]) for v in vals]
    den = es[0]
    for e in es[1:]:
        den = den + e
    sel = jnp.zeros(logits.shape, F32)
    for k in range(TOP_K):
        sel = jnp.where(lanef == ids[k], 1.0, sel)
    r = lax.broadcasted_iota(I32, (t, t), 0)
    c = lax.broadcasted_iota(I32, (t, t), 1)
    earlier = jnp.where(c < r, 1.0, 0.0).astype(BF16)
    local_rank = jnp.dot(earlier, sel.astype(BF16), preferred_element_type=F32)
    cnt = jnp.sum(sel, axis=0, keepdims=True)
    cnt8 = jnp.broadcast_to(_round_up_f32(cnt, float(SUBLANES)), (SUBLANES, LANES))
    er = lax.broadcasted_iota(I32, (LANES, LANES), 0)
    ec = lax.broadcasted_iota(I32, (LANES, LANES), 1)
    before = jnp.where(er < ec, 1.0, 0.0).astype(BF16)
    chunk_off = jnp.dot(cnt8.astype(BF16), before, preferred_element_type=F32)[0:1, :]
    pos = local_rank + chunk_off
    pk_out = jnp.zeros(logits.shape, F32)
    gt_out = jnp.zeros(logits.shape, F32)
    for k in range(TOP_K):
        pk = jnp.sum(jnp.where(lanef == ids[k], pos, 0.0), axis=1, keepdims=True)
        pk_out = jnp.where(lane == k, pk, pk_out)
        gt_out = jnp.where(lane == k, es[k] / den, gt_out)
    pk_ref[...] = pk_out
    gt_ref[...] = gt_out
    tc_ref[0] = jnp.broadcast_to(cnt, (SUBLANES, LANES))


def _mix_out(x, a, pm, gate, shift, scale, g, w_out_b, w_router_p, b_router_p, *, n_total, tile0,
             tiles_per_seq, n_experts, prev=None):
    n, d = x.shape
    tm = min(TOK_TILE, n)
    d_att = a.shape[1]
    mod_rows = gate.shape[1]
    mod_spec = pl.BlockSpec((1, mod_rows, d), lambda i: (i // tiles_per_seq, 0, 0))
    tok = lambda c: pl.BlockSpec((tm, c), lambda i: (i, 0))
    out_tok = lambda c: pl.BlockSpec((tm, c), lambda i: (tile0 + i, 0))
    out_shape = (jax.ShapeDtypeStruct((n_total, d), F32), jax.ShapeDtypeStruct((n_total, d), BF16),
                 jax.ShapeDtypeStruct((n_total, LANES), F32), jax.ShapeDtypeStruct((n_total, LANES), F32),
                 jax.ShapeDtypeStruct((n_total // tm, SUBLANES, LANES), F32))
    in_specs = [tok(d), tok(d_att), tok(pm.shape[1]), mod_spec, mod_spec, mod_spec,
                pl.BlockSpec((1, d), lambda i: (0, 0)),
                pl.BlockSpec(w_out_b.shape, lambda i: (0, 0)),
                pl.BlockSpec(w_router_p.shape, lambda i: (0, 0)),
                pl.BlockSpec((1, LANES), lambda i: (0, 0))]
    args = [x, a, pm, gate, shift, scale, g, w_out_b, w_router_p, b_router_p]
    aliases = {}
    n_alias = 0
    if prev is not None:
        n_alias = len(prev)
        in_specs += [pl.BlockSpec(memory_space=pl.ANY)] * n_alias
        aliases = {len(args) + k: k for k in range(n_alias)}
        args += list(prev)
    return pl.pallas_call(
        functools.partial(_mix_out_kernel, n_experts=n_experts, aliased=n_alias),
        out_shape=out_shape,
        grid=(n // tm,),
        in_specs=in_specs,
        out_specs=(out_tok(d), out_tok(d), out_tok(LANES), out_tok(LANES),
                   pl.BlockSpec((1, SUBLANES, LANES), lambda i: (tile0 + i, 0, 0))),
        input_output_aliases=aliases,
        compiler_params=_cparams(("arbitrary",)),
        name="mix_out_router",
    )(*args)


def _chunk_sizes(tile):
    sizes = []
    s = tile
    while s >= SUBLANES:
        sizes.append(s)
        s //= 2
    return sizes


def _for_each_chunk_piece(cnt_ref, base_ref, i, n_experts, tile, fn):
    def per_expert(e, off):
        l8 = (cnt_ref[i * n_experts + e] + (SUBLANES - 1)) & (-SUBLANES)
        dst0 = base_ref[i * n_experts + e]
        done = jnp.int32(0)
        for sz in _chunk_sizes(tile):
            take = l8 & sz

            @pl.when(take != 0)
            def _(done=done, sz=sz):
                fn(pl.multiple_of(off + done, SUBLANES), pl.multiple_of(dst0 + done, SUBLANES), sz)

            done = done + take
        return off + l8

    return lax.fori_loop(0, n_experts, per_expert, jnp.int32(0))


def _pack_halves(x):
    c = x.shape[1] // 2
    lo = lax.bitcast_convert_type(x[:, :c], U32) >> 16
    hi = lax.bitcast_convert_type(x[:, c:], U32) & jnp.uint32(0xFFFF0000)
    return lo | hi


def _unpack_halves(w):
    lo = lax.bitcast_convert_type(w << 16, F32).astype(BF16)
    hi = lax.bitcast_convert_type(w & jnp.uint32(0xFFFF0000), F32).astype(BF16)
    return lo, hi


def _one_hot_rows(pk, rows, weights=None):
    t = pk.shape[0]
    col = lax.broadcasted_iota(I32, (t, rows), 1).astype(F32)
    out = jnp.zeros((t, rows), F32)
    for k in range(TOP_K):
        w = 1.0 if weights is None else weights[:, k:k + 1]
        out = jnp.where(col == pk[:, k:k + 1], w, out)
    return out.astype(BF16)


def _dispatch_kernel(cnt_ref, base_ref, tail_ref, has_ref, h_ref, pk_ref, xs_hbm, sorted_ref, zero_ref, sem,
                     zero_sem, *, n_experts, tile):
    i = pl.program_id(0)
    last = pl.num_programs(0) - 1

    def zero_copy(e):
        return pltpu.make_async_copy(
            zero_ref, xs_hbm.at[pl.ds(pl.multiple_of(tail_ref[e], EXP_TILE), EXP_TILE)], zero_sem)

    @pl.when(i == 0)
    def _():
        zero_ref[...] = jnp.zeros(zero_ref.shape, U32)
        for e in range(n_experts):
            @pl.when(has_ref[e] == 1)
            def _(e=e):
                zero_copy(e).start()
        for e in range(n_experts):
            @pl.when(has_ref[e] == 1)
            def _(e=e):
                zero_copy(e).wait()

    def copy(step, src, dst, sz):
        slot = step & 1
        return pltpu.make_async_copy(sorted_ref.at[slot, pl.ds(src, sz)], xs_hbm.at[pl.ds(dst, sz)], sem.at[slot])

    onehot = _one_hot_rows(pk_ref[...], sorted_ref.shape[1])
    sorted_ref[i & 1] = _pack_halves(lax.dot_general(onehot, h_ref[...], _TN, preferred_element_type=F32))
    _for_each_chunk_piece(cnt_ref, base_ref, i, n_experts, tile, lambda s, d, sz: copy(i, s, d, sz).start())

    @pl.when(i > 0)
    def _():
        _for_each_chunk_piece(cnt_ref, base_ref, i - 1, n_experts, tile,
                              lambda s, d, sz: copy(i - 1, s, d, sz).wait())

    @pl.when(i == last)
    def _():
        _for_each_chunk_piece(cnt_ref, base_ref, i, n_experts, tile, lambda s, d, sz: copy(i, s, d, sz).wait())


def _sorted_rows(tile, n_experts):
    rows = tile * TOP_K + n_experts * (SUBLANES - 1)
    return -(-rows // LANES) * LANES


def _dispatch(h2, pk4, cnt_flat, base_flat, tail_rows, has_tile, *, r_max, n_experts):
    n, d = h2.shape
    tile = min(TOK_TILE, n)
    grid_spec = pltpu.PrefetchScalarGridSpec(
        num_scalar_prefetch=4,
        grid=(n // tile,),
        in_specs=[pl.BlockSpec((tile, d), lambda i, *_: (i, 0)),
                  pl.BlockSpec((tile, LANES), lambda i, *_: (i, 0))],
        out_specs=pl.BlockSpec(memory_space=pl.ANY),
        scratch_shapes=[pltpu.VMEM((2, _sorted_rows(tile, n_experts), d // 2), U32),
                        pltpu.VMEM((EXP_TILE, d // 2), U32), pltpu.SemaphoreType.DMA((2,)),
                        pltpu.SemaphoreType.DMA],
    )
    return pl.pallas_call(
        functools.partial(_dispatch_kernel, n_experts=n_experts, tile=tile),
        out_shape=jax.ShapeDtypeStruct((r_max, d // 2), U32),
        grid_spec=grid_spec,
        compiler_params=_cparams(("arbitrary",)),
        name="dispatch_rows",
    )(cnt_flat, base_flat, tail_rows, has_tile, h2, pk4)


def _expert_kernel(st_e, st_j, st_n, g_t0, g_gs, g_valid,
                   xs_hbm, wg_ref, wu_ref, wd_ref, bg_ref, bu_ref, bd_ref, ys_hbm,
                   xbuf, actbuf, wgb, wub, wdb, stage_in, stage_out, pending, sem_in, sem_out,
                   *, n_j, n_n, tm, tf):
    s = pl.program_id(0)
    steps = n_j + n_n
    q = s // steps
    ph = s - q * steps
    valid = g_valid[q] == 1
    gs = g_gs[q]
    row0 = g_t0[q] * tm

    @pl.when(s == 0)
    def _():
        pending[0] = 0
        pending[1] = 0

    def in_copy(first_row, i):
        slot = i & 1
        return pltpu.make_async_copy(xs_hbm.at[pl.ds(pl.multiple_of(first_row + i * tm, tm), tm)],
                                     stage_in.at[slot], sem_in.at[slot])

    def stage_to_xbuf(i):
        rows = pl.ds(pl.multiple_of(i * tm, tm), tm)
        lo, hi = _unpack_halves(stage_in[i & 1])
        half = lo.shape[1]
        xbuf[rows, 0:half] = lo
        xbuf[rows, half:2 * half] = hi

    @pl.when(s == 0)
    def _():
        in_copy(row0, 0).start()

        def load(i, carry):
            @pl.when(i + 1 < gs)
            def _():
                in_copy(row0, i + 1).start()

            in_copy(row0, i).wait()
            stage_to_xbuf(i)
            return carry

        lax.fori_loop(0, gs, load, 0)

    q_next = jnp.minimum(q + 1, g_gs.shape[0] - 1)
    gs_next = jnp.where(q + 1 < g_gs.shape[0], g_gs[q_next], 0)
    row0_next = g_t0[q_next] * tm
    fetch_tiles = [2 * (ph - n_j), 2 * (ph - n_j) + 1]

    n_pairs = gs // 2
    has_tail = (gs & 1) == 1
    tail_row = pl.multiple_of(n_pairs * 2 * tm, tm)

    @pl.when(jnp.logical_and(valid, ph < n_j))
    def _():
        wgb[...] = wg_ref[0].astype(BF16)
        wub[...] = wu_ref[0].astype(BF16)

        def tile(r0, size):
            rows = pl.ds(r0, size)
            x = xbuf[rows, :]
            gate = jnp.dot(x, wgb[...], preferred_element_type=F32) + bg_ref[0]
            up = jnp.dot(x, wub[...], preferred_element_type=F32) + bu_ref[0]
            gate = jnp.minimum(gate, SWIGLU_LIMIT)
            up = jnp.clip(up, -SWIGLU_LIMIT, SWIGLU_LIMIT)
            act = (up + 1.0) * (gate * jax.nn.sigmoid(SWIGLU_ALPHA * gate))
            actbuf[ph, rows, :] = act.astype(BF16)

        def pair(i, carry):
            tile(pl.multiple_of(i * 2 * tm, 2 * tm), 2 * tm)
            return carry

        lax.fori_loop(0, n_pairs, pair, 0)

        @pl.when(has_tail)
        def _():
            tile(tail_row, tm)

    @pl.when(jnp.logical_and(valid, ph >= n_j))
    def _():
        for t in fetch_tiles:
            @pl.when(t < gs_next)
            def _(t=t):
                in_copy(row0_next, t).start()

        wdb[...] = wd_ref[0].astype(BF16)
        col0 = pl.multiple_of((ph - n_j) * (tf // 2), tf // 2)

        def out_copy(slot, r0, size):
            return pltpu.make_async_copy(
                stage_out.at[slot, pl.ds(0, size)],
                ys_hbm.at[pl.ds(pl.multiple_of(row0 + r0, tm), size), pl.ds(col0, tf // 2)],
                sem_out.at[slot])

        def wait_slot(slot):
            for n_tiles in (1, 2):
                @pl.when(pending[slot] == n_tiles)
                def _(n_tiles=n_tiles):
                    out_copy(slot, 0, n_tiles * tm).wait()
                    pending[slot] = 0

        def tile(r0, size, slot):
            rows = pl.ds(r0, size)
            wait_slot(slot)
            y = bd_ref[0] + jnp.dot(actbuf[0, rows, :], wdb[0:tf, :], preferred_element_type=F32)
            for j in range(1, n_j):
                y = y + jnp.dot(actbuf[j, rows, :], wdb[j * tf:(j + 1) * tf, :], preferred_element_type=F32)
            stage_out[slot, 0:size] = _pack_halves(y.astype(BF16).astype(F32))
            out_copy(slot, r0, size).start()
            pending[slot] = size // tm

        def pair(i, carry):
            tile(pl.multiple_of(i * 2 * tm, 2 * tm), 2 * tm, i & 1)
            return carry

        lax.fori_loop(0, n_pairs, pair, 0)

        @pl.when(has_tail)
        def _():
            tile(tail_row, tm, n_pairs & 1)

        for t in fetch_tiles:
            @pl.when(t < gs_next)
            def _(t=t):
                in_copy(row0_next, t).wait()
                stage_to_xbuf(t)

        @pl.when(s == pl.num_programs(0) - 1)
        def _():
            wait_slot(0)
            wait_slot(1)


def _expert_ffn(tables, xs, w_gate, b_gate, w_up, b_up, w_down, b_down, *, n_steps):
    st_e, st_j, st_n, g_t0, g_gs, g_valid = tables
    r_max = xs.shape[0]
    n_exp, d, d_ff = w_gate.shape
    tf = min(FF_TILE, d_ff, d)
    n_j = d_ff // tf
    n_n = d // tf
    tm = EXP_TILE
    rows_g = EXP_GROUP * tm
    assert 2 * n_n >= EXP_GROUP, "phase-2 steps fetch two row tiles of the next group each"
    grid_spec = pltpu.PrefetchScalarGridSpec(
        num_scalar_prefetch=6,
        grid=(n_steps,),
        in_specs=[pl.BlockSpec(memory_space=pl.ANY),
                  pl.BlockSpec((1, d, tf), lambda s, e, j, n, *_: (e[s], 0, j[s])),
                  pl.BlockSpec((1, d, tf), lambda s, e, j, n, *_: (e[s], 0, j[s])),
                  pl.BlockSpec((1, d_ff, tf), lambda s, e, j, n, *_: (e[s], 0, n[s])),
                  pl.BlockSpec((1, 1, tf), lambda s, e, j, n, *_: (e[s], 0, j[s])),
                  pl.BlockSpec((1, 1, tf), lambda s, e, j, n, *_: (e[s], 0, j[s])),
                  pl.BlockSpec((1, 1, tf), lambda s, e, j, n, *_: (e[s], 0, n[s]))],
        out_specs=pl.BlockSpec(memory_space=pl.ANY),
        scratch_shapes=[pltpu.VMEM((rows_g, d), BF16),
                        pltpu.VMEM((n_j, rows_g, tf), BF16),
                        pltpu.VMEM((d, tf), BF16), pltpu.VMEM((d, tf), BF16), pltpu.VMEM((d_ff, tf), BF16),
                        pltpu.VMEM((2, tm, d // 2), U32), pltpu.VMEM((2, 2 * tm, tf // 2), U32),
                        pltpu.SMEM((2,), I32),
                        pltpu.SemaphoreType.DMA((2,)), pltpu.SemaphoreType.DMA((2,))],
    )
    return pl.pallas_call(
        functools.partial(_expert_kernel, n_j=n_j, n_n=n_n, tm=tm, tf=tf),
        out_shape=jax.ShapeDtypeStruct((r_max, d // 2), U32),
        grid_spec=grid_spec,
        compiler_params=_cparams(("arbitrary",)),
        name="expert_ffn",
    )(st_e, st_j, st_n, g_t0, g_gs, g_valid, xs, w_gate, w_up, w_down,
      b_gate.reshape(n_exp, 1, d_ff), b_up.reshape(n_exp, 1, d_ff), b_down.reshape(n_exp, 1, d))


def _combine_kernel(cnt_ref, base_ref, ys_hbm, x2_ref, pk_ref, gt_ref, gate_p_ref, gate_s_ref, g_ref,
                    yp_ref, ysm_ref, rows_ref, sem, *, n_experts, tile, n_prompt_tiles, out_tile):
    i = pl.program_id(0)
    last = pl.num_programs(0) - 1

    def copy(step, dst, src, sz):
        slot = step & 1
        return pltpu.make_async_copy(ys_hbm.at[pl.ds(src, sz)], rows_ref.at[slot, pl.ds(dst, sz)], sem.at[slot])

    def fetch(step):
        _for_each_chunk_piece(cnt_ref, base_ref, step, n_experts, tile,
                              lambda s, d, sz: copy(step, s, d, sz).start())

    @pl.when(i == 0)
    def _():
        rows_ref[...] = jnp.zeros(rows_ref.shape, U32)
        fetch(i)

    @pl.when(i < last)
    def _():
        fetch(i + 1)

    _for_each_chunk_piece(cnt_ref, base_ref, i, n_experts, tile, lambda s, d, sz: copy(i, s, d, sz).wait())
    weights = _one_hot_rows(pk_ref[...], rows_ref.shape[1], gt_ref[...])
    lo, hi = _unpack_halves(rows_ref[i & 1])
    y_lo = jnp.dot(weights, lo, preferred_element_type=F32)
    y_hi = jnp.dot(weights, hi, preferred_element_type=F32)
    half = out_tile // 2
    pieces = []
    for n in range(x2_ref.shape[1] // out_tile):
        pieces += [y_lo[:, n * half:(n + 1) * half], y_hi[:, n * half:(n + 1) * half]]
    y = jnp.concatenate(pieces, axis=1)
    is_prompt = i < n_prompt_tiles
    gate = jnp.where(is_prompt, gate_p_ref[0], gate_s_ref[...])
    x3 = x2_ref[...] + gate * y
    ms = jnp.mean(x3 * x3, axis=-1, keepdims=True)
    out = x3 * lax.rsqrt(ms + NORM_EPS) * g_ref[...]

    @pl.when(is_prompt)
    def _():
        yp_ref[...] = out

    @pl.when(jnp.logical_not(is_prompt))
    def _():
        ysm_ref[...] = out


def _combine(ys, cnt_flat, base_flat, x2, pk4, gt, gate_p, gate_s, final_g, *, n_prompt, tiles_per_seq, n_experts,
             out_tile):
    n, d = x2.shape
    tile = min(TOK_TILE, n_prompt)
    n_s = n - n_prompt
    assert n_s == tile and n_prompt % tile == 0
    npt = n_prompt // tile
    grid_spec = pltpu.PrefetchScalarGridSpec(
        num_scalar_prefetch=2,
        grid=(n // tile,),
        in_specs=[pl.BlockSpec(memory_space=pl.ANY),
                  pl.BlockSpec((tile, d), lambda i, *_: (i, 0)),
                  pl.BlockSpec((tile, LANES), lambda i, *_: (i, 0)),
                  pl.BlockSpec((tile, LANES), lambda i, *_: (i, 0)),
                  pl.BlockSpec((1, 1, d), lambda i, *_: (jnp.minimum(i, npt - 1) // tiles_per_seq, 0, 0)),
                  pl.BlockSpec((tile, d), lambda i, *_: (0, 0)),
                  pl.BlockSpec((1, d), lambda i, *_: (0, 0))],
        out_specs=(pl.BlockSpec((tile, d), lambda i, *_: (jnp.minimum(i, npt - 1), 0)),
                   pl.BlockSpec((tile, d), lambda i, *_: (0, 0))),
        scratch_shapes=[pltpu.VMEM((2, _sorted_rows(tile, n_experts), d // 2), U32),
                        pltpu.SemaphoreType.DMA((2,))],
    )
    return pl.pallas_call(
        functools.partial(_combine_kernel, n_experts=n_experts, tile=tile, n_prompt_tiles=npt, out_tile=out_tile),
        out_shape=(jax.ShapeDtypeStruct((n_prompt, d), F32), jax.ShapeDtypeStruct((n_s, d), F32)),
        grid_spec=grid_spec,
        compiler_params=_cparams(("arbitrary",)),
        name="combine_norm",
    )(cnt_flat, base_flat, ys, x2, pk4, gt, gate_p, gate_s, final_g)


def _expert_tables(tile_cnt, *, tile, n_j, n_n):
    n_tiles, n_experts = tile_cnt.shape
    tm, grp = EXP_TILE, EXP_GROUP
    cnt8 = (tile_cnt + (SUBLANES - 1)) // SUBLANES * SUBLANES
    rows_e = jnp.sum(cnt8, axis=0)
    max_rows = n_tiles * tile * TOP_K + n_experts * n_tiles * (SUBLANES - 1)
    t_max = -(-max_rows // tm) + n_experts
    ng_max = n_experts + t_max // grp
    ntile = (rows_e + tm - 1) // tm
    tile_start = jnp.cumsum(ntile) - ntile
    pstart = tile_start * tm
    base = pstart[None, :] + jnp.cumsum(cnt8, axis=0) - cnt8
    ng = (ntile + grp - 1) // grp
    cg = jnp.cumsum(ng)
    n_groups = cg[-1]
    q = jnp.arange(ng_max, dtype=I32)
    eq = jnp.minimum(jnp.sum((q[:, None] >= cg[None, :]).astype(I32), axis=1), n_experts - 1)
    onehot = eq[:, None] == jnp.arange(n_experts, dtype=I32)[None, :]

    def pick(v):
        return jnp.sum(jnp.where(onehot, v[None, :], 0), axis=1)

    lg = q - (pick(cg) - pick(ng))
    t0 = pick(tile_start) + lg * grp
    gs = jnp.clip(pick(ntile) - lg * grp, 0, grp)
    valid = q < n_groups
    is_last = q == jnp.maximum(n_groups - 1, 0)
    eq = jnp.where(valid, eq, jnp.sum(jnp.where(is_last, eq, 0)))
    t0 = jnp.where(valid, t0, jnp.sum(jnp.where(is_last, t0, 0)))
    gs = jnp.where(valid, gs, 0)
    steps = n_j + n_n
    ph = jnp.tile(jnp.arange(steps, dtype=I32), ng_max)
    vs = jnp.repeat(valid, steps)
    st_e = jnp.repeat(eq, steps)
    st_j = jnp.where(vs, jnp.minimum(ph, n_j - 1), n_j - 1)
    st_n = jnp.where(vs, jnp.maximum(ph - n_j, 0), n_n - 1)
    tail = jnp.maximum(pstart + (ntile - 1) * tm, 0)
    has = (ntile > 0).astype(I32)
    tabs = tuple(a.astype(I32) for a in (st_e, st_j, st_n, t0, gs, valid))
    return (tabs, tile_cnt.reshape(-1).astype(I32), base.reshape(-1).astype(I32), tail.astype(I32), has,
            t_max * tm, (n_groups * steps).astype(I32))


def _rope_tables(pos):
    inv = 1.0 / (ROPE_THETA ** (jnp.arange(0, HEAD_DIM, 2, dtype=F32) / HEAD_DIM))
    ang = pos.astype(F32)[:, None] * inv[None, :]
    reps = LANES // (HEAD_DIM // 2)
    return jnp.tile(jnp.cos(ang), (1, reps)), jnp.tile(jnp.sin(ang), (1, reps))


def kernel(x_prompt, x_sample, cache_k, cache_v, state_pool, page_table, c_prompt, c_sample, w_ada, b_ada, norm1_g, norm2_g, w_in, lam_q1, lam_k1, lam_q2, lam_k2, subln_g, w_pool, pool_scale, w_out, w_router, b_router, w_gate, b_gate, w_up, b_up, w_down, b_down, final_g):
    B, S, D = x_prompt.shape
    DB, T, _ = x_sample.shape
    depth = w_ada.shape[0]
    page = cache_k.shape[2]
    past = page_table.shape[1] * page
    d_att = D // 2
    n_heads = d_att // HEAD_W
    d_pool = w_in.shape[2] - 3 * d_att
    n_experts = w_router.shape[2]
    state_len = state_pool.shape[2]
    n_p, n_s = B * S, DB * T
    n_all = n_p + n_s
    tm = min(TOK_TILE, n_p)
    tiles_per_seq = S // tm
    assert depth == 1, "single-layer step"
    assert n_s == tm and S % tm == 0 and T % 8 == 0 and state_len < POOL_HALO <= tm

    cos_p, sin_p = _rope_tables(jnp.arange(S))
    cos_s, sin_s = _rope_tables(jnp.tile(past + jnp.arange(T), DB))

    l = 0
    lam_init = 0.8 - 0.6 * math.exp(-0.3 * l)
    lamp = jnp.zeros((8, LANES), F32)
    for r, vec in enumerate((lam_q1[l], lam_k1[l], lam_q2[l], lam_k2[l])):
        lamp = lamp.at[r, :HEAD_DIM].set(vec.astype(F32))
    subg = subln_g[l].reshape(1, HEAD_W)

    rows_c = -(-(B + DB) // 8) * 8
    c_all = jnp.zeros((rows_c, D), F32).at[:B].set(c_prompt).at[B:B + DB].set(c_sample)
    m_all = _adaln(c_all, w_ada[l], b_ada[l])
    mods_p = [m_all[:B, k * D:(k + 1) * D].reshape(B, 1, D) for k in range(N_ADA)]
    mods_s = [jnp.repeat(m_all[B:B + DB, k * D:(k + 1) * D], T, axis=0).reshape(1, n_s, D) for k in range(N_ADA)]

    w_in_b = _cast_bf16(w_in[l], 256)
    w_out_b = _cast_bf16(w_out[l], 256)
    ng, gw = w_pool.shape[1], w_pool.shape[2]
    w_pool_b = _cast_bf16(w_pool[l].reshape(ng * gw, gw), ng * gw).reshape(ng, gw, gw)
    g1 = norm1_g[l].reshape(1, D)
    g2 = norm2_g[l].reshape(1, D)
    pscale = pool_scale[l].reshape(1, d_pool)
    w_router_p = jnp.zeros((D, LANES), F32).at[:, :n_experts].set(w_router[l])
    b_router_p = jnp.zeros((1, LANES), F32).at[0, :n_experts].set(b_router[l].astype(F32))

    xp = x_prompt.reshape(n_p, D)
    q_p, k_p, v_p, u_p, kb_p, vb_p, kn_p = _project(xp, mods_p[0], mods_p[1], g1, w_in_b, cos_p, sin_p,
                                                    tiles_per_seq=tiles_per_seq, pos_tiles=tiles_per_seq)
    kn_seq = jnp.max(kn_p.reshape(B, tiles_per_seq, SUBLANES, LANES), axis=1)
    a_p = _attn_prompt(q_p, kb_p, vb_p, lamp, subg.reshape(HEAD_W, 1), kn_seq, batch=B, seq=S, lam_init=lam_init)
    pm_p = _pool_mix(u_p.reshape(B, S, d_pool), w_pool_b, pscale, tile=tm, cur_block0=0, pos0=0,
                     zero_first_halo=True).reshape(n_p, d_pool)

    xs_tok = x_sample.reshape(n_s, D)
    q_s, k_s, v_s, u_s, _, _, _ = _project(xs_tok, mods_s[0], mods_s[1], g1, w_in_b, cos_s, sin_s,
                                           tiles_per_seq=1, pos_tiles=1)
    a_s = _attn_sample(page_table, q_s, k_s, v_s, cache_k[l], cache_v[l], lamp, subg, t_new=T, lam_init=lam_init)
    u_ext = jnp.concatenate([jnp.zeros((DB, POOL_HALO - state_len, d_pool), F32),
                             state_pool[l].astype(F32), u_s.reshape(DB, T, d_pool)], axis=1)
    pm_s = _pool_mix(u_ext, w_pool_b, pscale, tile=T, cur_block0=POOL_HALO // T, pos0=past,
                     zero_first_halo=False).reshape(n_s, d_pool)

    outs = _mix_out(xp, a_p, pm_p, mods_p[2], mods_p[3], mods_p[4], g2, w_out_b, w_router_p, b_router_p,
                    n_total=n_all, tile0=0, tiles_per_seq=tiles_per_seq, n_experts=n_experts)
    x2, h2, pk4, gt4, tcnt = _mix_out(xs_tok, a_s, pm_s, mods_s[2], mods_s[3], mods_s[4], g2, w_out_b, w_router_p,
                                      b_router_p, n_total=n_all, tile0=n_p // tm, tiles_per_seq=1,
                                      n_experts=n_experts, prev=outs)

    d_ff = w_gate.shape[3]
    tf = min(FF_TILE, d_ff, D)
    tile_cnt = tcnt[:, 0, :n_experts].astype(I32)
    tabs, cnt_flat, base_flat, tail, has, r_max, n_steps = _expert_tables(
        tile_cnt, tile=tm, n_j=d_ff // tf, n_n=D // tf)
    xs_rows = _dispatch(h2, pk4, cnt_flat, base_flat, tail, has, r_max=r_max, n_experts=n_experts)
    ys_rows = _expert_ffn(tabs, xs_rows, w_gate[l], b_gate[l], w_up[l], b_up[l], w_down[l], b_down[l],
                          n_steps=n_steps)
    y_p, y_s = _combine(ys_rows, cnt_flat, base_flat, x2, pk4, gt4, mods_p[5], mods_s[5].reshape(n_s, D),
                        final_g.reshape(1, D), n_prompt=n_p, tiles_per_seq=tiles_per_seq, n_experts=n_experts,
                        out_tile=tf)

    n_pages_p = S // page
    k_prompt = k_p.reshape(1, B, n_pages_p, page, n_heads, HEAD_W)
    v_prompt = v_p.reshape(1, B, n_pages_p, page, n_heads, HEAD_W)
    pool_prompt = u_p.reshape(B, S, d_pool)[:, S - state_len:][None]
    k_sample = k_s.reshape(1, DB, T, n_heads, HEAD_W)
    v_sample = v_s.reshape(1, DB, T, n_heads, HEAD_W)
    pool_sample = u_ext[:, -state_len:][None]
    return (y_p.reshape(B, S, D), y_s.reshape(DB, T, D), k_prompt, v_prompt, pool_prompt,
            k_sample, v_sample, pool_sample)
```

```python
import functools
import math

import jax
import jax.numpy as jnp
from jax import lax
from jax.experimental import pallas as pl
from jax.experimental.pallas import tpu as pltpu

F32 = jnp.float32
BF16 = jnp.bfloat16
I32 = jnp.int32
U32 = jnp.uint32

HEAD_DIM = 64
HEAD_W = 2 * HEAD_DIM
POOL_WINDOWS = (2, 4, 8, 16)
POOL_HALO = 16
TOP_K = 4
SWIGLU_LIMIT = 7.0
SWIGLU_ALPHA = 1.702
ROPE_THETA = 10000.0
NORM_EPS = 1e-5
N_ADA = 6
LANES = 128
SUBLANES = 8
V7X_VMEM_LIMIT = 58 * 1024 * 1024

TOK_TILE = 256
EXP_TILE = 256
EXP_GROUP = 6
FF_TILE = 512
Q_SCALE = (HEAD_DIM ** -0.5) * math.log2(math.e)
SCORE_BOUND = 60.0


def _cparams(sem, vmem=V7X_VMEM_LIMIT):
    return pltpu.CompilerParams(dimension_semantics=sem, vmem_limit_bytes=vmem)


def _cast_kernel(x_ref, o_ref):
    o_ref[...] = x_ref[...].astype(o_ref.dtype)


def _cast_bf16(w, rows):
    r, c = w.shape
    return pl.pallas_call(
        _cast_kernel,
        out_shape=jax.ShapeDtypeStruct((r, c), BF16),
        grid=(r // rows,),
        in_specs=[pl.BlockSpec((rows, c), lambda i: (i, 0))],
        out_specs=pl.BlockSpec((rows, c), lambda i: (i, 0)),
        compiler_params=_cparams(("arbitrary",)),
        name="cast_bf16",
    )(w)


def _ada_kernel(c_ref, w_ref, b_ref, o_ref):
    c = c_ref[...]
    s = (c * jax.nn.sigmoid(c)).astype(BF16)
    o_ref[...] = jnp.dot(s, w_ref[...].astype(BF16), preferred_element_type=F32) + b_ref[...]


def _adaln(c_all, w_ada, b_ada):
    rows, d = c_all.shape
    n = w_ada.shape[1]
    tn = min(1024, n)
    return pl.pallas_call(
        _ada_kernel,
        out_shape=jax.ShapeDtypeStruct((rows, n), F32),
        grid=(n // tn,),
        in_specs=[pl.BlockSpec((rows, d), lambda j: (0, 0)),
                  pl.BlockSpec((d, tn), lambda j: (0, j)),
                  pl.BlockSpec((1, tn), lambda j: (0, j))],
        out_specs=pl.BlockSpec((rows, tn), lambda j: (0, j)),
        compiler_params=_cparams(("arbitrary",)),
        name="adaln",
    )(c_all, w_ada, b_ada.reshape(1, n))


def _modulated_norm(x, g, shift, scale):
    ms = jnp.mean(x * x, axis=-1, keepdims=True)
    return (x * lax.rsqrt(ms + NORM_EPS) * g) * (1.0 + scale) + shift


def _proj_kernel(x_ref, shift_ref, scale_ref, g_ref, w_ref, cos_ref, sin_ref,
                 q_ref, k_ref, v_ref, u_ref, kb_ref, vb_ref, kn_ref, *, d_att):
    h = _modulated_norm(x_ref[...], g_ref[...], shift_ref[0], scale_ref[0]).astype(BF16)
    cos = cos_ref[...]
    sin = sin_ref[...]
    lane = lax.broadcasted_iota(I32, cos.shape, 1)
    first_half = (lane & (HEAD_DIM - 1)) < (HEAD_DIM // 2)
    head_lane = lax.broadcasted_iota(I32, (1, LANES), 1)

    def rope(z):
        rot = jnp.where(first_half, -pltpu.roll(z, LANES - HEAD_DIM // 2, 1), pltpu.roll(z, HEAD_DIM // 2, 1))
        return z * cos + rot * sin

    zq = jnp.dot(h, w_ref[:, 0:d_att], preferred_element_type=F32)
    zk = jnp.dot(h, w_ref[:, d_att:2 * d_att], preferred_element_type=F32)
    k_norm2 = jnp.zeros((1, LANES), F32)
    for hh in range(d_att // HEAD_W):
        sl = slice(hh * HEAD_W, (hh + 1) * HEAD_W)
        q_ref[:, sl] = rope(zq[:, sl]) * Q_SCALE
        kr = rope(zk[:, sl])
        k_ref[:, sl] = kr
        kb_ref[:, sl] = kr.astype(BF16)
        n2 = jnp.max(jnp.sum(kr * kr, axis=1, keepdims=True), axis=0, keepdims=True)
        k_norm2 = jnp.where(head_lane == hh, n2, k_norm2)
    kn_ref[0] = jnp.broadcast_to(k_norm2, kn_ref.shape[1:])
    zv = jnp.dot(h, w_ref[:, 2 * d_att:3 * d_att], preferred_element_type=F32)
    v_ref[...] = zv
    vb_ref[...] = zv.astype(BF16)
    u_ref[...] = jnp.dot(h, w_ref[:, 3 * d_att:], preferred_element_type=F32)


def _project(x, shift, scale, g, w_in_b, cos, sin, *, tiles_per_seq, pos_tiles):
    n, d = x.shape
    tm = min(TOK_TILE, n)
    d_in = w_in_b.shape[1]
    d_att = (d // 2)
    d_pool = d_in - 3 * d_att
    mod_rows = shift.shape[1]
    mod_spec = pl.BlockSpec((1, mod_rows, d), lambda i: (i // tiles_per_seq, 0, 0))
    tok = lambda c: pl.BlockSpec((tm, c), lambda i: (i, 0))
    return pl.pallas_call(
        functools.partial(_proj_kernel, d_att=d_att),
        out_shape=(jax.ShapeDtypeStruct((n, d_att), F32), jax.ShapeDtypeStruct((n, d_att), F32),
                   jax.ShapeDtypeStruct((n, d_att), F32), jax.ShapeDtypeStruct((n, d_pool), F32),
                   jax.ShapeDtypeStruct((n, d_att), BF16), jax.ShapeDtypeStruct((n, d_att), BF16),
                   jax.ShapeDtypeStruct((n // tm, SUBLANES, LANES), F32)),
        grid=(n // tm,),
        in_specs=[tok(d), mod_spec, mod_spec,
                  pl.BlockSpec((1, d), lambda i: (0, 0)),
                  pl.BlockSpec((d, d_in), lambda i: (0, 0)),
                  pl.BlockSpec((tm, LANES), lambda i: (i % pos_tiles, 0)),
                  pl.BlockSpec((tm, LANES), lambda i: (i % pos_tiles, 0))],
        out_specs=(tok(d_att), tok(d_att), tok(d_att), tok(d_pool), tok(d_att), tok(d_att),
                   pl.BlockSpec((1, SUBLANES, LANES), lambda i: (i, 0, 0))),
        compiler_params=_cparams(("arbitrary",)),
        name="in_proj",
    )(x, shift, scale, g, w_in_b, cos, sin)


def _lambda_value(lam_ref, lam_init):
    lp = lam_ref[...]
    a = jnp.sum(lp[0:1] * lp[1:2], axis=1, keepdims=True)
    b = jnp.sum(lp[2:3] * lp[3:4], axis=1, keepdims=True)
    return jnp.exp(a) - jnp.exp(b) + lam_init


def _stack_maps(q):
    lane = lax.broadcasted_iota(I32, q.shape, 1)
    q1 = jnp.where(lane < HEAD_DIM, q, 0.0)
    q2 = jnp.where(lane >= HEAD_DIM, q, 0.0)
    return jnp.concatenate([q1, q2], axis=0).astype(BF16)


_NT = (((1,), (1,)), ((), ()))


_TN = (((0,), (0,)), ((), ()))


def _attn_prompt_kernel(lam_ref, g_ref, kn_ref, q_ref, k_ref, k2_ref, v_ref, o_ref, m_ref, l_ref, acc_ref,
                        *, tq, tk, lam_init):
    qi = pl.program_id(2)
    heads = q_ref.shape[1] // HEAD_W
    qqs = [_stack_maps(q_ref[:, hh * HEAD_W:(hh + 1) * HEAD_W]) for hh in range(heads)]
    m_ref[...] = jnp.full(m_ref.shape, -jnp.inf, F32)
    l_ref[...] = jnp.zeros(l_ref.shape, F32)
    acc_ref[...] = jnp.zeros(acc_ref.shape, F32)

    lane = lax.broadcasted_iota(I32, (1, LANES), 1)
    kn = kn_ref[0, 0:1, :]
    worst = jnp.zeros((1, 1), F32)
    for hh in range(heads):
        qf = qqs[hh].astype(F32)
        qn2 = jnp.max(jnp.sum(qf * qf, axis=1, keepdims=True), axis=0, keepdims=True)
        kn2 = jnp.sum(jnp.where(lane == pl.program_id(1) * heads + hh, kn, 0.0), axis=1, keepdims=True)
        worst = jnp.maximum(worst, qn2 * kn2)
    bounded = worst[0, 0] <= SCORE_BOUND * SCORE_BOUND

    def run(fixed_reference):
        def block(start, size, masked):
            rows = pl.ds(start, size)
            for hh in range(heads):
                cols = slice(hh * HEAD_W, (hh + 1) * HEAD_W)

                def scores(kref):
                    st = lax.dot_general(kref[rows, cols], qqs[hh], _NT, preferred_element_type=F32)
                    if masked:
                        kpos = start + lax.broadcasted_iota(I32, st.shape, 0)
                        c = lax.broadcasted_iota(I32, st.shape, 1)
                        qpos = qi * tq + jnp.where(c >= tq, c - tq, c)
                        st = jnp.where(kpos <= qpos, st, -jnp.inf)
                    return st

                if fixed_reference:
                    p = jnp.exp2(scores(k_ref))
                    l_ref[hh] = l_ref[hh] + jnp.sum(p, axis=0, keepdims=True)
                    acc_ref[hh] = acc_ref[hh] + lax.dot_general(v_ref[rows, cols], p.astype(BF16), _TN,
                                                                preferred_element_type=F32)
                else:
                    m_prev = m_ref[hh]
                    m_new = jnp.maximum(m_prev, jnp.max(scores(k_ref), axis=0, keepdims=True))
                    p = jnp.exp2(scores(k2_ref) - m_new)
                    alpha = jnp.exp2(m_prev - m_new)
                    l_ref[hh] = alpha * l_ref[hh] + jnp.sum(p, axis=0, keepdims=True)
                    pv = lax.dot_general(v_ref[rows, cols], p.astype(BF16), _TN, preferred_element_type=F32)
                    acc_ref[hh] = acc_ref[hh] * alpha + pv
                    m_ref[hh] = m_new

        n_full = (qi * tq) // tk
        tail_start = pl.multiple_of(n_full * tk, tk)

        def body(j, carry):
            block(pl.multiple_of(j * tk, tk), tk, False)
            return carry

        lax.fori_loop(0, n_full, body, 0)
        if tk == tq:
            block(tail_start, tq, True)
        else:
            whole = qi * tq == n_full * tk

            @pl.when(whole)
            def _():
                block(tail_start, tq, True)

            @pl.when(jnp.logical_not(whole))
            def _():
                block(tail_start, tk, True)

    @pl.when(bounded)
    def _():
        run(True)

    @pl.when(jnp.logical_not(bounded))
    def _():
        run(False)

    lam = _lambda_value(lam_ref, lam_init)
    for hh in range(heads):
        l = l_ref[hh]
        acc = acc_ref[hh]
        o = acc[:, :tq] / l[:, :tq] - lam * (acc[:, tq:] / l[:, tq:])
        ms = jnp.mean(o * o, axis=0, keepdims=True)
        a = o * lax.rsqrt(ms + NORM_EPS) * g_ref[...] * (1.0 - lam_init)
        o_ref[:, hh * HEAD_W:(hh + 1) * HEAD_W] = a.T.astype(o_ref.dtype)


def _attn_prompt(q, kb, vb, lamp, g_col, k_norm2, *, batch, seq, lam_init):
    n, d_att = q.shape
    nh = d_att // HEAD_W
    hp = 4 if nh % 4 == 0 else 1
    tq = min(256, seq)
    tk = min(512, seq)
    nq = seq // tq
    return pl.pallas_call(
        functools.partial(_attn_prompt_kernel, tq=tq, tk=tk, lam_init=lam_init),
        out_shape=jax.ShapeDtypeStruct((n, d_att), BF16),
        grid=(batch, nh // hp, nq),
        in_specs=[pl.BlockSpec((8, LANES), lambda b, h, i: (0, 0)),
                  pl.BlockSpec((HEAD_W, 1), lambda b, h, i: (0, 0)),
                  pl.BlockSpec((1, SUBLANES, LANES), lambda b, h, i: (b, 0, 0)),
                  pl.BlockSpec((tq, hp * HEAD_W), lambda b, h, i: (b * nq + i, h)),
                  pl.BlockSpec((seq, hp * HEAD_W), lambda b, h, i: (b, h)),
                  pl.BlockSpec((seq, hp * HEAD_W), lambda b, h, i: (b, h)),
                  pl.BlockSpec((seq, hp * HEAD_W), lambda b, h, i: (b, h))],
        out_specs=pl.BlockSpec((tq, hp * HEAD_W), lambda b, h, i: (b * nq + i, h)),
        scratch_shapes=[pltpu.VMEM((hp, 1, 2 * tq), F32), pltpu.VMEM((hp, 1, 2 * tq), F32),
                        pltpu.VMEM((hp, HEAD_W, 2 * tq), F32)],
        compiler_params=_cparams(("arbitrary", "arbitrary", "arbitrary")),
        name="attn_prompt",
    )(lamp, g_col, k_norm2, q, kb, kb, vb)


def _attn_sample_kernel(pt_ref, lam_ref, g_ref, q_ref, kn_ref, vn_ref, *rest, n_heads, pages, t_new, lam_init):
    k_pages = rest[:pages]
    v_pages = rest[pages:2 * pages]
    o_ref = rest[2 * pages]
    m_ref, l_ref, acc_ref = rest[2 * pages + 1:]
    c = pl.program_id(1)
    rows_h = 2 * t_new
    assert n_heads == SUBLANES and n_heads * rows_h == LANES

    @pl.when(c == 0)
    def _():
        m_ref[...] = jnp.full(m_ref.shape, -jnp.inf, F32)
        l_ref[...] = jnp.zeros(l_ref.shape, F32)
        acc_ref[...] = jnp.zeros(acc_ref.shape, F32)

    q = q_ref[...]
    q_all = jnp.concatenate([_stack_maps(q[:, hh * HEAD_W:(hh + 1) * HEAD_W]) for hh in range(n_heads)], axis=0)
    sub = lax.broadcasted_iota(I32, (SUBLANES, LANES), 0)
    lane = lax.broadcasted_iota(I32, (SUBLANES, LANES), 1)
    own_head = (lane // rows_h) == sub

    def update(k_rows, v_rows, valid):
        n_pos = k_rows.shape[0] // n_heads
        r = lax.dot_general(k_rows.astype(BF16), q_all, _NT, preferred_element_type=F32)
        r = r.reshape(n_pos, n_heads, LANES)
        if valid is not None:
            r = jnp.where(valid, r, -jnp.inf)
        m_old = m_ref[...]
        m_new = jnp.maximum(m_old, jnp.max(r, axis=0))
        p = jnp.exp2(r - m_new[None])
        alpha = jnp.exp2(m_old - m_new)
        l_ref[...] = alpha * l_ref[...] + jnp.sum(p, axis=0)
        p_own = jnp.where(own_head[None], p, 0.0).reshape(n_pos * n_heads, LANES).astype(BF16)
        alpha_row = jnp.sum(jnp.where(own_head, alpha, 0.0), axis=0, keepdims=True)
        pv = lax.dot_general(v_rows.astype(BF16), p_own, _TN, preferred_element_type=F32)
        acc_ref[...] = acc_ref[...] * alpha_row + pv
        m_ref[...] = m_new

    for i in range(pages):
        update(k_pages[i][0], v_pages[i][0], None)

    @pl.when(c == pl.num_programs(1) - 1)
    def _():
        t_key = lax.broadcasted_iota(I32, (t_new, n_heads, LANES), 0)
        t_query = lax.broadcasted_iota(I32, (t_new, n_heads, LANES), 2) % t_new
        update(kn_ref[0], vn_ref[0], t_key <= t_query)
        l_row = jnp.sum(jnp.where(own_head, l_ref[...], 0.0), axis=0, keepdims=True)
        o_all = (acc_ref[...] / l_row).T
        lam = _lambda_value(lam_ref, lam_init)
        for hh in range(n_heads):
            rows = o_all[hh * rows_h:(hh + 1) * rows_h]
            o = rows[:t_new] - lam * rows[t_new:]
            ms = jnp.mean(o * o, axis=1, keepdims=True)
            o_ref[:, hh * HEAD_W:(hh + 1) * HEAD_W] = o * lax.rsqrt(ms + NORM_EPS) * g_ref[...] * (1.0 - lam_init)


def _attn_sample(page_table, q, k_new, v_new, cache_k, cache_v, lamp, g, *, t_new, lam_init):
    n, d_att = q.shape
    nh = d_att // HEAD_W
    db, n_pages = page_table.shape
    n_pool, page = cache_k.shape[0], cache_k.shape[1]
    pages = min(16, n_pages)
    ck = cache_k.reshape(n_pool, page * nh, HEAD_W)
    cv = cache_v.reshape(n_pool, page * nh, HEAD_W)

    def page_spec(i):
        return pl.BlockSpec((1, page * nh, HEAD_W), lambda b, c, pt: (pt[b, c * pages + i], 0, 0))

    tok = pl.BlockSpec((t_new, d_att), lambda b, c, pt: (b, 0))
    new_rows = pl.BlockSpec((1, t_new * nh, HEAD_W), lambda b, c, pt: (b, 0, 0))
    k_new = k_new.reshape(db, t_new, nh, HEAD_W).reshape(db, t_new * nh, HEAD_W)
    v_new = v_new.reshape(db, t_new, nh, HEAD_W).reshape(db, t_new * nh, HEAD_W)
    grid_spec = pltpu.PrefetchScalarGridSpec(
        num_scalar_prefetch=1,
        grid=(db, n_pages // pages),
        in_specs=[pl.BlockSpec((8, LANES), lambda b, c, pt: (0, 0)),
                  pl.BlockSpec((1, HEAD_W), lambda b, c, pt: (0, 0)),
                  tok, new_rows, new_rows]
                 + [page_spec(i) for i in range(pages)] + [page_spec(i) for i in range(pages)],
        out_specs=tok,
        scratch_shapes=[pltpu.VMEM((nh, LANES), F32), pltpu.VMEM((nh, LANES), F32),
                        pltpu.VMEM((HEAD_W, LANES), F32)],
    )
    return pl.pallas_call(
        functools.partial(_attn_sample_kernel, n_heads=nh, pages=pages, t_new=t_new, lam_init=lam_init),
        out_shape=jax.ShapeDtypeStruct((n, d_att), F32),
        grid_spec=grid_spec,
        compiler_params=_cparams(("arbitrary", "arbitrary")),
        name="attn_sample",
    )(page_table, lamp, g, q, k_new, v_new, *([ck] * pages), *([cv] * pages))


def _pool_kernel(halo_ref, cur_ref, w_ref, scale_ref, o_ref, ext_ref, *, pos0, tile_pos, zero_first_halo):
    i = pl.program_id(1)
    t = cur_ref.shape[1]
    halo = halo_ref[0]
    if zero_first_halo:
        halo = jnp.where(i == 0, 0.0, halo)
    cur = cur_ref[0]
    ext_ref[0:POOL_HALO, :] = halo
    ext_ref[POOL_HALO:POOL_HALO + t, :] = cur
    pos = pos0 + i * tile_pos + lax.broadcasted_iota(I32, (t, 1), 0)
    gw = cur.shape[1] // len(POOL_WINDOWS)
    for gi, w in enumerate(POOL_WINDOWS):
        cols = slice(gi * gw, (gi + 1) * gw)
        total = cur[:, cols]
        for j in range(1, w):
            total = total + ext_ref[POOL_HALO - j:POOL_HALO - j + t, cols]
        cnt = jnp.minimum(pos + 1, w).astype(F32)
        dlt = total / cnt - cur[:, cols]
        y = jnp.dot(dlt.astype(BF16), w_ref[gi], preferred_element_type=F32)
        o_ref[0, :, cols] = (y * scale_ref[:, cols]).astype(o_ref.dtype)


def _pool_mix(u3, w_pool_b, scale, *, tile, cur_block0, pos0, zero_first_halo):
    b, rows, c = u3.shape
    n_tiles = (rows - cur_block0 * tile) // tile
    ng, gw = w_pool_b.shape[0], w_pool_b.shape[1]
    assert (cur_block0 * tile) % POOL_HALO == 0 and (tile % POOL_HALO == 0 or n_tiles == 1)

    def halo_map(bi, i):
        return (bi, jnp.maximum(((cur_block0 + i) * tile) // POOL_HALO - 1, 0), 0)

    return pl.pallas_call(
        functools.partial(_pool_kernel, pos0=pos0, tile_pos=tile, zero_first_halo=zero_first_halo),
        out_shape=jax.ShapeDtypeStruct((b, n_tiles * tile, c), BF16),
        grid=(b, n_tiles),
        in_specs=[pl.BlockSpec((1, POOL_HALO, c), halo_map),
                  pl.BlockSpec((1, tile, c), lambda bi, i: (bi, cur_block0 + i, 0)),
                  pl.BlockSpec((ng, gw, gw), lambda bi, i: (0, 0, 0)),
                  pl.BlockSpec((1, c), lambda bi, i: (0, 0))],
        out_specs=pl.BlockSpec((1, tile, c), lambda bi, i: (bi, i, 0)),
        scratch_shapes=[pltpu.VMEM((POOL_HALO + tile, c), F32)],
        compiler_params=_cparams(("arbitrary", "arbitrary")),
        name="pool_mix",
    )(u3, u3, w_pool_b, scale)


def _split_bf16(x):
    hi = x.astype(BF16)
    lo = (x - hi.astype(F32)).astype(BF16)
    return hi, lo


def _round_up_f32(x, m):
    return jnp.floor((x + (m - 1.0)) * (1.0 / m)) * m


def _mix_out_kernel(*refs, n_experts, aliased):
    (x_ref, a_ref, pm_ref, gate_ref, shift_ref, scale_ref, g_ref, wo_ref, wr_ref, br_ref) = refs[:10]
    x2_ref, h2_ref, pk_ref, gt_ref, tc_ref = refs[10 + aliased:]
    d_att = a_ref.shape[1]
    mix = (jnp.dot(a_ref[...].astype(BF16), wo_ref[0:d_att, :], preferred_element_type=F32)
           + jnp.dot(pm_ref[...], wo_ref[d_att:, :], preferred_element_type=F32))
    x2 = x_ref[...] + gate_ref[0] * mix
    x2_ref[...] = x2
    h2 = _modulated_norm(x2, g_ref[...], shift_ref[0], scale_ref[0])
    h2_ref[...] = h2.astype(BF16)
    hh, hl = _split_bf16(h2)
    wh, wl = _split_bf16(wr_ref[...])
    logits = (jnp.dot(hh, wh, preferred_element_type=F32) + jnp.dot(hl, wh, preferred_element_type=F32)
              + jnp.dot(hh, wl, preferred_element_type=F32)) + br_ref[...]
    t = logits.shape[0]
    lane = lax.broadcasted_iota(I32, logits.shape, 1)
    lanef = lane.astype(F32)
    work = jnp.where(lane < n_experts, logits, -jnp.inf)
    vals, ids = [], []
    for _ in range(TOP_K):
        mx = jnp.max(work, axis=1, keepdims=True)
        ix = jnp.min(jnp.where(work == mx, lanef, float(LANES)), axis=1, keepdims=True)
        vals.append(mx)
        ids.append(ix)
        work = jnp.where(lanef == ix, -jnp.inf, work)
    es = [jnp.exp(v - vals[0]) for v in vals]
    den = es[0]
    for e in es[1:]:
        den = den + e
    sel = jnp.zeros(logits.shape, F32)
    for k in range(TOP_K):
        sel = jnp.where(lanef == ids[k], 1.0, sel)
    r = lax.broadcasted_iota(I32, (t, t), 0)
    c = lax.broadcasted_iota(I32, (t, t), 1)
    earlier = jnp.where(c < r, 1.0, 0.0).astype(BF16)
    local_rank = jnp.dot(earlier, sel.astype(BF16), preferred_element_type=F32)
    cnt = jnp.sum(sel, axis=0, keepdims=True)
    cnt8 = jnp.broadcast_to(_round_up_f32(cnt, float(SUBLANES)), (SUBLANES, LANES))
    er = lax.broadcasted_iota(I32, (LANES, LANES), 0)
    ec = lax.broadcasted_iota(I32, (LANES, LANES), 1)
    before = jnp.where(er < ec, 1.0, 0.0).astype(BF16)
    chunk_off = jnp.dot(cnt8.astype(BF16), before, preferred_element_type=F32)[0:1, :]
    pos = local_rank + chunk_off
    pk_out = jnp.zeros(logits.shape, F32)
    gt_out = jnp.zeros(logits.shape, F32)
    for k in range(TOP_K):
        pk = jnp.sum(jnp.where(lanef == ids[k], pos, 0.0), axis=1, keepdims=True)
        pk_out = jnp.where(lane == k, pk, pk_out)
        gt_out = jnp.where(lane == k, es[k] / den, gt_out)
    pk_ref[...] = pk_out
    gt_ref[...] = gt_out
    tc_ref[0] = jnp.broadcast_to(cnt, (SUBLANES, LANES))


def _mix_out(x, a, pm, gate, shift, scale, g, w_out_b, w_router_p, b_router_p, *, n_total, tile0,
             tiles_per_seq, n_experts, prev=None):
    n, d = x.shape
    tm = min(TOK_TILE, n)
    d_att = a.shape[1]
    mod_rows = gate.shape[1]
    mod_spec = pl.BlockSpec((1, mod_rows, d), lambda i: (i // tiles_per_seq, 0, 0))
    tok = lambda c: pl.BlockSpec((tm, c), lambda i: (i, 0))
    out_tok = lambda c: pl.BlockSpec((tm, c), lambda i: (tile0 + i, 0))
    out_shape = (jax.ShapeDtypeStruct((n_total, d), F32), jax.ShapeDtypeStruct((n_total, d), BF16),
                 jax.ShapeDtypeStruct((n_total, LANES), F32), jax.ShapeDtypeStruct((n_total, LANES), F32),
                 jax.ShapeDtypeStruct((n_total // tm, SUBLANES, LANES), F32))
    in_specs = [tok(d), tok(d_att), tok(pm.shape[1]), mod_spec, mod_spec, mod_spec,
                pl.BlockSpec((1, d), lambda i: (0, 0)),
                pl.BlockSpec(w_out_b.shape, lambda i: (0, 0)),
                pl.BlockSpec(w_router_p.shape, lambda i: (0, 0)),
                pl.BlockSpec((1, LANES), lambda i: (0, 0))]
    args = [x, a, pm, gate, shift, scale, g, w_out_b, w_router_p, b_router_p]
    aliases = {}
    n_alias = 0
    if prev is not None:
        n_alias = len(prev)
        in_specs += [pl.BlockSpec(memory_space=pl.ANY)] * n_alias
        aliases = {len(args) + k: k for k in range(n_alias)}
        args += list(prev)
    return pl.pallas_call(
        functools.partial(_mix_out_kernel, n_experts=n_experts, aliased=n_alias),
        out_shape=out_shape,
        grid=(n // tm,),
        in_specs=in_specs,
        out_specs=(out_tok(d), out_tok(d), out_tok(LANES), out_tok(LANES),
                   pl.BlockSpec((1, SUBLANES, LANES), lambda i: (tile0 + i, 0, 0))),
        input_output_aliases=aliases,
        compiler_params=_cparams(("arbitrary",)),
        name="mix_out_router",
    )(*args)


def _chunk_sizes(tile):
    sizes = []
    s = tile
    while s >= SUBLANES:
        sizes.append(s)
        s //= 2
    return sizes


def _for_each_chunk_piece(cnt_ref, base_ref, i, n_experts, tile, fn):
    assert n_experts % 2 == 0

    def per_expert(e, off, lane):
        l8 = (cnt_ref[i * n_experts + e] + (SUBLANES - 1)) & (-SUBLANES)
        dst0 = base_ref[i * n_experts + e]
        done = jnp.int32(0)
        for sz in _chunk_sizes(tile):
            take = l8 & sz

            @pl.when(take != 0)
            def _(done=done, sz=sz):
                fn(pl.multiple_of(off + done, SUBLANES), pl.multiple_of(dst0 + done, SUBLANES), sz, lane)

            done = done + take
        return off + l8

    def per_pair(e2, off):
        return per_expert(2 * e2 + 1, per_expert(2 * e2, off, 0), 1)

    return lax.fori_loop(0, n_experts // 2, per_pair, jnp.int32(0))


def _pack_halves(x):
    c = x.shape[1] // 2
    lo = lax.bitcast_convert_type(x[:, :c], U32) >> 16
    hi = lax.bitcast_convert_type(x[:, c:], U32) & jnp.uint32(0xFFFF0000)
    return lo | hi


def _unpack_halves(w):
    lo = lax.bitcast_convert_type(w << 16, F32).astype(BF16)
    hi = lax.bitcast_convert_type(w & jnp.uint32(0xFFFF0000), F32).astype(BF16)
    return lo, hi


def _one_hot_rows(pk, rows, weights=None):
    t = pk.shape[0]
    col = lax.broadcasted_iota(I32, (t, rows), 1).astype(F32)
    out = jnp.zeros((t, rows), F32)
    for k in range(TOP_K):
        w = 1.0 if weights is None else weights[:, k:k + 1]
        out = jnp.where(col == pk[:, k:k + 1], w, out)
    return out.astype(BF16)


def _dispatch_kernel(cnt_ref, base_ref, tail_ref, has_ref, h_ref, pk_ref, xs_hbm, sorted_ref, zero_ref, sem,
                     zero_sem, *, n_experts, tile):
    i = pl.program_id(0)
    last = pl.num_programs(0) - 1

    def zero_copy(e):
        return pltpu.make_async_copy(
            zero_ref, xs_hbm.at[pl.ds(pl.multiple_of(tail_ref[e], EXP_TILE), EXP_TILE)], zero_sem)

    @pl.when(i == 0)
    def _():
        zero_ref[...] = jnp.zeros(zero_ref.shape, U32)
        for e in range(n_experts):
            @pl.when(has_ref[e] == 1)
            def _(e=e):
                zero_copy(e).start()
        for e in range(n_experts):
            @pl.when(has_ref[e] == 1)
            def _(e=e):
                zero_copy(e).wait()

    def copy(step, src, dst, sz):
        slot = step & 1
        return pltpu.make_async_copy(sorted_ref.at[slot, pl.ds(src, sz)], xs_hbm.at[pl.ds(dst, sz)], sem.at[slot])

    onehot = _one_hot_rows(pk_ref[...], sorted_ref.shape[1])
    sorted_ref[i & 1] = _pack_halves(lax.dot_general(onehot, h_ref[...], _TN, preferred_element_type=F32))
    _for_each_chunk_piece(cnt_ref, base_ref, i, n_experts, tile,
                          lambda s, d, sz, lane: copy(i, s, d, sz).start(priority=lane))

    @pl.when(i > 0)
    def _():
        _for_each_chunk_piece(cnt_ref, base_ref, i - 1, n_experts, tile,
                              lambda s, d, sz, lane: copy(i - 1, s, d, sz).wait())

    @pl.when(i == last)
    def _():
        _for_each_chunk_piece(cnt_ref, base_ref, i, n_experts, tile,
                              lambda s, d, sz, lane: copy(i, s, d, sz).wait())


def _sorted_rows(tile, n_experts):
    rows = tile * TOP_K + n_experts * (SUBLANES - 1)
    return -(-rows // LANES) * LANES


def _dispatch(h2, pk4, cnt_flat, base_flat, tail_rows, has_tile, *, r_max, n_experts):
    n, d = h2.shape
    tile = min(TOK_TILE, n)
    grid_spec = pltpu.PrefetchScalarGridSpec(
        num_scalar_prefetch=4,
        grid=(n // tile,),
        in_specs=[pl.BlockSpec((tile, d), lambda i, *_: (i, 0)),
                  pl.BlockSpec((tile, LANES), lambda i, *_: (i, 0))],
        out_specs=pl.BlockSpec(memory_space=pl.ANY),
        scratch_shapes=[pltpu.VMEM((2, _sorted_rows(tile, n_experts), d // 2), U32),
                        pltpu.VMEM((EXP_TILE, d // 2), U32), pltpu.SemaphoreType.DMA((2,)),
                        pltpu.SemaphoreType.DMA],
    )
    return pl.pallas_call(
        functools.partial(_dispatch_kernel, n_experts=n_experts, tile=tile),
        out_shape=jax.ShapeDtypeStruct((r_max, d // 2), U32),
        grid_spec=grid_spec,
        compiler_params=_cparams(("arbitrary",)),
        name="dispatch_rows",
    )(cnt_flat, base_flat, tail_rows, has_tile, h2, pk4)


def _expert_kernel(st_e, st_j, st_n, g_t0, g_gs, g_valid,
                   xs_hbm, wg_ref, wu_ref, wd_ref, bg_ref, bu_ref, bd_ref, ys_hbm,
                   xbuf, actbuf, wgb, wub, wdb, stage_in, stage_out, pending, sem_in, sem_out,
                   *, n_j, n_n, tm, tf):
    s = pl.program_id(0)
    steps = n_j + n_n
    q = s // steps
    ph = s - q * steps
    valid = g_valid[q] == 1
    gs = g_gs[q]
    row0 = g_t0[q] * tm

    @pl.when(s == 0)
    def _():
        pending[0] = 0
        pending[1] = 0

    def in_copy(first_row, i):
        slot = i & 1
        return pltpu.make_async_copy(xs_hbm.at[pl.ds(pl.multiple_of(first_row + i * tm, tm), tm)],
                                     stage_in.at[slot], sem_in.at[slot])

    def stage_to_xbuf(i):
        rows = pl.ds(pl.multiple_of(i * tm, tm), tm)
        lo, hi = _unpack_halves(stage_in[i & 1])
        half = lo.shape[1]
        xbuf[rows, 0:half] = lo
        xbuf[rows, half:2 * half] = hi

    @pl.when(s == 0)
    def _():
        in_copy(row0, 0).start()

        def load(i, carry):
            @pl.when(i + 1 < gs)
            def _():
                in_copy(row0, i + 1).start()

            in_copy(row0, i).wait()
            stage_to_xbuf(i)
            return carry

        lax.fori_loop(0, gs, load, 0)

    q_next = jnp.minimum(q + 1, g_gs.shape[0] - 1)
    gs_next = jnp.where(q + 1 < g_gs.shape[0], g_gs[q_next], 0)
    row0_next = g_t0[q_next] * tm
    fetch_tiles = [2 * (ph - n_j), 2 * (ph - n_j) + 1]

    n_pairs = gs // 2
    has_tail = (gs & 1) == 1
    tail_row = pl.multiple_of(n_pairs * 2 * tm, tm)

    @pl.when(jnp.logical_and(valid, ph < n_j))
    def _():
        wgb[...] = wg_ref[0].astype(BF16)
        wub[...] = wu_ref[0].astype(BF16)

        def tile(r0, size):
            rows = pl.ds(r0, size)
            x = xbuf[rows, :]
            gate = jnp.dot(x, wgb[...], preferred_element_type=F32) + bg_ref[0]
            up = jnp.dot(x, wub[...], preferred_element_type=F32) + bu_ref[0]
            gate = jnp.minimum(gate, SWIGLU_LIMIT)
            up = jnp.clip(up, -SWIGLU_LIMIT, SWIGLU_LIMIT)
            act = (up + 1.0) * (gate * jax.nn.sigmoid(SWIGLU_ALPHA * gate))
            actbuf[ph, rows, :] = act.astype(BF16)

        def pair(i, carry):
            tile(pl.multiple_of(i * 2 * tm, 2 * tm), 2 * tm)
            return carry

        lax.fori_loop(0, n_pairs, pair, 0)

        @pl.when(has_tail)
        def _():
            tile(tail_row, tm)

    @pl.when(jnp.logical_and(valid, ph >= n_j))
    def _():
        for t in fetch_tiles:
            @pl.when(t < gs_next)
            def _(t=t):
                in_copy(row0_next, t).start()

        wdb[...] = wd_ref[0].astype(BF16)
        col0 = pl.multiple_of((ph - n_j) * (tf // 2), tf // 2)

        def out_copy(slot, r0, size):
            return pltpu.make_async_copy(
                stage_out.at[slot, pl.ds(0, size)],
                ys_hbm.at[pl.ds(pl.multiple_of(row0 + r0, tm), size), pl.ds(col0, tf // 2)],
                sem_out.at[slot])

        def wait_slot(slot):
            for n_tiles in (1, 2):
                @pl.when(pending[slot] == n_tiles)
                def _(n_tiles=n_tiles):
                    out_copy(slot, 0, n_tiles * tm).wait()
                    pending[slot] = 0

        def tile(r0, size, slot):
            rows = pl.ds(r0, size)
            wait_slot(slot)
            y = bd_ref[0] + jnp.dot(actbuf[0, rows, :], wdb[0:tf, :], preferred_element_type=F32)
            for j in range(1, n_j):
                y = y + jnp.dot(actbuf[j, rows, :], wdb[j * tf:(j + 1) * tf, :], preferred_element_type=F32)
            stage_out[slot, 0:size] = _pack_halves(y.astype(BF16).astype(F32))
            out_copy(slot, r0, size).start()
            pending[slot] = size // tm

        def pair(i, carry):
            tile(pl.multiple_of(i * 2 * tm, 2 * tm), 2 * tm, i & 1)
            return carry

        lax.fori_loop(0, n_pairs, pair, 0)

        @pl.when(has_tail)
        def _():
            tile(tail_row, tm, n_pairs & 1)

        for t in fetch_tiles:
            @pl.when(t < gs_next)
            def _(t=t):
                in_copy(row0_next, t).wait()
                stage_to_xbuf(t)

        @pl.when(s == pl.num_programs(0) - 1)
        def _():
            wait_slot(0)
            wait_slot(1)


def _expert_ffn(tables, xs, w_gate, b_gate, w_up, b_up, w_down, b_down, *, n_steps):
    st_e, st_j, st_n, g_t0, g_gs, g_valid = tables
    r_max = xs.shape[0]
    n_exp, d, d_ff = w_gate.shape
    tf = min(FF_TILE, d_ff, d)
    n_j = d_ff // tf
    n_n = d // tf
    tm = EXP_TILE
    rows_g = EXP_GROUP * tm
    assert 2 * n_n >= EXP_GROUP, "phase-2 steps fetch two row tiles of the next group each"
    grid_spec = pltpu.PrefetchScalarGridSpec(
        num_scalar_prefetch=6,
        grid=(n_steps,),
        in_specs=[pl.BlockSpec(memory_space=pl.ANY),
                  pl.BlockSpec((1, d, tf), lambda s, e, j, n, *_: (e[s], 0, j[s])),
                  pl.BlockSpec((1, d, tf), lambda s, e, j, n, *_: (e[s], 0, j[s])),
                  pl.BlockSpec((1, d_ff, tf), lambda s, e, j, n, *_: (e[s], 0, n[s])),
                  pl.BlockSpec((1, 1, tf), lambda s, e, j, n, *_: (e[s], 0, j[s])),
                  pl.BlockSpec((1, 1, tf), lambda s, e, j, n, *_: (e[s], 0, j[s])),
                  pl.BlockSpec((1, 1, tf), lambda s, e, j, n, *_: (e[s], 0, n[s]))],
        out_specs=pl.BlockSpec(memory_space=pl.ANY),
        scratch_shapes=[pltpu.VMEM((rows_g, d), BF16),
                        pltpu.VMEM((n_j, rows_g, tf), BF16),
                        pltpu.VMEM((d, tf), BF16), pltpu.VMEM((d, tf), BF16), pltpu.VMEM((d_ff, tf), BF16),
                        pltpu.VMEM((2, tm, d // 2), U32), pltpu.VMEM((2, 2 * tm, tf // 2), U32),
                        pltpu.SMEM((2,), I32),
                        pltpu.SemaphoreType.DMA((2,)), pltpu.SemaphoreType.DMA((2,))],
    )
    return pl.pallas_call(
        functools.partial(_expert_kernel, n_j=n_j, n_n=n_n, tm=tm, tf=tf),
        out_shape=jax.ShapeDtypeStruct((r_max, d // 2), U32),
        grid_spec=grid_spec,
        compiler_params=_cparams(("arbitrary",)),
        name="expert_ffn",
    )(st_e, st_j, st_n, g_t0, g_gs, g_valid, xs, w_gate, w_up, w_down,
      b_gate.reshape(n_exp, 1, d_ff), b_up.reshape(n_exp, 1, d_ff), b_down.reshape(n_exp, 1, d))


def _combine_kernel(cnt_ref, base_ref, ys_hbm, x2_ref, pk_ref, gt_ref, gate_p_ref, gate_s_ref, g_ref,
                    yp_ref, ysm_ref, rows_ref, sem, *, n_experts, tile, n_prompt_tiles, out_tile):
    i = pl.program_id(0)
    last = pl.num_programs(0) - 1

    def copy(step, dst, src, sz):
        slot = step & 1
        return pltpu.make_async_copy(ys_hbm.at[pl.ds(src, sz)], rows_ref.at[slot, pl.ds(dst, sz)], sem.at[slot])

    def fetch(step):
        _for_each_chunk_piece(cnt_ref, base_ref, step, n_experts, tile,
                              lambda s, d, sz, lane: copy(step, s, d, sz).start(priority=lane))

    @pl.when(i == 0)
    def _():
        rows_ref[...] = jnp.zeros(rows_ref.shape, U32)
        fetch(i)

    @pl.when(i < last)
    def _():
        fetch(i + 1)

    _for_each_chunk_piece(cnt_ref, base_ref, i, n_experts, tile,
                          lambda s, d, sz, lane: copy(i, s, d, sz).wait())
    weights =_one_hot_rows(pk_ref[...], rows_ref.shape[1], gt_ref[...])
    lo, hi = _unpack_halves(rows_ref[i & 1])
    y_lo = jnp.dot(weights, lo, preferred_element_type=F32)
    y_hi = jnp.dot(weights, hi, preferred_element_type=F32)
    half = out_tile // 2
    pieces = []
    for n in range(x2_ref.shape[1] // out_tile):
        pieces += [y_lo[:, n * half:(n + 1) * half], y_hi[:, n * half:(n + 1) * half]]
    y = jnp.concatenate(pieces, axis=1)
    is_prompt = i < n_prompt_tiles
    gate = jnp.where(is_prompt, gate_p_ref[0], gate_s_ref[...])
    x3 = x2_ref[...] + gate * y
    ms = jnp.mean(x3 * x3, axis=-1, keepdims=True)
    out = x3 * lax.rsqrt(ms + NORM_EPS) * g_ref[...]

    @pl.when(is_prompt)
    def _():
        yp_ref[...] = out

    @pl.when(jnp.logical_not(is_prompt))
    def _():
        ysm_ref[...] = out


def _combine(ys, cnt_flat, base_flat, x2, pk4, gt, gate_p, gate_s, final_g, *, n_prompt, tiles_per_seq, n_experts,
             out_tile):
    n, d = x2.shape
    tile = min(TOK_TILE, n_prompt)
    n_s = n - n_prompt
    assert n_s == tile and n_prompt % tile == 0
    npt = n_prompt // tile
    grid_spec = pltpu.PrefetchScalarGridSpec(
        num_scalar_prefetch=2,
        grid=(n // tile,),
        in_specs=[pl.BlockSpec(memory_space=pl.ANY),
                  pl.BlockSpec((tile, d), lambda i, *_: (i, 0)),
                  pl.BlockSpec((tile, LANES), lambda i, *_: (i, 0)),
                  pl.BlockSpec((tile, LANES), lambda i, *_: (i, 0)),
                  pl.BlockSpec((1, 1, d), lambda i, *_: (jnp.minimum(i, npt - 1) // tiles_per_seq, 0, 0)),
                  pl.BlockSpec((tile, d), lambda i, *_: (0, 0)),
                  pl.BlockSpec((1, d), lambda i, *_: (0, 0))],
        out_specs=(pl.BlockSpec((tile, d), lambda i, *_: (jnp.minimum(i, npt - 1), 0)),
                   pl.BlockSpec((tile, d), lambda i, *_: (0, 0))),
        scratch_shapes=[pltpu.VMEM((2, _sorted_rows(tile, n_experts), d // 2), U32),
                        pltpu.SemaphoreType.DMA((2,))],
    )
    return pl.pallas_call(
        functools.partial(_combine_kernel, n_experts=n_experts, tile=tile, n_prompt_tiles=npt, out_tile=out_tile),
        out_shape=(jax.ShapeDtypeStruct((n_prompt, d), F32), jax.ShapeDtypeStruct((n_s, d), F32)),
        grid_spec=grid_spec,
        compiler_params=_cparams(("arbitrary",)),
        name="combine_norm",
    )(cnt_flat, base_flat, ys, x2, pk4, gt, gate_p, gate_s, final_g)


def _expert_tables(tile_cnt, *, tile, n_j, n_n):
    n_tiles, n_experts = tile_cnt.shape
    tm, grp = EXP_TILE, EXP_GROUP
    cnt8 = (tile_cnt + (SUBLANES - 1)) // SUBLANES * SUBLANES
    rows_e = jnp.sum(cnt8, axis=0)
    max_rows = n_tiles * tile * TOP_K + n_experts * n_tiles * (SUBLANES - 1)
    t_max = -(-max_rows // tm) + n_experts
    ng_max = n_experts + t_max // grp
    ntile = (rows_e + tm - 1) // tm
    tile_start = jnp.cumsum(ntile) - ntile
    pstart = tile_start * tm
    base = pstart[None, :] + jnp.cumsum(cnt8, axis=0) - cnt8
    ng = (ntile + grp - 1) // grp
    cg = jnp.cumsum(ng)
    n_groups = cg[-1]
    q = jnp.arange(ng_max, dtype=I32)
    eq = jnp.minimum(jnp.sum((q[:, None] >= cg[None, :]).astype(I32), axis=1), n_experts - 1)
    onehot = eq[:, None] == jnp.arange(n_experts, dtype=I32)[None, :]

    def pick(v):
        return jnp.sum(jnp.where(onehot, v[None, :], 0), axis=1)

    lg = q - (pick(cg) - pick(ng))
    t0 = pick(tile_start) + lg * grp
    gs = jnp.clip(pick(ntile) - lg * grp, 0, grp)
    valid = q < n_groups
    is_last = q == jnp.maximum(n_groups - 1, 0)
    eq = jnp.where(valid, eq, jnp.sum(jnp.where(is_last, eq, 0)))
    t0 = jnp.where(valid, t0, jnp.sum(jnp.where(is_last, t0, 0)))
    gs = jnp.where(valid, gs, 0)
    steps = n_j + n_n
    ph = jnp.tile(jnp.arange(steps, dtype=I32), ng_max)
    vs = jnp.repeat(valid, steps)
    st_e = jnp.repeat(eq, steps)
    st_j = jnp.where(vs, jnp.minimum(ph, n_j - 1), n_j - 1)
    st_n = jnp.where(vs, jnp.maximum(ph - n_j, 0), n_n - 1)
    tail = jnp.maximum(pstart + (ntile - 1) * tm, 0)
    has = (ntile > 0).astype(I32)
    tabs = tuple(a.astype(I32) for a in (st_e, st_j, st_n, t0, gs, valid))
    return (tabs, tile_cnt.reshape(-1).astype(I32), base.reshape(-1).astype(I32), tail.astype(I32), has,
            t_max * tm, (n_groups * steps).astype(I32))


def _rope_tables(pos):
    inv = 1.0 / (ROPE_THETA ** (jnp.arange(0, HEAD_DIM, 2, dtype=F32) / HEAD_DIM))
    ang = pos.astype(F32)[:, None] * inv[None, :]
    reps = LANES // (HEAD_DIM // 2)
    return jnp.tile(jnp.cos(ang), (1, reps)), jnp.tile(jnp.sin(ang), (1, reps))


def kernel(x_prompt, x_sample, cache_k, cache_v, state_pool, page_table, c_prompt, c_sample, w_ada, b_ada, norm1_g, norm2_g, w_in, lam_q1, lam_k1, lam_q2, lam_k2, subln_g, w_pool, pool_scale, w_out, w_router, b_router, w_gate, b_gate, w_up, b_up, w_down, b_down, final_g):
    B, S, D = x_prompt.shape
    DB, T, _ = x_sample.shape
    depth = w_ada.shape[0]
    page = cache_k.shape[2]
    past = page_table.shape[1] * page
    d_att = D // 2
    n_heads = d_att // HEAD_W
    d_pool = w_in.shape[2] - 3 * d_att
    n_experts = w_router.shape[2]
    state_len = state_pool.shape[2]
    n_p, n_s = B * S, DB * T
    n_all = n_p + n_s
    tm = min(TOK_TILE, n_p)
    tiles_per_seq = S // tm
    assert depth == 1, "single-layer step"
    assert n_s == tm and S % tm == 0 and T % 8 == 0 and state_len < POOL_HALO <= tm

    cos_p, sin_p = _rope_tables(jnp.arange(S))
    cos_s, sin_s = _rope_tables(jnp.tile(past + jnp.arange(T), DB))

    l = 0
    lam_init = 0.8 - 0.6 * math.exp(-0.3 * l)
    lamp = jnp.zeros((8, LANES), F32)
    for r, vec in enumerate((lam_q1[l], lam_k1[l], lam_q2[l], lam_k2[l])):
        lamp = lamp.at[r, :HEAD_DIM].set(vec.astype(F32))
    subg = subln_g[l].reshape(1, HEAD_W)

    rows_c = -(-(B + DB) // 8) * 8
    c_all = jnp.zeros((rows_c, D), F32).at[:B].set(c_prompt).at[B:B + DB].set(c_sample)
    m_all = _adaln(c_all, w_ada[l], b_ada[l])
    mods_p = [m_all[:B, k * D:(k + 1) * D].reshape(B, 1, D) for k in range(N_ADA)]
    mods_s = [jnp.repeat(m_all[B:B + DB, k * D:(k + 1) * D], T, axis=0).reshape(1, n_s, D) for k in range(N_ADA)]

    w_in_b = _cast_bf16(w_in[l], 256)
    w_out_b = _cast_bf16(w_out[l], 256)
    ng, gw = w_pool.shape[1], w_pool.shape[2]
    w_pool_b = _cast_bf16(w_pool[l].reshape(ng * gw, gw), ng * gw).reshape(ng, gw, gw)
    g1 = norm1_g[l].reshape(1, D)
    g2 = norm2_g[l].reshape(1, D)
    pscale = pool_scale[l].reshape(1, d_pool)
    w_router_p = jnp.zeros((D, LANES), F32).at[:, :n_experts].set(w_router[l])
    b_router_p = jnp.zeros((1, LANES), F32).at[0, :n_experts].set(b_router[l].astype(F32))

    xp = x_prompt.reshape(n_p, D)
    q_p, k_p, v_p, u_p, kb_p, vb_p, kn_p = _project(xp, mods_p[0], mods_p[1], g1, w_in_b, cos_p, sin_p,
                                                    tiles_per_seq=tiles_per_seq, pos_tiles=tiles_per_seq)
    kn_seq = jnp.max(kn_p.reshape(B, tiles_per_seq, SUBLANES, LANES), axis=1)
    a_p = _attn_prompt(q_p, kb_p, vb_p, lamp, subg.reshape(HEAD_W, 1), kn_seq, batch=B, seq=S, lam_init=lam_init)
    pm_p = _pool_mix(u_p.reshape(B, S, d_pool), w_pool_b, pscale, tile=tm, cur_block0=0, pos0=0,
                     zero_first_halo=True).reshape(n_p, d_pool)

    xs_tok = x_sample.reshape(n_s, D)
    q_s, k_s, v_s, u_s, _, _, _ = _project(xs_tok, mods_s[0], mods_s[1], g1, w_in_b, cos_s, sin_s,
                                           tiles_per_seq=1, pos_tiles=1)
    a_s = _attn_sample(page_table, q_s, k_s, v_s, cache_k[l], cache_v[l], lamp, subg, t_new=T, lam_init=lam_init)
    u_ext = jnp.concatenate([jnp.zeros((DB, POOL_HALO - state_len, d_pool), F32),
                             state_pool[l].astype(F32), u_s.reshape(DB, T, d_pool)], axis=1)
    pm_s = _pool_mix(u_ext, w_pool_b, pscale, tile=T, cur_block0=POOL_HALO // T, pos0=past,
                     zero_first_halo=False).reshape(n_s, d_pool)

    outs = _mix_out(xp, a_p, pm_p, mods_p[2], mods_p[3], mods_p[4], g2, w_out_b, w_router_p, b_router_p,
                    n_total=n_all, tile0=0, tiles_per_seq=tiles_per_seq, n_experts=n_experts)
    x2, h2, pk4, gt4, tcnt = _mix_out(xs_tok, a_s, pm_s, mods_s[2], mods_s[3], mods_s[4], g2, w_out_b, w_router_p,
                                      b_router_p, n_total=n_all, tile0=n_p // tm, tiles_per_seq=1,
                                      n_experts=n_experts, prev=outs)

    d_ff = w_gate.shape[3]
    tf = min(FF_TILE, d_ff, D)
    tile_cnt = tcnt[:, 0, :n_experts].astype(I32)
    tabs, cnt_flat, base_flat, tail, has, r_max, n_steps = _expert_tables(
        tile_cnt, tile=tm, n_j=d_ff // tf, n_n=D // tf)
    xs_rows = _dispatch(h2, pk4, cnt_flat, base_flat, tail, has, r_max=r_max, n_experts=n_experts)
    ys_rows = _expert_ffn(tabs, xs_rows, w_gate[l], b_gate[l], w_up[l], b_up[l], w_down[l], b_down[l],
                          n_steps=n_steps)
    y_p, y_s = _combine(ys_rows, cnt_flat, base_flat, x2, pk4, gt4, mods_p[5], mods_s[5].reshape(n_s, D),
                        final_g.reshape(1, D), n_prompt=n_p, tiles_per_seq=tiles_per_seq, n_experts=n_experts,
                        out_tile=tf)

    n_pages_p = S // page
    k_prompt = k_p.reshape(1, B, n_pages_p, page, n_heads, HEAD_W)
    v_prompt = v_p.reshape(1, B, n_pages_p, page, n_heads, HEAD_W)
    pool_prompt = u_p.reshape(B, S, d_pool)[:, S - state_len:][None]
    k_sample = k_s.reshape(1, DB, T, n_heads, HEAD_W)
    v_sample = v_s.reshape(1, DB, T, n_heads, HEAD_W)
    pool_sample = u_ext[:, -state_len:][None]
    return (y_p.reshape(B, S, D), y_s.reshape(DB, T, D), k_prompt, v_prompt, pool_prompt,
            k_sample, v_sample, pool_sample)
```
